```python
import jax, jax.numpy as jnp
from jax import lax
import numpy as np

D_MODEL = 1024
BATCH = 16
SEQ = 2048
DEPTH = 1
DEC_BATCH = 16
DEC_SEQ = 4096
PAST_LEN = 128

GRID_W = 64
MIX_WIDTH = D_MODEL
NA_WIDTH = MIX_WIDTH // 2
HEAD_DIM = 64
NA_HEADS = NA_WIDTH // HEAD_DIM
FN_WIDTH = MIX_WIDTH - NA_WIDTH
FN_GROUPS = 4
FN_GROUP_DIM = FN_WIDTH // FN_GROUPS
IN_WIDTH = 3 * NA_WIDTH + FN_WIDTH
WIN_ROWS_MAX = 8
WIN_COLS = 16
N_EXPERTS = 32
TOP_K = 4
D_FF = D_MODEL
SWIGLU_ALPHA = 1.702
SWIGLU_LIMIT = 7.0
MOE_BLOCK = 256
EPS = 1e-6

kernel_name = "hymba_natten_fnet_moe_encoder"


def rms_norm(x, g):
    xf = x.astype(jnp.float32)
    y = xf * lax.rsqrt(jnp.mean(xf * xf, axis=-1, keepdims=True) + EPS)
    return (y * g.astype(jnp.float32)).astype(x.dtype)


def na_tables(rows, rpb):
    kh = min(WIN_ROWS_MAX, rows)
    r = jnp.arange(rows, dtype=jnp.int32)
    c = jnp.arange(GRID_W, dtype=jnp.int32)
    kr = jnp.clip(r - kh // 2, 0, rows - kh)[:, None] + jnp.arange(kh, dtype=jnp.int32)[None, :]
    kc = jnp.clip(c - WIN_COLS // 2, 0, GRID_W - WIN_COLS)[:, None] + jnp.arange(WIN_COLS, dtype=jnp.int32)[None, :]
    idx = (kr[:, None, :, None] * GRID_W + kc[None, :, None, :]).reshape(rows, GRID_W, kh * WIN_COLS)
    ri = (kr - r[:, None] + (WIN_ROWS_MAX - 1))[:, None, :, None]
    ci = (kc - c[:, None] + (WIN_COLS - 1))[None, :, None, :]
    bias = rpb[:, ri, ci]
    bias = bias.transpose(1, 0, 2, 3, 4).reshape(rows, rpb.shape[0], GRID_W, kh * WIN_COLS)
    return idx, bias


def neighbourhood_attention(q, k, v, rpb):
    bsz, L, H, dh = q.shape
    rows = L // GRID_W
    idx, bias = na_tables(rows, rpb)
    q_rows = q.reshape(bsz, rows, GRID_W, H, dh).transpose(1, 0, 2, 3, 4)
    scale = dh ** -0.5

    def one_row(args):
        q_r, idx_r, bias_r = args
        k_r = k[:, idx_r]
        v_r = v[:, idx_r]
        s = jnp.einsum('bqhd,bqnhd->bhqn', q_r, k_r).astype(jnp.float32) * scale + bias_r[None].astype(jnp.float32)
        p = jax.nn.softmax(s, axis=-1).astype(v.dtype)
        return jnp.einsum('bhqn,bqnhd->bqhd', p, v_r)

    out = lax.map(one_row, (q_rows, idx, bias))
    return out.transpose(1, 0, 2, 3, 4).reshape(bsz, L, H * dh)


def fourier_mix(u, w_f):
    bsz, L, _ = u.shape
    uf = u.astype(jnp.float32).reshape(bsz, L, FN_GROUPS, FN_GROUP_DIM)
    z = jnp.fft.fftn(uf, axes=(1, 3)).real.astype(u.dtype)
    y = jnp.einsum('blgc,gce->blge', z, w_f)
    return y.reshape(bsz, L, FN_WIDTH)


def token_mixer(h, w_in, rpb, w_f, g_na, g_fn, w_o):
    bsz, L, _ = h.shape
    proj = h @ w_in
    q = proj[..., :NA_WIDTH].reshape(bsz, L, NA_HEADS, HEAD_DIM)
    k = proj[..., NA_WIDTH:2 * NA_WIDTH].reshape(bsz, L, NA_HEADS, HEAD_DIM)
    v = proj[..., 2 * NA_WIDTH:3 * NA_WIDTH].reshape(bsz, L, NA_HEADS, HEAD_DIM)
    u = proj[..., 3 * NA_WIDTH:]
    a = neighbourhood_attention(q, k, v, rpb)
    f = fourier_mix(u, w_f)
    o = jnp.concatenate([rms_norm(a, g_na), rms_norm(f, g_fn)], axis=-1)
    return o @ w_o


def moe_ffn(h, w_router, b_router, w_gu, b_gu, w_down, b_down):
    bsz, L, D = h.shape
    T = bsz * L
    xt = h.reshape(T, D)
    logits = (xt @ w_router + b_router).astype(jnp.float32)
    top_v, top_e = lax.top_k(logits, TOP_K)
    gates = jax.nn.softmax(top_v, axis=-1)
    N = T * TOP_K
    e_flat = top_e.reshape(N).astype(jnp.int32)
    g_flat = gates.reshape(N)
    tok_flat = jnp.arange(N, dtype=jnp.int32) // TOP_K
    order = jnp.argsort(e_flat)
    e_sorted = e_flat[order]
    counts = jnp.bincount(e_flat, length=N_EXPERTS).astype(jnp.int32)
    padded = ((counts + MOE_BLOCK - 1) // MOE_BLOCK) * MOE_BLOCK
    start = jnp.cumsum(counts) - counts
    pend = jnp.cumsum(padded)
    pstart = pend - padded
    dest = pstart[e_sorted] + jnp.arange(N, dtype=jnp.int32) - start[e_sorted]
    n_blocks = -(-(N + N_EXPERTS * (MOE_BLOCK - 1)) // MOE_BLOCK)
    P = n_blocks * MOE_BLOCK
    tok_buf = jnp.full((P,), T, jnp.int32).at[dest].set(tok_flat[order])
    gate_buf = jnp.zeros((P,), jnp.float32).at[dest].set(g_flat[order])
    block_e = jnp.minimum(
        jnp.searchsorted(pend, jnp.arange(n_blocks, dtype=jnp.int32) * MOE_BLOCK, side='right'),
        N_EXPERTS - 1).astype(jnp.int32)
    x_pad = jnp.concatenate([xt, jnp.zeros((1, D), xt.dtype)], axis=0)
    xb = x_pad[tok_buf].reshape(n_blocks, MOE_BLOCK, D)

    def expert_block(args):
        xblk, e = args
        gu = xblk @ w_gu[e] + b_gu[e]
        glu = jnp.minimum(gu[:, :D_FF], SWIGLU_LIMIT)
        lin = jnp.clip(gu[:, D_FF:], -SWIGLU_LIMIT, SWIGLU_LIMIT)
        act = glu * jax.nn.sigmoid(SWIGLU_ALPHA * glu) * (lin + 1)
        return act @ w_down[e] + b_down[e]

    yb = lax.map(expert_block, (xb, block_e)).reshape(P, D)
    out = jnp.zeros((T + 1, D), jnp.float32).at[tok_buf].add(yb.astype(jnp.float32) * gate_buf[:, None])
    return out[:T].astype(h.dtype).reshape(bsz, L, D)


def encoder_layer(x, c, w_ada, b_ada, g_pre_mix, g_post_mix, g_pre_ffn, g_post_ffn,
                  w_in, rpb, w_fourier, g_na, g_fn, w_out,
                  w_router, b_router, w_gate_up, b_gate_up, w_down, b_down):
    ada = (jax.nn.silu(c) @ w_ada + b_ada)[:, None, :]
    sh1, sc1, gt1, sh2, sc2, gt2 = jnp.split(ada, 6, axis=-1)
    h = rms_norm(x, g_pre_mix) * (1 + sc1) + sh1
    x = x + gt1 * rms_norm(token_mixer(h, w_in, rpb, w_fourier, g_na, g_fn, w_out), g_post_mix)
    h = rms_norm(x, g_pre_ffn) * (1 + sc2) + sh2
    x = x + gt2 * rms_norm(moe_ffn(h, w_router, b_router, w_gate_up, b_gate_up, w_down, b_down), g_post_ffn)
    return x


def setup_inputs(seed: int = 0) -> dict:
    key = jax.random.key(seed)
    ks = jax.random.split(key, 24)
    f32 = jnp.float32
    nrm = lambda k, shape, s: (jax.random.normal(k, shape, f32) * s)
    D = D_MODEL
    return {
        "x_prompt": nrm(ks[0], (BATCH, SEQ, D), 1.0),
        "x_sample": nrm(ks[1], (DEC_BATCH, DEC_SEQ, D), 1.0),
        "c_prompt": nrm(ks[2], (BATCH, D), 1.0),
        "c_sample": nrm(ks[3], (DEC_BATCH, D), 1.0),
        "w_ada": nrm(ks[4], (DEPTH, D, 6 * D), 0.5 * D ** -0.5),
        "b_ada": nrm(ks[5], (DEPTH, 6 * D), 0.02),
        "g_pre_mix": 1.0 + nrm(ks[6], (DEPTH, D), 0.02),
        "g_post_mix": 1.0 + nrm(ks[7], (DEPTH, D), 0.02),
        "g_pre_ffn": 1.0 + nrm(ks[8], (DEPTH, D), 0.02),
        "g_post_ffn": 1.0 + nrm(ks[9], (DEPTH, D), 0.02),
        "w_in": nrm(ks[10], (DEPTH, D, IN_WIDTH), D ** -0.5),
        "rpb": nrm(ks[11], (DEPTH, NA_HEADS, 2 * WIN_ROWS_MAX - 1, 2 * WIN_COLS - 1), 0.1),
        "w_fourier": nrm(ks[12], (DEPTH, FN_GROUPS, FN_GROUP_DIM, FN_GROUP_DIM), FN_GROUP_DIM ** -0.5),
        "g_na": 1.0 + nrm(ks[13], (DEPTH, NA_WIDTH), 0.02),
        "g_fn": 1.0 + nrm(ks[14], (DEPTH, FN_WIDTH), 0.02),
        "w_out": nrm(ks[15], (DEPTH, MIX_WIDTH, D), MIX_WIDTH ** -0.5),
        "w_router": nrm(ks[16], (DEPTH, D, N_EXPERTS), D ** -0.5),
        "b_router": nrm(ks[17], (DEPTH, N_EXPERTS), 0.01),
        "w_gate_up": nrm(ks[18], (DEPTH, N_EXPERTS, D, 2 * D_FF), D ** -0.5),
        "b_gate_up": nrm(ks[19], (DEPTH, N_EXPERTS, 2 * D_FF), 0.02),
        "w_down": nrm(ks[20], (DEPTH, N_EXPERTS, D_FF, D), D_FF ** -0.5),
        "b_down": nrm(ks[21], (DEPTH, N_EXPERTS, D), 0.02),
    }


def reference(x_prompt, x_sample, c_prompt, c_sample, w_ada, b_ada, g_pre_mix, g_post_mix,
              g_pre_ffn, g_post_ffn, w_in, rpb, w_fourier, g_na, g_fn, w_out,
              w_router, b_router, w_gate_up, b_gate_up, w_down, b_down):
    y_prompt = x_prompt
    y_sample = x_sample
    for l in range(DEPTH):
        layer_params = (w_ada[l], b_ada[l], g_pre_mix[l], g_post_mix[l], g_pre_ffn[l], g_post_ffn[l],
                        w_in[l], rpb[l], w_fourier[l], g_na[l], g_fn[l], w_out[l],
                        w_router[l], b_router[l], w_gate_up[l], b_gate_up[l], w_down[l], b_down[l])
        y_prompt = encoder_layer(y_prompt, c_prompt, *layer_params)
        y_sample = encoder_layer(y_sample, c_sample, *layer_params)
    return (y_prompt, y_sample)
```

```python
import functools

import numpy as np
import jax
import jax.numpy as jnp
from jax import lax
from jax.experimental import pallas as pl
from jax.experimental.pallas import tpu as pltpu

F32 = jnp.float32
BF16 = jnp.bfloat16

D_MODEL = 1024
GRID_W = 64
NA_WIDTH = 512
HEAD_DIM = 64
NA_HEADS = 8
FN_WIDTH = 512
FN_GROUPS = 4
FN_GROUP_DIM = 128
WIN_ROWS = 8
WIN_COLS = 16
N_EXPERTS = 32
TOP_K = 4
D_FF = 1024
SWIGLU_ALPHA = 1.702
SWIGLU_LIMIT = 7.0
MOE_BLOCK = 256
EPS = 1e-6

PAIR_TOKENS = 2 * GRID_W
SLAB_ROWS = 10
SLAB_TOKENS = SLAB_ROWS * GRID_W
N_BIAS_VARIANTS = 5
MASK_VALUE = -1e30
TRASH_ROWS = 512
TOKEN_PAD = TRASH_ROWS // TOP_K

TM = 512
ATT_TOKENS = 1024
CMB_TM = 256
VMEM_LIMIT = 56 * 1024 * 1024


def _cparams(sem):
    return pltpu.CompilerParams(dimension_semantics=sem, vmem_limit_bytes=VMEM_LIMIT)


def _rms(x, g):
    return x * lax.rsqrt(jnp.mean(x * x, axis=-1, keepdims=True) + EPS) * g


def _ada_kernel(c_ref, w_ref, b_ref, o_ref):
    c = c_ref[...]
    s = c * jax.nn.sigmoid(c)
    o_ref[...] = jnp.dot(s.astype(BF16), w_ref[...].astype(BF16), preferred_element_type=F32) + b_ref[...]


def _ada(c_all, w_ada, b_ada):
    nb = c_all.shape[0]
    n_out = w_ada.shape[1]
    tn = 1536
    return pl.pallas_call(
        _ada_kernel,
        grid=(n_out // tn,),
        in_specs=[pl.BlockSpec((nb, D_MODEL), lambda j: (0, 0)),
                  pl.BlockSpec((D_MODEL, tn), lambda j: (0, j)),
                  pl.BlockSpec((1, tn), lambda j: (0, j))],
        out_specs=pl.BlockSpec((nb, tn), lambda j: (0, j)),
        out_shape=jax.ShapeDtypeStruct((nb, n_out), F32),
        compiler_params=_cparams(("arbitrary",)),
    )(c_all, w_ada, b_ada.reshape(1, n_out))


def _inproj_kernel(x_ref, sh_ref, sc_ref, g_ref, wq_ref, wkt_ref, wv_ref, wu_ref, cs_ref,
                   q_ref, kt_ref, v_ref, ucs_ref):
    h = _rms(x_ref[...], g_ref[...]) * (1.0 + sc_ref[0]) + sh_ref[0]
    hb = h.astype(BF16)
    q_ref[...] = jnp.dot(hb, wq_ref[...], preferred_element_type=F32).astype(BF16)
    v_ref[...] = jnp.dot(hb, wv_ref[...], preferred_element_type=F32).astype(BF16)
    kt = lax.dot_general(wkt_ref[...], hb, (((1,), (1,)), ((), ())), preferred_element_type=F32).astype(BF16)
    for c in range(TM // PAIR_TOKENS):
        kt_ref[0, c] = kt[:, c * PAIR_TOKENS:(c + 1) * PAIR_TOKENS]
    u = jnp.dot(hb, wu_ref[...], preferred_element_type=F32).astype(BF16)
    for g in range(FN_GROUPS):
        sl = slice(g * FN_GROUP_DIM, (g + 1) * FN_GROUP_DIM)
        r = jnp.dot(u[:, sl], cs_ref[...], preferred_element_type=F32)
        ucs_ref[0, 0, :, sl] = r[:, :FN_GROUP_DIM].astype(BF16)
        ucs_ref[0, 1, :, sl] = r[:, FN_GROUP_DIM:].astype(BF16)


def _inproj(x2d, ada3, ada_row0, bsz, L, g_pre, wq, wkt, wv, wu, cs):
    T = bsz * L
    tpb = L // TM
    row = lambda j: (lambda i: ((ada_row0 + i // tpb) * 6 + j, 0, 0))
    const2 = lambda i: (0, 0)
    return pl.pallas_call(
        _inproj_kernel,
        grid=(T // TM,),
        in_specs=[pl.BlockSpec((TM, D_MODEL), lambda i: (i, 0)),
                  pl.BlockSpec((1, 1, D_MODEL), row(0)),
                  pl.BlockSpec((1, 1, D_MODEL), row(1)),
                  pl.BlockSpec((1, D_MODEL), const2),
                  pl.BlockSpec((D_MODEL, NA_WIDTH), const2),
                  pl.BlockSpec((NA_WIDTH, D_MODEL), const2),
                  pl.BlockSpec((D_MODEL, NA_WIDTH), const2),
                  pl.BlockSpec((D_MODEL, FN_WIDTH), const2),
                  pl.BlockSpec((FN_GROUP_DIM, 2 * FN_GROUP_DIM), const2)],
        out_specs=[pl.BlockSpec((TM, NA_WIDTH), lambda i: (i, 0)),
                   pl.BlockSpec((1, TM // PAIR_TOKENS, NA_WIDTH, PAIR_TOKENS),
                                lambda i: (i // tpb, i % tpb, 0, 0)),
                   pl.BlockSpec((TM, NA_WIDTH), lambda i: (i, 0)),
                   pl.BlockSpec((1, 2, TM, FN_WIDTH), lambda i: (i // tpb, 0, i % tpb, 0))],
        out_shape=[jax.ShapeDtypeStruct((T, NA_WIDTH), BF16),
                   jax.ShapeDtypeStruct((bsz, L // PAIR_TOKENS, NA_WIDTH, PAIR_TOKENS), BF16),
                   jax.ShapeDtypeStruct((T, NA_WIDTH), BF16),
                   jax.ShapeDtypeStruct((bsz, 2, L, FN_WIDTH), BF16)],
        compiler_params=_cparams(("arbitrary",)),
    )(x2d, ada3, ada3, g_pre, wq, wkt, wv, wu, cs)


def _bias_table(rpb):
    rows = 32
    n_pairs = rows // 2
    js = [0, 1, 2, n_pairs - 2, n_pairs - 1]
    c = np.arange(GRID_W)
    kc0 = np.clip(c - WIN_COLS // 2, 0, GRID_W - WIN_COLS)
    cp = np.arange(GRID_W)
    col_ok = (cp[None, :] >= kc0[:, None]) & (cp[None, :] < kc0[:, None] + WIN_COLS)
    ci = np.clip(cp[None, :] - c[:, None] + WIN_COLS - 1, 0, 2 * WIN_COLS - 2)
    RI = np.zeros((N_BIAS_VARIANTS, PAIR_TOKENS, SLAB_TOKENS), np.int32)
    CI = np.zeros_like(RI)
    OK = np.zeros(RI.shape, bool)
    for vi, j in enumerate(js):
        start = int(np.clip(2 * j - 4, 0, rows - SLAB_ROWS))
        for rl in range(2):
            r = 2 * j + rl
            kr0 = int(np.clip(r - WIN_ROWS // 2, 0, rows - WIN_ROWS))
            for i in range(SLAB_ROWS):
                kr = start + i
                row_ok = kr0 <= kr < kr0 + WIN_ROWS
                ri = int(np.clip(kr - r + WIN_ROWS - 1, 0, 2 * WIN_ROWS - 2))
                qs = slice(rl * GRID_W, (rl + 1) * GRID_W)
                ks = slice(i * GRID_W, (i + 1) * GRID_W)
                RI[vi, qs, ks] = ri
                CI[vi, qs, ks] = ci
                OK[vi, qs, ks] = col_ok & row_ok
    tab = rpb[:, RI, CI]
    tab = jnp.where(OK[None], tab, MASK_VALUE)
    return tab.transpose(1, 0, 2, 3).astype(BF16)


def _attn_kernel(q_ref, kt_ref, v_ref, bias_ref, g_ref, o_ref, *, n_pairs):
    step = pl.program_id(1)
    pairs_per_step = ATT_TOKENS // PAIR_TOKENS
    lane = lax.broadcasted_iota(jnp.int32, (PAIR_TOKENS, 2 * HEAD_DIM), 1)
    first_head = lane < HEAD_DIM

    def pair_body(p, carry):
        j = step * pairs_per_step + p
        start2 = jnp.clip(j - 2, 0, n_pairs - SLAB_ROWS // 2)
        variant = jnp.where(j < 2, j, jnp.where(j >= n_pairs - 2, j - (n_pairs - 2) + 3, 2))
        qp = q_ref[0, pl.ds(pl.multiple_of(p * PAIR_TOKENS, PAIR_TOKENS), PAIR_TOKENS), :]
        kt5 = kt_ref[0, pl.ds(start2, SLAB_ROWS // 2)]
        vs = v_ref[0, pl.ds(pl.multiple_of(start2 * PAIR_TOKENS, PAIR_TOKENS), SLAB_TOKENS), :]
        outs = []
        for hp in range(NA_HEADS // 2):
            hs = slice(hp * 2 * HEAD_DIM, (hp + 1) * 2 * HEAD_DIM)
            q2 = qp[:, hs]
            kt2 = jnp.concatenate([kt5[c, hs, :] for c in range(SLAB_ROWS // 2)], axis=1)
            v2 = vs[:, hs]
            o_pair = None
            for hh in range(2):
                qm = jnp.where(first_head if hh == 0 else ~first_head, q2, jnp.zeros_like(q2))
                s = jnp.dot(qm, kt2, preferred_element_type=F32)
                s = s + bias_ref[variant, hp * 2 + hh].astype(F32)
                m = jnp.max(s, axis=-1, keepdims=True)
                e = jnp.exp(s - m)
                l = jnp.sum(e, axis=-1, keepdims=True)
                o = jnp.dot(e.astype(BF16), v2, preferred_element_type=F32) / l
                o_pair = o if hh == 0 else jnp.where(first_head, o_pair, o)
            outs.append(o_pair)
        a = jnp.concatenate(outs, axis=1)
        o_ref[0, pl.ds(pl.multiple_of(p * PAIR_TOKENS, PAIR_TOKENS), PAIR_TOKENS), :] = _rms(a, g_ref[...]).astype(BF16)
        return carry

    lax.fori_loop(0, pairs_per_step, pair_body, 0)


def _attention(q3, kt4, v3, bias, g_na):
    bsz, L, _ = q3.shape
    n_pairs = L // PAIR_TOKENS
    return pl.pallas_call(
        functools.partial(_attn_kernel, n_pairs=n_pairs),
        grid=(bsz, L // ATT_TOKENS),
        in_specs=[pl.BlockSpec((1, ATT_TOKENS, NA_WIDTH), lambda b, s: (b, s, 0)),
                  pl.BlockSpec((1, n_pairs, NA_WIDTH, PAIR_TOKENS), lambda b, s: (b, 0, 0, 0)),
                  pl.BlockSpec((1, L, NA_WIDTH), lambda b, s: (b, 0, 0)),
                  pl.BlockSpec((N_BIAS_VARIANTS, NA_HEADS, PAIR_TOKENS, SLAB_TOKENS), lambda b, s: (0, 0, 0, 0)),
                  pl.BlockSpec((1, NA_WIDTH), lambda b, s: (0, 0))],
        out_specs=pl.BlockSpec((1, ATT_TOKENS, NA_WIDTH), lambda b, s: (b, s, 0)),
        out_shape=jax.ShapeDtypeStruct((bsz, L, NA_WIDTH), BF16),
        compiler_params=_cparams(("arbitrary", "arbitrary")),
    )(q3, kt4, v3, bias, g_na)


def _dft_tables(L):
    j = jnp.arange(L, dtype=jnp.int32)
    ang = ((j[:, None] * j[None, :]) % L).astype(F32) * (2.0 * np.pi / L)
    return jnp.concatenate([jnp.cos(ang), -jnp.sin(ang)], axis=1).astype(BF16)


def _channel_table():
    j = jnp.arange(FN_GROUP_DIM, dtype=jnp.int32)
    ang = ((j[:, None] * j[None, :]) % FN_GROUP_DIM).astype(F32) * (2.0 * np.pi / FN_GROUP_DIM)
    return jnp.concatenate([jnp.cos(ang), jnp.sin(ang)], axis=1).astype(BF16)


def _fourier_kernel(tab_ref, ucs_ref, wf_ref, g_ref, o_ref):
    z = jnp.dot(tab_ref[...], ucs_ref[0], preferred_element_type=F32).astype(BF16)
    ys = [jnp.dot(z[:, g * FN_GROUP_DIM:(g + 1) * FN_GROUP_DIM], wf_ref[g], preferred_element_type=F32)
          for g in range(FN_GROUPS)]
    o_ref[0] = _rms(jnp.concatenate(ys, axis=1), g_ref[...]).astype(BF16)


def _fourier(tab, ucs3, wf, g_fn):
    bsz, L2, _ = ucs3.shape
    L = L2 // 2
    return pl.pallas_call(
        _fourier_kernel,
        grid=(bsz, L // TM),
        in_specs=[pl.BlockSpec((TM, L2), lambda b, t: (t, 0)),
                  pl.BlockSpec((1, L2, FN_WIDTH), lambda b, t: (b, 0, 0)),
                  pl.BlockSpec((FN_GROUPS, FN_GROUP_DIM, FN_GROUP_DIM), lambda b, t: (0, 0, 0)),
                  pl.BlockSpec((1, FN_WIDTH), lambda b, t: (0, 0))],
        out_specs=pl.BlockSpec((1, TM, FN_WIDTH), lambda b, t: (b, t, 0)),
        out_shape=jax.ShapeDtypeStruct((bsz, L, FN_WIDTH), BF16),
        compiler_params=_cparams(("arbitrary", "arbitrary")),
    )(tab, ucs3, wf, g_fn)


def _outproj_kernel(na_ref, fn_ref, x_ref, gt1_ref, sh2_ref, sc2_ref, gpost_ref, gpre_ref,
                    woa_ref, wof_ref, wr_ref, br_ref, cnt0_ref,
                    x1_ref, h2_ref, e_ref, gate_ref, rank_ref, cnt_ref, run_ref):
    @pl.when(pl.program_id(0) == 0)
    def _():
        run_ref[...] = cnt0_ref[...]

    mix = (jnp.dot(na_ref[...], woa_ref[...], preferred_element_type=F32)
           + jnp.dot(fn_ref[...], wof_ref[...], preferred_element_type=F32))
    x1 = x_ref[...] + gt1_ref[0] * _rms(mix, gpost_ref[...])
    x1_ref[...] = x1
    h2 = _rms(x1, gpre_ref[...]) * (1.0 + sc2_ref[0]) + sh2_ref[0]
    h2_ref[...] = h2

    hi = h2.astype(BF16)
    lo = (h2 - hi.astype(F32)).astype(BF16)
    nt = (((1,), (1,)), ((), ()))
    a = lax.dot_general(wr_ref[...], hi, nt, preferred_element_type=F32)
    b = lax.dot_general(wr_ref[0:N_EXPERTS, :], lo, nt, preferred_element_type=F32)
    logits = a[:N_EXPERTS] + a[N_EXPERTS:] + b + br_ref[:, 0:1]

    eidx = lax.broadcasted_iota(jnp.int32, (N_EXPERTS, TM), 0)
    vals, onehots = [], []
    cur = logits
    for k in range(TOP_K):
        m = jnp.max(cur, axis=0, keepdims=True)
        idx = jnp.min(jnp.where(cur == m, eidx, N_EXPERTS), axis=0, keepdims=True)
        hit = eidx == idx
        vals.append(m)
        onehots.append(hit)
        e_ref[k:k + 1, :] = idx
        cur = jnp.where(hit, -jnp.inf, cur)
    exps = [jnp.exp(v - vals[0]) for v in vals]
    den = exps[0] + exps[1] + exps[2] + exps[3]
    for k in range(TOP_K):
        gate_ref[k:k + 1, :] = exps[k] / den

    oh = [h.astype(F32) for h in onehots]
    oh_all = oh[0] + oh[1] + oh[2] + oh[3]
    s_i = lax.broadcasted_iota(jnp.int32, (TM, TM), 0)
    t_i = lax.broadcasted_iota(jnp.int32, (TM, TM), 1)
    before = jnp.where(s_i < t_i, 1.0, 0.0).astype(BF16)
    prefix = jnp.dot(oh_all.astype(BF16), before, preferred_element_type=F32)
    base = run_ref[:, 0:1] + prefix
    for k in range(TOP_K):
        rank_ref[k:k + 1, :] = jnp.sum(oh[k] * base, axis=0, keepdims=True).astype(jnp.int32)
    run_ref[...] = run_ref[...] + jnp.sum(oh_all, axis=1, keepdims=True)
    cnt_ref[...] = run_ref[...]


def _outproj(na2d, fn2d, x2d, ada3, ada_row0, bsz, L, g_post, g_pre, woa, wof, wr2, br, cnt0):
    T = bsz * L
    tpb = L // TM
    row = lambda j: (lambda i: ((ada_row0 + i // tpb) * 6 + j, 0, 0))
    const2 = lambda i: (0, 0)
    tok = lambda i: (i, 0)
    lanes = lambda i: (0, i)
    return pl.pallas_call(
        _outproj_kernel,
        grid=(T // TM,),
        in_specs=[pl.BlockSpec((TM, NA_WIDTH), tok),
                  pl.BlockSpec((TM, FN_WIDTH), tok),
                  pl.BlockSpec((TM, D_MODEL), tok),
                  pl.BlockSpec((1, 1, D_MODEL), row(2)),
                  pl.BlockSpec((1, 1, D_MODEL), row(3)),
                  pl.BlockSpec((1, 1, D_MODEL), row(4)),
                  pl.BlockSpec((1, D_MODEL), const2),
                  pl.BlockSpec((1, D_MODEL), const2),
                  pl.BlockSpec((NA_WIDTH, D_MODEL), const2),
                  pl.BlockSpec((FN_WIDTH, D_MODEL), const2),
                  pl.BlockSpec((2 * N_EXPERTS, D_MODEL), const2),
                  pl.BlockSpec((N_EXPERTS, 128), const2),
                  pl.BlockSpec((N_EXPERTS, 128), const2)],
        out_specs=[pl.BlockSpec((TM, D_MODEL), tok),
                   pl.BlockSpec((TM, D_MODEL), tok),
                   pl.BlockSpec((TOP_K, TM), lanes),
                   pl.BlockSpec((TOP_K, TM), lanes),
                   pl.BlockSpec((TOP_K, TM), lanes),
                   pl.BlockSpec((N_EXPERTS, 128), const2)],
        out_shape=[jax.ShapeDtypeStruct((T, D_MODEL), F32),
                   jax.ShapeDtypeStruct((T, D_MODEL), F32),
                   jax.ShapeDtypeStruct((TOP_K, T), jnp.int32),
                   jax.ShapeDtypeStruct((TOP_K, T), F32),
                   jax.ShapeDtypeStruct((TOP_K, T), jnp.int32),
                   jax.ShapeDtypeStruct((N_EXPERTS, 128), F32)],
        scratch_shapes=[pltpu.VMEM((N_EXPERTS, 128), F32)],
        compiler_params=_cparams(("arbitrary",)),
    )(na2d, fn2d, x2d, ada3, ada3, ada3, g_post, g_pre, woa, wof, wr2, br, cnt0)


def _row_gather(h2_hbm, xbuf, sem, slots_ref, buf, r):
    tok = slots_ref[0, 0, r] // TOP_K
    return pltpu.make_async_copy(h2_hbm.at[pl.ds(tok, 1), :], xbuf.at[buf, pl.ds(r, 1), :], sem.at[buf])


def _row_scatter(ybuf, y_hbm, sem, slots_ref, buf, r):
    return pltpu.make_async_copy(ybuf.at[buf, pl.ds(r, 1), :], y_hbm.at[pl.ds(slots_ref[0, 0, r], 1), :],
                                 sem.at[buf])


def _gather_wait(h2_hbm, xbuf, sem, buf):
    pltpu.make_async_copy(h2_hbm.at[pl.ds(0, MOE_BLOCK), :], xbuf.at[buf], sem.at[buf]).wait()


def _scatter_wait(ybuf, y_hbm, sem, buf):
    pltpu.make_async_copy(ybuf.at[buf], y_hbm.at[pl.ds(0, MOE_BLOCK), :], sem.at[buf]).wait()


def _expert_kernel(be_ref, cur_ref, nxt_ref, h2_hbm, wgu_ref, bgu_ref, wd_ref, bd_ref, y_hbm,
                   xbuf, ybuf, gsem, ssem, *, n_blocks):
    i = pl.program_id(0)
    buf = i % 2
    other = 1 - buf

    @pl.when(i == 0)
    def _():
        def issue(r, c):
            _row_gather(h2_hbm, xbuf, gsem, cur_ref, 0, r).start()
            return c
        lax.fori_loop(0, MOE_BLOCK, issue, 0)

    for r in range(MOE_BLOCK):
        _row_gather(h2_hbm, xbuf, gsem, nxt_ref, other, r).start()

    _gather_wait(h2_hbm, xbuf, gsem, buf)
    x = xbuf[buf].astype(BF16)
    gu = jnp.dot(x, wgu_ref[0], preferred_element_type=F32) + bgu_ref[0]
    glu = jnp.minimum(gu[:, :D_FF], SWIGLU_LIMIT)
    lin = jnp.clip(gu[:, D_FF:], -SWIGLU_LIMIT, SWIGLU_LIMIT)
    act = glu * jax.nn.sigmoid(SWIGLU_ALPHA * glu) * (lin + 1.0)
    y = jnp.dot(act.astype(BF16), wd_ref[0], preferred_element_type=F32) + bd_ref[0]

    @pl.when(i >= 2)
    def _():
        _scatter_wait(ybuf, y_hbm, ssem, buf)

    ybuf[buf] = y
    for r in range(MOE_BLOCK):
        _row_scatter(ybuf, y_hbm, ssem, cur_ref, buf, r).start()

    @pl.when(i == n_blocks - 1)
    def _():
        _gather_wait(h2_hbm, xbuf, gsem, other)
        if n_blocks >= 2:
            _scatter_wait(ybuf, y_hbm, ssem, other)
        _scatter_wait(ybuf, y_hbm, ssem, buf)


def _experts(block_e, slots3, h2_all, wgu, bgu, wd, bd, n_slots):
    n_blocks = slots3.shape[0]
    last = n_blocks - 1
    return pl.pallas_call(
        functools.partial(_expert_kernel, n_blocks=n_blocks),
        grid_spec=pltpu.PrefetchScalarGridSpec(
            num_scalar_prefetch=1,
            grid=(n_blocks,),
            in_specs=[pl.BlockSpec((1, 1, MOE_BLOCK), lambda i, be: (i, 0, 0), memory_space=pltpu.SMEM),
                      pl.BlockSpec((1, 1, MOE_BLOCK), lambda i, be: (jnp.minimum(i + 1, last), 0, 0),
                                   memory_space=pltpu.SMEM),
                      pl.BlockSpec(memory_space=pl.ANY),
                      pl.BlockSpec((1, D_MODEL, 2 * D_FF), lambda i, be: (be[i], 0, 0)),
                      pl.BlockSpec((1, 1, 2 * D_FF), lambda i, be: (be[i], 0, 0)),
                      pl.BlockSpec((1, D_FF, D_MODEL), lambda i, be: (be[i], 0, 0)),
                      pl.BlockSpec((1, 1, D_MODEL), lambda i, be: (be[i], 0, 0))],
            out_specs=pl.BlockSpec(memory_space=pl.ANY),
            scratch_shapes=[pltpu.VMEM((2, MOE_BLOCK, D_MODEL), F32),
                            pltpu.VMEM((2, MOE_BLOCK, D_MODEL), F32),
                            pltpu.SemaphoreType.DMA((2,)),
                            pltpu.SemaphoreType.DMA((2,))]),
        out_shape=jax.ShapeDtypeStruct((n_slots, D_MODEL), F32),
        compiler_params=_cparams(("arbitrary",)),
    )(block_e, slots3, slots3, h2_all, wgu, bgu, wd, bd)


def _combine_kernel(y_ref, gate_ref, x1_ref, gt2_ref, g_ref, o_ref):
    moe = gate_ref[:, 0:1] * y_ref[:, 0:D_MODEL]
    for k in range(1, TOP_K):
        moe = moe + gate_ref[:, k:k + 1] * y_ref[:, k * D_MODEL:(k + 1) * D_MODEL]
    o_ref[...] = x1_ref[...] + gt2_ref[0] * _rms(moe, g_ref[...])


def _combine(y4, tile0, gates_tk, x1, ada3, ada_row0, bsz, L, g_post):
    T = bsz * L
    tpb = L // CMB_TM
    return pl.pallas_call(
        _combine_kernel,
        grid=(T // CMB_TM,),
        in_specs=[pl.BlockSpec((CMB_TM, TOP_K * D_MODEL), lambda i: (tile0 + i, 0)),
                  pl.BlockSpec((CMB_TM, TOP_K), lambda i: (i, 0)),
                  pl.BlockSpec((CMB_TM, D_MODEL), lambda i: (i, 0)),
                  pl.BlockSpec((1, 1, D_MODEL), lambda i: ((ada_row0 + i // tpb) * 6 + 5, 0, 0)),
                  pl.BlockSpec((1, D_MODEL), lambda i: (0, 0))],
        out_specs=pl.BlockSpec((CMB_TM, D_MODEL), lambda i: (i, 0)),
        out_shape=jax.ShapeDtypeStruct((T, D_MODEL), F32),
        compiler_params=_cparams(("arbitrary",)),
    )(y4, gates_tk, x1, ada3, g_post)


def kernel(x_prompt, x_sample, c_prompt, c_sample, w_ada, b_ada, g_pre_mix, g_post_mix, g_pre_ffn, g_post_ffn,
           w_in, rpb, w_fourier, g_na, g_fn, w_out, w_router, b_router, w_gate_up, b_gate_up, w_down, b_down):
    assert w_ada.shape[0] == 1, "single layer"
    groups = [(x_prompt, c_prompt), (x_sample, c_sample)]
    for x, _ in groups:
        assert x.shape[1] % ATT_TOKENS == 0 and x.shape[1] // GRID_W >= 12 and x.shape[2] == D_MODEL

    row2 = lambda a: a[0].reshape(1, -1)
    w_in0 = w_in[0]
    wq = (w_in0[:, :NA_WIDTH] * HEAD_DIM ** -0.5).astype(BF16)
    wkt = w_in0[:, NA_WIDTH:2 * NA_WIDTH].T.astype(BF16)
    wv = w_in0[:, 2 * NA_WIDTH:3 * NA_WIDTH].astype(BF16)
    wu = w_in0[:, 3 * NA_WIDTH:].astype(BF16)
    cs = _channel_table()
    bias = _bias_table(rpb[0])
    wf = w_fourier[0].astype(BF16)
    woa = w_out[0, :NA_WIDTH].astype(BF16)
    wof = w_out[0, NA_WIDTH:].astype(BF16)
    wr_t = w_router[0].T
    wr_hi = wr_t.astype(BF16)
    wr_lo = (wr_t - wr_hi.astype(F32)).astype(BF16)
    wr2 = jnp.concatenate([wr_hi, wr_lo], axis=0)
    br = jnp.broadcast_to(b_router[0][:, None], (N_EXPERTS, 128))
    wgu = w_gate_up[0].astype(BF16)
    wd = w_down[0].astype(BF16)
    bgu = b_gate_up[0].reshape(N_EXPERTS, 1, 2 * D_FF)
    bd = b_down[0].reshape(N_EXPERTS, 1, D_MODEL)

    c_all = jnp.concatenate([c_prompt, c_sample], axis=0)
    ada3 = _ada(c_all, w_ada[0], b_ada[0]).reshape(c_all.shape[0] * 6, 1, D_MODEL)

    counts = jnp.zeros((N_EXPERTS, 128), F32)
    per_group = []
    ada_row0 = 0
    for x, _ in groups:
        bsz, L, _ = x.shape
        x2d = x.reshape(bsz * L, D_MODEL)
        q, kt4, v, ucs = _inproj(x2d, ada3, ada_row0, bsz, L, row2(g_pre_mix), wq, wkt, wv, wu, cs)
        na = _attention(q.reshape(bsz, L, NA_WIDTH), kt4, v.reshape(bsz, L, NA_WIDTH), bias, row2(g_na))
        fn = _fourier(_dft_tables(L), ucs.reshape(bsz, 2 * L, FN_WIDTH), wf, row2(g_fn))
        x1, h2, e_t, gate_t, rank_t, counts = _outproj(
            na.reshape(bsz * L, NA_WIDTH), fn.reshape(bsz * L, FN_WIDTH), x2d, ada3, ada_row0, bsz, L,
            row2(g_post_mix), row2(g_pre_ffn), woa, wof, wr2, br, counts)
        per_group.append((x1, h2, e_t, gate_t, rank_t, bsz, L, ada_row0))
        ada_row0 += bsz

    T = sum(g[5] * g[6] for g in per_group)
    N = T * TOP_K
    n_blocks = -(-(N + N_EXPERTS * (MOE_BLOCK - 1)) // MOE_BLOCK)
    P = n_blocks * MOE_BLOCK
    cnt = counts[:, 0].astype(jnp.int32)
    padded = ((cnt + MOE_BLOCK - 1) // MOE_BLOCK) * MOE_BLOCK
    pend = jnp.cumsum(padded)
    pstart = pend - padded
    e_all = jnp.concatenate([g[2] for g in per_group], axis=1)
    rank_all = jnp.concatenate([g[4] for g in per_group], axis=1)
    dest = pstart[e_all] + rank_all
    slot_id = jnp.arange(T, dtype=jnp.int32)[None, :] * TOP_K + jnp.arange(TOP_K, dtype=jnp.int32)[:, None]
    trash = N + jnp.arange(P, dtype=jnp.int32) % TRASH_ROWS
    slots = trash.at[dest.reshape(-1)].set(slot_id.reshape(-1), unique_indices=True)
    block_e = jnp.minimum(
        jnp.searchsorted(pend, jnp.arange(n_blocks, dtype=jnp.int32) * MOE_BLOCK, side='right'),
        N_EXPERTS - 1).astype(jnp.int32)
    h2_all = jnp.concatenate([g[1] for g in per_group] + [jnp.zeros((TOKEN_PAD, D_MODEL), F32)], axis=0)

    y_slots = _experts(block_e, slots.reshape(n_blocks, 1, MOE_BLOCK), h2_all, wgu, bgu, wd, bd, N + TRASH_ROWS)
    y4 = y_slots.reshape((N + TRASH_ROWS) // TOP_K, TOP_K * D_MODEL)

    outs = []
    tok0 = 0
    for x1, _, _, gate_t, _, bsz, L, row0 in per_group:
        out = _combine(y4, tok0 // CMB_TM, gate_t.T, x1, ada3, row0, bsz, L, row2(g_post_ffn))
        outs.append(out.reshape(bsz, L, D_MODEL))
        tok0 += bsz * L
    return tuple(outs)
```

```python
import functools

import numpy as np
import jax
import jax.numpy as jnp
from jax import lax
from jax.experimental import pallas as pl
from jax.experimental.pallas import tpu as pltpu

F32 = jnp.float32
BF16 = jnp.bfloat16

D_MODEL = 1024
GRID_W = 64
NA_WIDTH = 512
HEAD_DIM = 64
NA_HEADS = 8
FN_WIDTH = 512
FN_GROUPS = 4
FN_GROUP_DIM = 128
WIN_ROWS = 8
WIN_COLS = 16
N_EXPERTS = 32
TOP_K = 4
D_FF = 1024
SWIGLU_ALPHA = 1.702
SWIGLU_LIMIT = 7.0
MOE_BLOCK = 256
EPS = 1e-6

LANES = 128
ROW_TILE = (D_MODEL // LANES, LANES)
PAIR_TOKENS = 2 * GRID_W
SLAB_ROWS = 10
SLAB_TOKENS = SLAB_ROWS * GRID_W
N_BIAS_VARIANTS = 5
MASK_VALUE = -1e30

TM = 512
ATT_TOKENS = 1024
CMB_TM = 256
DISPATCH_UNROLL = 8
ZERO_ROWS = 128
VMEM_LIMIT = 56 * 1024 * 1024


def _cparams(sem):
    return pltpu.CompilerParams(dimension_semantics=sem, vmem_limit_bytes=VMEM_LIMIT)


def _rms(x, g):
    return x * lax.rsqrt(jnp.mean(x * x, axis=-1, keepdims=True) + EPS) * g


def _to_row_tiles(ref, val):
    for s in range(ROW_TILE[0]):
        ref[:, s, :] = val[:, s * LANES:(s + 1) * LANES]


def _from_row_tiles(ref):
    return jnp.concatenate([ref[:, s, :] for s in range(ROW_TILE[0])], axis=1)


def _ada_kernel(c_ref, w_ref, b_ref, o_ref):
    c = c_ref[...]
    s = c * jax.nn.sigmoid(c)
    o_ref[...] = jnp.dot(s.astype(BF16), w_ref[...].astype(BF16), preferred_element_type=F32) + b_ref[...]


def _ada(c_all, w_ada, b_ada):
    nb = c_all.shape[0]
    n_out = w_ada.shape[1]
    tn = 1536
    return pl.pallas_call(
        _ada_kernel,
        grid=(n_out // tn,),
        in_specs=[pl.BlockSpec((nb, D_MODEL), lambda j: (0, 0)),
                  pl.BlockSpec((D_MODEL, tn), lambda j: (0, j)),
                  pl.BlockSpec((1, tn), lambda j: (0, j))],
        out_specs=pl.BlockSpec((nb, tn), lambda j: (0, j)),
        out_shape=jax.ShapeDtypeStruct((nb, n_out), F32),
        compiler_params=_cparams(("arbitrary",)),
    )(c_all, w_ada, b_ada.reshape(1, n_out))


def _inproj_kernel(x_ref, sh_ref, sc_ref, g_ref, wq_ref, wkt_ref, wv_ref, wu_ref, cs_ref,
                   q_ref, kt_ref, v_ref, ucs_ref):
    h = _rms(x_ref[...], g_ref[...]) * (1.0 + sc_ref[0]) + sh_ref[0]
    hb = h.astype(BF16)
    q_ref[...] = jnp.dot(hb, wq_ref[...], preferred_element_type=F32).astype(BF16)
    v_ref[...] = jnp.dot(hb, wv_ref[...], preferred_element_type=F32).astype(BF16)
    kt = lax.dot_general(wkt_ref[...], hb, (((1,), (1,)), ((), ())), preferred_element_type=F32).astype(BF16)
    for c in range(TM // PAIR_TOKENS):
        kt_ref[0, c] = kt[:, c * PAIR_TOKENS:(c + 1) * PAIR_TOKENS]
    u = jnp.dot(hb, wu_ref[...], preferred_element_type=F32).astype(BF16)
    for g in range(FN_GROUPS):
        sl = slice(g * FN_GROUP_DIM, (g + 1) * FN_GROUP_DIM)
        r = jnp.dot(u[:, sl], cs_ref[...], preferred_element_type=F32)
        ucs_ref[0, 0, :, sl] = r[:, :FN_GROUP_DIM].astype(BF16)
        ucs_ref[0, 1, :, sl] = r[:, FN_GROUP_DIM:].astype(BF16)


def _inproj(x2d, ada3, ada_row0, bsz, L, g_pre, wq, wkt, wv, wu, cs):
    T = bsz * L
    tpb = L // TM
    row = lambda j: (lambda i: ((ada_row0 + i // tpb) * 6 + j, 0, 0))
    const2 = lambda i: (0, 0)
    return pl.pallas_call(
        _inproj_kernel,
        grid=(T // TM,),
        in_specs=[pl.BlockSpec((TM, D_MODEL), lambda i: (i, 0)),
                  pl.BlockSpec((1, 1, D_MODEL), row(0)),
                  pl.BlockSpec((1, 1, D_MODEL), row(1)),
                  pl.BlockSpec((1, D_MODEL), const2),
                  pl.BlockSpec((D_MODEL, NA_WIDTH), const2),
                  pl.BlockSpec((NA_WIDTH, D_MODEL), const2),
                  pl.BlockSpec((D_MODEL, NA_WIDTH), const2),
                  pl.BlockSpec((D_MODEL, FN_WIDTH), const2),
                  pl.BlockSpec((FN_GROUP_DIM, 2 * FN_GROUP_DIM), const2)],
        out_specs=[pl.BlockSpec((TM, NA_WIDTH), lambda i: (i, 0)),
                   pl.BlockSpec((1, TM // PAIR_TOKENS, NA_WIDTH, PAIR_TOKENS),
                                lambda i: (i // tpb, i % tpb, 0, 0)),
                   pl.BlockSpec((TM, NA_WIDTH), lambda i: (i, 0)),
                   pl.BlockSpec((1, 2, TM, FN_WIDTH), lambda i: (i // tpb, 0, i % tpb, 0))],
        out_shape=[jax.ShapeDtypeStruct((T, NA_WIDTH), BF16),
                   jax.ShapeDtypeStruct((bsz, L // PAIR_TOKENS, NA_WIDTH, PAIR_TOKENS), BF16),
                   jax.ShapeDtypeStruct((T, NA_WIDTH), BF16),
                   jax.ShapeDtypeStruct((bsz, 2, L, FN_WIDTH), BF16)],
        compiler_params=_cparams(("arbitrary",)),
    )(x2d, ada3, ada3, g_pre, wq, wkt, wv, wu, cs)


def _bias_table(rpb):
    rows = 32
    n_pairs = rows // 2
    c = np.arange(GRID_W)
    kc0 = np.clip(c - WIN_COLS // 2, 0, GRID_W - WIN_COLS)
    col_ok = (c[None, :] >= kc0[:, None]) & (c[None, :] < kc0[:, None] + WIN_COLS)
    ci = c[None, :] - c[:, None] + WIN_COLS - 1
    pick = ((ci[None] == np.arange(2 * WIN_COLS - 1)[:, None, None]) & col_ok[None]).astype(np.float32)
    toe = jnp.einsum('hrk,kcd->hrcd', rpb, jnp.asarray(pick), precision=lax.Precision.HIGHEST)
    toe = jnp.where(col_ok, toe, MASK_VALUE)
    masked = jnp.full((NA_HEADS, GRID_W, GRID_W), MASK_VALUE, F32)
    variants = []
    for j in (0, 1, 2, n_pairs - 2, n_pairs - 1):
        start = int(np.clip(2 * j - 4, 0, rows - SLAB_ROWS))
        halves = []
        for rl in range(2):
            r = 2 * j + rl
            kr0 = int(np.clip(r - WIN_ROWS // 2, 0, rows - WIN_ROWS))
            blocks = []
            for i in range(SLAB_ROWS):
                kr = start + i
                blocks.append(toe[:, kr - r + WIN_ROWS - 1] if kr0 <= kr < kr0 + WIN_ROWS else masked)
            halves.append(jnp.concatenate(blocks, axis=2))
        variants.append(jnp.concatenate(halves, axis=1))
    return jnp.stack(variants).astype(BF16)


def _attn_kernel(q_ref, kt_ref, v_ref, bias_ref, g_ref, o_ref, *, n_pairs):
    step = pl.program_id(1)
    pairs_per_step = ATT_TOKENS // PAIR_TOKENS
    lane = lax.broadcasted_iota(jnp.int32, (PAIR_TOKENS, 2 * HEAD_DIM), 1)
    first_head = lane < HEAD_DIM

    def pair_body(p, carry):
        j = step * pairs_per_step + p
        start2 = jnp.clip(j - 2, 0, n_pairs - SLAB_ROWS // 2)
        variant = jnp.where(j < 2, j, jnp.where(j >= n_pairs - 2, j - (n_pairs - 2) + 3, 2))
        qp = q_ref[0, pl.ds(pl.multiple_of(p * PAIR_TOKENS, PAIR_TOKENS), PAIR_TOKENS), :]
        kt5 = kt_ref[0, pl.ds(start2, SLAB_ROWS // 2)]
        vs = v_ref[0, pl.ds(pl.multiple_of(start2 * PAIR_TOKENS, PAIR_TOKENS), SLAB_TOKENS), :]
        outs = []
        for hp in range(NA_HEADS // 2):
            hs = slice(hp * 2 * HEAD_DIM, (hp + 1) * 2 * HEAD_DIM)
            q2 = qp[:, hs]
            kt2 = jnp.concatenate([kt5[c, hs, :] for c in range(SLAB_ROWS // 2)], axis=1)
            v2 = vs[:, hs]
            o_pair = None
            for hh in range(2):
                qm = jnp.where(first_head if hh == 0 else ~first_head, q2, jnp.zeros_like(q2))
                s = jnp.dot(qm, kt2, preferred_element_type=F32)
                s = s + bias_ref[variant, hp * 2 + hh].astype(F32)
                m = jnp.max(s, axis=-1, keepdims=True)
                e = jnp.exp(s - m)
                l = jnp.sum(e, axis=-1, keepdims=True)
                o = jnp.dot(e.astype(BF16), v2, preferred_element_type=F32) / l
                o_pair = o if hh == 0 else jnp.where(first_head, o_pair, o)
            outs.append(o_pair)
        a = jnp.concatenate(outs, axis=1)
        o_ref[0, pl.ds(pl.multiple_of(p * PAIR_TOKENS, PAIR_TOKENS), PAIR_TOKENS), :] = _rms(a, g_ref[...]).astype(BF16)
        return carry

    lax.fori_loop(0, pairs_per_step, pair_body, 0)


def _attention(q3, kt4, v3, bias, g_na):
    bsz, L, _ = q3.shape
    n_pairs = L // PAIR_TOKENS
    return pl.pallas_call(
        functools.partial(_attn_kernel, n_pairs=n_pairs),
        grid=(bsz, L // ATT_TOKENS),
        in_specs=[pl.BlockSpec((1, ATT_TOKENS, NA_WIDTH), lambda b, s: (b, s, 0)),
                  pl.BlockSpec((1, n_pairs, NA_WIDTH, PAIR_TOKENS), lambda b, s: (b, 0, 0, 0)),
                  pl.BlockSpec((1, L, NA_WIDTH), lambda b, s: (b, 0, 0)),
                  pl.BlockSpec((N_BIAS_VARIANTS, NA_HEADS, PAIR_TOKENS, SLAB_TOKENS), lambda b, s: (0, 0, 0, 0)),
                  pl.BlockSpec((1, NA_WIDTH), lambda b, s: (0, 0))],
        out_specs=pl.BlockSpec((1, ATT_TOKENS, NA_WIDTH), lambda b, s: (b, s, 0)),
        out_shape=jax.ShapeDtypeStruct((bsz, L, NA_WIDTH), BF16),
        compiler_params=_cparams(("arbitrary", "arbitrary")),
    )(q3, kt4, v3, bias, g_na)


def _dft_tables(L):
    j = jnp.arange(L, dtype=jnp.int32)
    ang = ((j[:, None] * j[None, :]) % L).astype(F32) * (2.0 * np.pi / L)
    return jnp.concatenate([jnp.cos(ang), -jnp.sin(ang)], axis=1).astype(BF16)


def _channel_table():
    j = jnp.arange(FN_GROUP_DIM, dtype=jnp.int32)
    ang = ((j[:, None] * j[None, :]) % FN_GROUP_DIM).astype(F32) * (2.0 * np.pi / FN_GROUP_DIM)
    return jnp.concatenate([jnp.cos(ang), jnp.sin(ang)], axis=1).astype(BF16)


def _fourier_kernel(tab_ref, ucs_ref, wf_ref, g_ref, o_ref):
    z = jnp.dot(tab_ref[...], ucs_ref[0], preferred_element_type=F32).astype(BF16)
    ys = [jnp.dot(z[:, g * FN_GROUP_DIM:(g + 1) * FN_GROUP_DIM], wf_ref[g], preferred_element_type=F32)
          for g in range(FN_GROUPS)]
    o_ref[0] = _rms(jnp.concatenate(ys, axis=1), g_ref[...]).astype(BF16)


def _fourier(tab, ucs3, wf, g_fn):
    bsz, L2, _ = ucs3.shape
    L = L2 // 2
    return pl.pallas_call(
        _fourier_kernel,
        grid=(bsz, L // TM),
        in_specs=[pl.BlockSpec((TM, L2), lambda b, t: (t, 0)),
                  pl.BlockSpec((1, L2, FN_WIDTH), lambda b, t: (b, 0, 0)),
                  pl.BlockSpec((FN_GROUPS, FN_GROUP_DIM, FN_GROUP_DIM), lambda b, t: (0, 0, 0)),
                  pl.BlockSpec((1, FN_WIDTH), lambda b, t: (0, 0))],
        out_specs=pl.BlockSpec((1, TM, FN_WIDTH), lambda b, t: (b, t, 0)),
        out_shape=jax.ShapeDtypeStruct((bsz, L, FN_WIDTH), BF16),
        compiler_params=_cparams(("arbitrary", "arbitrary")),
    )(tab, ucs3, wf, g_fn)


def _outproj_kernel(na_ref, fn_ref, x_ref, gt1_ref, sh2_ref, sc2_ref, gpost_ref, gpre_ref,
                    woa_ref, wof_ref, wr_ref, br_ref, cnt0_ref,
                    x1_ref, h2_ref, e_ref, gate_ref, rank_ref, cnt_ref, run_ref):
    @pl.when(pl.program_id(0) == 0)
    def _():
        run_ref[...] = cnt0_ref[...]

    mix = (jnp.dot(na_ref[...], woa_ref[...], preferred_element_type=F32)
           + jnp.dot(fn_ref[...], wof_ref[...], preferred_element_type=F32))
    x1 = x_ref[...] + gt1_ref[0] * _rms(mix, gpost_ref[...])
    x1_ref[...] = x1
    h2 = _rms(x1, gpre_ref[...]) * (1.0 + sc2_ref[0]) + sh2_ref[0]
    _to_row_tiles(h2_ref, h2)

    hi = h2.astype(BF16)
    lo = (h2 - hi.astype(F32)).astype(BF16)
    nt = (((1,), (1,)), ((), ()))
    a = lax.dot_general(wr_ref[...], hi, nt, preferred_element_type=F32)
    b = lax.dot_general(wr_ref[0:N_EXPERTS, :], lo, nt, preferred_element_type=F32)
    logits = a[:N_EXPERTS] + a[N_EXPERTS:] + b + br_ref[:, 0:1]

    eidx = lax.broadcasted_iota(jnp.int32, (N_EXPERTS, TM), 0)
    vals, onehots = [], []
    cur = logits
    for k in range(TOP_K):
        m = jnp.max(cur, axis=0, keepdims=True)
        idx = jnp.min(jnp.where(cur == m, eidx, N_EXPERTS), axis=0, keepdims=True)
        hit = eidx == idx
        vals.append(m)
        onehots.append(hit)
        e_ref[k:k + 1, :] = idx
        cur = jnp.where(hit, -jnp.inf, cur)
    exps = [jnp.exp(v - vals[0]) for v in vals]
    den = exps[0] + exps[1] + exps[2] + exps[3]
    for k in range(TOP_K):
        gate_ref[k:k + 1, :] = exps[k] / den

    oh = [h.astype(F32) for h in onehots]
    oh_all = oh[0] + oh[1] + oh[2] + oh[3]
    s_i = lax.broadcasted_iota(jnp.int32, (TM, TM), 0)
    t_i = lax.broadcasted_iota(jnp.int32, (TM, TM), 1)
    before = jnp.where(s_i < t_i, 1.0, 0.0).astype(BF16)
    prefix = jnp.dot(oh_all.astype(BF16), before, preferred_element_type=F32)
    base = run_ref[:, 0:1] + prefix
    for k in range(TOP_K):
        rank_ref[k:k + 1, :] = jnp.sum(oh[k] * base, axis=0, keepdims=True).astype(jnp.int32)
    run_ref[...] = run_ref[...] + jnp.sum(oh_all, axis=1, keepdims=True)
    cnt_ref[...] = run_ref[...]


def _outproj(na2d, fn2d, x2d, ada3, ada_row0, bsz, L, g_post, g_pre, woa, wof, wr2, br, cnt0):
    T = bsz * L
    tpb = L // TM
    row = lambda j: (lambda i: ((ada_row0 + i // tpb) * 6 + j, 0, 0))
    const2 = lambda i: (0, 0)
    tok = lambda i: (i, 0)
    lanes = lambda i: (0, i)
    return pl.pallas_call(
        _outproj_kernel,
        grid=(T // TM,),
        in_specs=[pl.BlockSpec((TM, NA_WIDTH), tok),
                  pl.BlockSpec((TM, FN_WIDTH), tok),
                  pl.BlockSpec((TM, D_MODEL), tok),
                  pl.BlockSpec((1, 1, D_MODEL), row(2)),
                  pl.BlockSpec((1, 1, D_MODEL), row(3)),
                  pl.BlockSpec((1, 1, D_MODEL), row(4)),
                  pl.BlockSpec((1, D_MODEL), const2),
                  pl.BlockSpec((1, D_MODEL), const2),
                  pl.BlockSpec((NA_WIDTH, D_MODEL), const2),
                  pl.BlockSpec((FN_WIDTH, D_MODEL), const2),
                  pl.BlockSpec((2 * N_EXPERTS, D_MODEL), const2),
                  pl.BlockSpec((N_EXPERTS, LANES), const2),
                  pl.BlockSpec((N_EXPERTS, LANES), const2)],
        out_specs=[pl.BlockSpec((TM, D_MODEL), tok),
                   pl.BlockSpec((TM,) + ROW_TILE, lambda i: (i, 0, 0)),
                   pl.BlockSpec((TOP_K, TM), lanes),
                   pl.BlockSpec((TOP_K, TM), lanes),
                   pl.BlockSpec((TOP_K, TM), lanes),
                   pl.BlockSpec((N_EXPERTS, LANES), const2)],
        out_shape=[jax.ShapeDtypeStruct((T, D_MODEL), F32),
                   jax.ShapeDtypeStruct((T,) + ROW_TILE, F32),
                   jax.ShapeDtypeStruct((TOP_K, T), jnp.int32),
                   jax.ShapeDtypeStruct((TOP_K, T), F32),
                   jax.ShapeDtypeStruct((TOP_K, T), jnp.int32),
                   jax.ShapeDtypeStruct((N_EXPERTS, LANES), F32)],
        scratch_shapes=[pltpu.VMEM((N_EXPERTS, LANES), F32)],
        compiler_params=_cparams(("arbitrary",)),
    )(na2d, fn2d, x2d, ada3, ada3, ada3, g_post, g_pre, woa, wof, wr2, br, cnt0)


def _zero_fill(pad_lo_ref, pad_n_ref, tail_ref, xs_hbm, zbuf, zsem, n_blocks, wait):
    def run(cp):
        cp.wait() if wait else cp.start()

    def expert(e, c):
        pos = pad_lo_ref[e]
        n = pad_n_ref[e]
        size = ZERO_ROWS
        while size >= 1:
            @pl.when((n & size) != 0)
            def _(pos=pos, size=size):
                run(pltpu.make_async_copy(zbuf.at[pl.ds(0, size)], xs_hbm.at[pl.ds(pos, size)], zsem))
            pos = pos + (n & size)
            size //= 2
        return c

    lax.fori_loop(0, N_EXPERTS, expert, 0)

    def tail(blk, c):
        for half in range(MOE_BLOCK // ZERO_ROWS):
            run(pltpu.make_async_copy(zbuf, xs_hbm.at[pl.ds(blk * MOE_BLOCK + half * ZERO_ROWS, ZERO_ROWS)], zsem))
        return c

    lax.fori_loop(tail_ref[0], n_blocks, tail, 0)


def _dispatch_kernel(pad_lo_ref, pad_n_ref, tail_ref, dest_ref, h2a_ref, h2b_ref, xs_hbm, zbuf, sem, zsem,
                     *, n_blocks, tiles_a):
    i = pl.program_id(0)

    @pl.when(i == 0)
    def _():
        zbuf[...] = jnp.zeros_like(zbuf)
        _zero_fill(pad_lo_ref, pad_n_ref, tail_ref, xs_hbm, zbuf, zsem, n_blocks, wait=False)
        _zero_fill(pad_lo_ref, pad_n_ref, tail_ref, xs_hbm, zbuf, zsem, n_blocks, wait=True)

    def scatter_tile(h2_ref):
        def rows(g, c):
            for u in range(DISPATCH_UNROLL):
                r = g * DISPATCH_UNROLL + u
                for k in range(TOP_K):
                    pltpu.make_async_copy(h2_ref.at[r], xs_hbm.at[dest_ref[k, r]], sem).start()
            return c

        lax.fori_loop(0, TM // DISPATCH_UNROLL, rows, 0)
        for k in range(TOP_K):
            pltpu.make_async_copy(h2_ref, xs_hbm.at[pl.ds(0, TM)], sem).wait()

    @pl.when(i < tiles_a)
    def _():
        scatter_tile(h2a_ref)

    @pl.when(i >= tiles_a)
    def _():
        scatter_tile(h2b_ref)


def _dispatch(pad_lo, pad_n, tail, dest, h2a, h2b, n_blocks):
    tiles_a = h2a.shape[0] // TM
    tiles_b = h2b.shape[0] // TM
    P = n_blocks * MOE_BLOCK
    kern = functools.partial(_dispatch_kernel, n_blocks=n_blocks, tiles_a=tiles_a)
    return pl.pallas_call(
        kern,
        grid_spec=pltpu.PrefetchScalarGridSpec(
            num_scalar_prefetch=3,
            grid=(tiles_a + tiles_b,),
            in_specs=[pl.BlockSpec((TOP_K, TM), lambda i, a, b, c: (0, i), memory_space=pltpu.SMEM),
                      pl.BlockSpec((TM,) + ROW_TILE, lambda i, a, b, c: (jnp.minimum(i, tiles_a - 1), 0, 0)),
                      pl.BlockSpec((TM,) + ROW_TILE, lambda i, a, b, c: (jnp.maximum(i - tiles_a, 0), 0, 0))],
            out_specs=pl.BlockSpec(memory_space=pl.ANY),
            scratch_shapes=[pltpu.VMEM((ZERO_ROWS,) + ROW_TILE, F32),
                            pltpu.SemaphoreType.DMA,
                            pltpu.SemaphoreType.DMA]),
        out_shape=jax.ShapeDtypeStruct((P,) + ROW_TILE, F32),
        compiler_params=_cparams(("arbitrary",)),
    )(pad_lo, pad_n, tail, dest, h2a, h2b)


def _expert_kernel(be_ref, x_ref, wgu_ref, bgu_ref, wd_ref, bd_ref, y_ref):
    del be_ref
    x = _from_row_tiles(x_ref).astype(BF16)
    gu = jnp.dot(x, wgu_ref[0], preferred_element_type=F32) + bgu_ref[0]
    glu = jnp.minimum(gu[:, :D_FF], SWIGLU_LIMIT)
    lin = jnp.clip(gu[:, D_FF:], -SWIGLU_LIMIT, SWIGLU_LIMIT)
    act = glu * jax.nn.sigmoid(SWIGLU_ALPHA * glu) * (lin + 1.0)
    y = jnp.dot(act.astype(BF16), wd_ref[0], preferred_element_type=F32) + bd_ref[0]
    _to_row_tiles(y_ref, y)


def _experts(block_e, xs, wgu, bgu, wd, bd):
    n_blocks = block_e.shape[0]
    return pl.pallas_call(
        _expert_kernel,
        grid_spec=pltpu.PrefetchScalarGridSpec(
            num_scalar_prefetch=1,
            grid=(n_blocks,),
            in_specs=[pl.BlockSpec((MOE_BLOCK,) + ROW_TILE, lambda i, be: (i, 0, 0)),
                      pl.BlockSpec((1, D_MODEL, 2 * D_FF), lambda i, be: (be[i], 0, 0)),
                      pl.BlockSpec((1, 1, 2 * D_FF), lambda i, be: (be[i], 0, 0)),
                      pl.BlockSpec((1, D_FF, D_MODEL), lambda i, be: (be[i], 0, 0)),
                      pl.BlockSpec((1, 1, D_MODEL), lambda i, be: (be[i], 0, 0))],
            out_specs=pl.BlockSpec((MOE_BLOCK,) + ROW_TILE, lambda i, be: (i, 0, 0))),
        out_shape=jax.ShapeDtypeStruct(xs.shape, F32),
        compiler_params=_cparams(("arbitrary",)),
    )(block_e, xs, wgu, bgu, wd, bd)


def _combine_fetch(dest_ref, ys_hbm, ybuf, sem, buf):
    def rows(g, c):
        for u in range(DISPATCH_UNROLL):
            r = g * DISPATCH_UNROLL + u
            for k in range(TOP_K):
                pltpu.make_async_copy(ys_hbm.at[dest_ref[k, r]], ybuf.at[buf, k, r], sem.at[buf]).start()
        return c

    lax.fori_loop(0, CMB_TM // DISPATCH_UNROLL, rows, 0)


def _combine_wait(ys_hbm, ybuf, sem, buf):
    for k in range(TOP_K):
        pltpu.make_async_copy(ys_hbm.at[pl.ds(0, CMB_TM)], ybuf.at[buf, k], sem.at[buf]).wait()


def _combine_kernel(cur_ref, nxt_ref, ys_hbm, gate_ref, x1_ref, gt2_ref, g_ref, o_ref, ybuf, sem, *, n_tiles):
    i = pl.program_id(0)
    buf = i % 2

    @pl.when(i == 0)
    def _():
        _combine_fetch(cur_ref, ys_hbm, ybuf, sem, 0)

    _combine_fetch(nxt_ref, ys_hbm, ybuf, sem, 1 - buf)
    _combine_wait(ys_hbm, ybuf, sem, buf)

    chunks = []
    for s in range(ROW_TILE[0]):
        acc = gate_ref[:, 0:1] * ybuf[buf, 0, :, s, :]
        for k in range(1, TOP_K):
            acc = acc + gate_ref[:, k:k + 1] * ybuf[buf, k, :, s, :]
        chunks.append(acc)
    moe = jnp.concatenate(chunks, axis=1)
    o_ref[...] = x1_ref[...] + gt2_ref[0] * _rms(moe, g_ref[...])

    @pl.when(i == n_tiles - 1)
    def _():
        _combine_wait(ys_hbm, ybuf, sem, 1 - buf)


def _combine(dest, ys, gates_tk, x1, ada3, ada_row0, bsz, L, g_post):
    T = bsz * L
    tpb = L // CMB_TM
    n_tiles = T // CMB_TM
    return pl.pallas_call(
        functools.partial(_combine_kernel, n_tiles=n_tiles),
        grid=(n_tiles,),
        in_specs=[pl.BlockSpec((TOP_K, CMB_TM), lambda i: (0, i), memory_space=pltpu.SMEM),
                  pl.BlockSpec((TOP_K, CMB_TM), lambda i: (0, jnp.minimum(i + 1, n_tiles - 1)),
                               memory_space=pltpu.SMEM),
                  pl.BlockSpec(memory_space=pl.ANY),
                  pl.BlockSpec((CMB_TM, TOP_K), lambda i: (i, 0)),
                  pl.BlockSpec((CMB_TM, D_MODEL), lambda i: (i, 0)),
                  pl.BlockSpec((1, 1, D_MODEL), lambda i: ((ada_row0 + i // tpb) * 6 + 5, 0, 0)),
                  pl.BlockSpec((1, D_MODEL), lambda i: (0, 0))],
        out_specs=pl.BlockSpec((CMB_TM, D_MODEL), lambda i: (i, 0)),
        out_shape=jax.ShapeDtypeStruct((T, D_MODEL), F32),
        scratch_shapes=[pltpu.VMEM((2, TOP_K, CMB_TM) + ROW_TILE, F32),
                        pltpu.SemaphoreType.DMA((2,))],
        compiler_params=_cparams(("arbitrary",)),
    )(dest, dest, ys, gates_tk, x1, ada3, g_post)


def kernel(x_prompt, x_sample, c_prompt, c_sample, w_ada, b_ada, g_pre_mix, g_post_mix, g_pre_ffn, g_post_ffn,
           w_in, rpb, w_fourier, g_na, g_fn, w_out, w_router, b_router, w_gate_up, b_gate_up, w_down, b_down):
    assert w_ada.shape[0] == 1, "single layer"
    groups = [(x_prompt, c_prompt), (x_sample, c_sample)]
    for x, _ in groups:
        assert x.shape[1] % ATT_TOKENS == 0 and x.shape[1] // GRID_W >= 12 and x.shape[2] == D_MODEL

    row2 = lambda a: a[0].reshape(1, -1)
    w_in0 = w_in[0]
    wq = (w_in0[:, :NA_WIDTH] * HEAD_DIM ** -0.5).astype(BF16)
    wkt = w_in0[:, NA_WIDTH:2 * NA_WIDTH].T.astype(BF16)
    wv = w_in0[:, 2 * NA_WIDTH:3 * NA_WIDTH].astype(BF16)
    wu = w_in0[:, 3 * NA_WIDTH:].astype(BF16)
    cs = _channel_table()
    bias = _bias_table(rpb[0])
    wf = w_fourier[0].astype(BF16)
    woa = w_out[0, :NA_WIDTH].astype(BF16)
    wof = w_out[0, NA_WIDTH:].astype(BF16)
    wr_t = w_router[0].T
    wr_hi = wr_t.astype(BF16)
    wr_lo = (wr_t - wr_hi.astype(F32)).astype(BF16)
    wr2 = jnp.concatenate([wr_hi, wr_lo], axis=0)
    br = jnp.broadcast_to(b_router[0][:, None], (N_EXPERTS, LANES))
    wgu = w_gate_up[0].astype(BF16)
    wd = w_down[0].astype(BF16)
    bgu = b_gate_up[0].reshape(N_EXPERTS, 1, 2 * D_FF)
    bd = b_down[0].reshape(N_EXPERTS, 1, D_MODEL)

    c_all = jnp.concatenate([c_prompt, c_sample], axis=0)
    ada3 = _ada(c_all, w_ada[0], b_ada[0]).reshape(c_all.shape[0] * 6, 1, D_MODEL)

    counts = jnp.zeros((N_EXPERTS, LANES), F32)
    per_group = []
    ada_row0 = 0
    for x, _ in groups:
        bsz, L, _ = x.shape
        x2d = x.reshape(bsz * L, D_MODEL)
        q, kt4, v, ucs = _inproj(x2d, ada3, ada_row0, bsz, L, row2(g_pre_mix), wq, wkt, wv, wu, cs)
        na = _attention(q.reshape(bsz, L, NA_WIDTH), kt4, v.reshape(bsz, L, NA_WIDTH), bias, row2(g_na))
        fn = _fourier(_dft_tables(L), ucs.reshape(bsz, 2 * L, FN_WIDTH), wf, row2(g_fn))
        x1, h2, e_t, gate_t, rank_t, counts = _outproj(
            na.reshape(bsz * L, NA_WIDTH), fn.reshape(bsz * L, FN_WIDTH), x2d, ada3, ada_row0, bsz, L,
            row2(g_post_mix), row2(g_pre_ffn), woa, wof, wr2, br, counts)
        per_group.append((x1, h2, e_t, gate_t, rank_t, bsz, L, ada_row0))
        ada_row0 += bsz

    T = sum(g[5] * g[6] for g in per_group)
    N = T * TOP_K
    n_blocks = -(-(N + N_EXPERTS * (MOE_BLOCK - 1)) // MOE_BLOCK)
    cnt = counts[:, 0].astype(jnp.int32)
    padded = ((cnt + MOE_BLOCK - 1) // MOE_BLOCK) * MOE_BLOCK
    pend = jnp.cumsum(padded)
    pstart = pend - padded
    block_e = jnp.minimum(
        jnp.searchsorted(pend, jnp.arange(n_blocks, dtype=jnp.int32) * MOE_BLOCK, side='right'),
        N_EXPERTS - 1).astype(jnp.int32)
    pad_lo = pstart + cnt
    pad_n = padded - cnt
    tail = (pend[-1:] // MOE_BLOCK).astype(jnp.int32)
    experts = jnp.arange(N_EXPERTS, dtype=jnp.int32)[:, None, None]

    dests = [g[4] + jnp.sum(jnp.where(g[2][None] == experts, pstart[:, None, None], 0), axis=0)
             for g in per_group]
    xs = _dispatch(pad_lo, pad_n, tail, jnp.concatenate(dests, axis=1), per_group[0][1], per_group[1][1], n_blocks)
    ys = _experts(block_e, xs, wgu, bgu, wd, bd)

    outs = []
    for (x1, _, _, gate_t, _, bsz, L, row0), dest in zip(per_group, dests):
        out = _combine(dest, ys, gate_t.T, x1, ada3, row0, bsz, L, row2(g_post_ffn))
        outs.append(out.reshape(bsz, L, D_MODEL))
    return tuple(outs)
```

```python
import functools

import numpy as np
import jax
import jax.numpy as jnp
from jax import lax
from jax.experimental import pallas as pl
from jax.experimental.pallas import tpu as pltpu

F32 = jnp.float32
BF16 = jnp.bfloat16

D_MODEL = 1024
GRID_W = 64
NA_WIDTH = 512
HEAD_DIM = 64
NA_HEADS = 8
FN_WIDTH = 512
FN_GROUPS = 4
FN_GROUP_DIM = 128
WIN_ROWS = 8
WIN_COLS = 16
N_EXPERTS = 32
TOP_K = 4
D_FF = 1024
SWIGLU_ALPHA = 1.702
SWIGLU_LIMIT = 7.0
MOE_BLOCK = 256
EPS = 1e-6

LANES = 128
PAIR_TOKENS = 2 * GRID_W
SLAB_ROWS = 10
SLAB_TOKENS = SLAB_ROWS * GRID_W
N_BIAS_VARIANTS = 5
MASK_VALUE = -1e30

TM = 512
SUBLANES = 8
SORT_CHUNK = 256
TILE_ROWS = -(-(TM * TOP_K + N_EXPERTS * (SUBLANES - 1)) // SORT_CHUNK) * SORT_CHUNK
ATT_TOKENS = 1024
ZERO_ROWS = 128
VMEM_LIMIT = 56 * 1024 * 1024


def _cparams(sem):
    return pltpu.CompilerParams(dimension_semantics=sem, vmem_limit_bytes=VMEM_LIMIT)


def _rms(x, g):
    return x * lax.rsqrt(jnp.mean(x * x, axis=-1, keepdims=True) + EPS) * g


def _pow2_sizes(limit):
    sizes = []
    while limit >= SUBLANES:
        sizes.append(limit)
        limit //= 2
    return sizes


def _round_up(n, m):
    return (n + m - 1) // m * m


def _ada_kernel(c_ref, w_ref, b_ref, o_ref):
    c = c_ref[...]
    s = c * jax.nn.sigmoid(c)
    o_ref[...] = jnp.dot(s.astype(BF16), w_ref[...].astype(BF16), preferred_element_type=F32) + b_ref[...]


def _ada(c_all, w_ada, b_ada):
    nb = c_all.shape[0]
    n_out = w_ada.shape[1]
    tn = 1536
    return pl.pallas_call(
        _ada_kernel,
        grid=(n_out // tn,),
        in_specs=[pl.BlockSpec((nb, D_MODEL), lambda j: (0, 0)),
                  pl.BlockSpec((D_MODEL, tn), lambda j: (0, j)),
                  pl.BlockSpec((1, tn), lambda j: (0, j))],
        out_specs=pl.BlockSpec((nb, tn), lambda j: (0, j)),
        out_shape=jax.ShapeDtypeStruct((nb, n_out), F32),
        compiler_params=_cparams(("arbitrary",)),
    )(c_all, w_ada, b_ada.reshape(1, n_out))


def _inproj_kernel(x_ref, sh_ref, sc_ref, g_ref, wq_ref, wkt_ref, wv_ref, wu_ref, cs_ref,
                   q_ref, kt_ref, v_ref, ucs_ref):
    h = _rms(x_ref[...], g_ref[...]) * (1.0 + sc_ref[0]) + sh_ref[0]
    hb = h.astype(BF16)
    q_ref[...] = jnp.dot(hb, wq_ref[...], preferred_element_type=F32).astype(BF16)
    v_ref[...] = jnp.dot(hb, wv_ref[...], preferred_element_type=F32).astype(BF16)
    kt = lax.dot_general(wkt_ref[...], hb, (((1,), (1,)), ((), ())), preferred_element_type=F32).astype(BF16)
    for c in range(TM // PAIR_TOKENS):
        kt_ref[0, c] = kt[:, c * PAIR_TOKENS:(c + 1) * PAIR_TOKENS]
    u = jnp.dot(hb, wu_ref[...], preferred_element_type=F32).astype(BF16)
    for g in range(FN_GROUPS):
        sl = slice(g * FN_GROUP_DIM, (g + 1) * FN_GROUP_DIM)
        r = jnp.dot(u[:, sl], cs_ref[...], preferred_element_type=F32)
        ucs_ref[0, 0, :, sl] = r[:, :FN_GROUP_DIM].astype(BF16)
        ucs_ref[0, 1, :, sl] = r[:, FN_GROUP_DIM:].astype(BF16)


def _inproj(x2d, ada3, ada_row0, bsz, L, g_pre, wq, wkt, wv, wu, cs):
    T = bsz * L
    tpb = L // TM
    row = lambda j: (lambda i: ((ada_row0 + i // tpb) * 6 + j, 0, 0))
    const2 = lambda i: (0, 0)
    return pl.pallas_call(
        _inproj_kernel,
        grid=(T // TM,),
        in_specs=[pl.BlockSpec((TM, D_MODEL), lambda i: (i, 0)),
                  pl.BlockSpec((1, 1, D_MODEL), row(0)),
                  pl.BlockSpec((1, 1, D_MODEL), row(1)),
                  pl.BlockSpec((1, D_MODEL), const2),
                  pl.BlockSpec((D_MODEL, NA_WIDTH), const2),
                  pl.BlockSpec((NA_WIDTH, D_MODEL), const2),
                  pl.BlockSpec((D_MODEL, NA_WIDTH), const2),
                  pl.BlockSpec((D_MODEL, FN_WIDTH), const2),
                  pl.BlockSpec((FN_GROUP_DIM, 2 * FN_GROUP_DIM), const2)],
        out_specs=[pl.BlockSpec((TM, NA_WIDTH), lambda i: (i, 0)),
                   pl.BlockSpec((1, TM // PAIR_TOKENS, NA_WIDTH, PAIR_TOKENS),
                                lambda i: (i // tpb, i % tpb, 0, 0)),
                   pl.BlockSpec((TM, NA_WIDTH), lambda i: (i, 0)),
                   pl.BlockSpec((1, 2, TM, FN_WIDTH), lambda i: (i // tpb, 0, i % tpb, 0))],
        out_shape=[jax.ShapeDtypeStruct((T, NA_WIDTH), BF16),
                   jax.ShapeDtypeStruct((bsz, L // PAIR_TOKENS, NA_WIDTH, PAIR_TOKENS), BF16),
                   jax.ShapeDtypeStruct((T, NA_WIDTH), BF16),
                   jax.ShapeDtypeStruct((bsz, 2, L, FN_WIDTH), BF16)],
        compiler_params=_cparams(("arbitrary",)),
    )(x2d, ada3, ada3, g_pre, wq, wkt, wv, wu, cs)


def _bias_table(rpb):
    rows = 32
    n_pairs = rows // 2
    c = np.arange(GRID_W)
    kc0 = np.clip(c - WIN_COLS // 2, 0, GRID_W - WIN_COLS)
    col_ok = (c[None, :] >= kc0[:, None]) & (c[None, :] < kc0[:, None] + WIN_COLS)
    ci = c[None, :] - c[:, None] + WIN_COLS - 1
    pick = ((ci[None] == np.arange(2 * WIN_COLS - 1)[:, None, None]) & col_ok[None]).astype(np.float32)
    toe = jnp.einsum('hrk,kcd->hrcd', rpb, jnp.asarray(pick), precision=lax.Precision.HIGHEST)
    toe = jnp.where(col_ok, toe, MASK_VALUE)
    masked = jnp.full((NA_HEADS, GRID_W, GRID_W), MASK_VALUE, F32)
    variants = []
    for j in (0, 1, 2, n_pairs - 2, n_pairs - 1):
        start = int(np.clip(2 * j - 4, 0, rows - SLAB_ROWS))
        halves = []
        for rl in range(2):
            r = 2 * j + rl
            kr0 = int(np.clip(r - WIN_ROWS // 2, 0, rows - WIN_ROWS))
            blocks = []
            for i in range(SLAB_ROWS):
                kr = start + i
                blocks.append(toe[:, kr - r + WIN_ROWS - 1] if kr0 <= kr < kr0 + WIN_ROWS else masked)
            halves.append(jnp.concatenate(blocks, axis=2))
        variants.append(jnp.concatenate(halves, axis=1))
    return jnp.stack(variants).astype(BF16)


def _attn_kernel(q_ref, kt_ref, v_ref, bias_ref, g_ref, o_ref, *, n_pairs):
    step = pl.program_id(1)
    pairs_per_step = ATT_TOKENS // PAIR_TOKENS
    lane = lax.broadcasted_iota(jnp.int32, (PAIR_TOKENS, 2 * HEAD_DIM), 1)
    first_head = lane < HEAD_DIM

    def pair_body(p, carry):
        j = step * pairs_per_step + p
        start2 = jnp.clip(j - 2, 0, n_pairs - SLAB_ROWS // 2)
        variant = jnp.where(j < 2, j, jnp.where(j >= n_pairs - 2, j - (n_pairs - 2) + 3, 2))
        qp = q_ref[0, pl.ds(pl.multiple_of(p * PAIR_TOKENS, PAIR_TOKENS), PAIR_TOKENS), :]
        kt5 = kt_ref[0, pl.ds(start2, SLAB_ROWS // 2)]
        vs = v_ref[0, pl.ds(pl.multiple_of(start2 * PAIR_TOKENS, PAIR_TOKENS), SLAB_TOKENS), :]
        outs = []
        for hp in range(NA_HEADS // 2):
            hs = slice(hp * 2 * HEAD_DIM, (hp + 1) * 2 * HEAD_DIM)
            q2 = qp[:, hs]
            kt2 = jnp.concatenate([kt5[c, hs, :] for c in range(SLAB_ROWS // 2)], axis=1)
            v2 = vs[:, hs]
            o_pair = None
            for hh in range(2):
                qm = jnp.where(first_head if hh == 0 else ~first_head, q2, jnp.zeros_like(q2))
                s = jnp.dot(qm, kt2, preferred_element_type=F32)
                s = s + bias_ref[variant, hp * 2 + hh].astype(F32)
                m = jnp.max(s, axis=-1, keepdims=True)
                e = jnp.exp(s - m)
                l = jnp.sum(e, axis=-1, keepdims=True)
                o = jnp.dot(e.astype(BF16), v2, preferred_element_type=F32) / l
                o_pair = o if hh == 0 else jnp.where(first_head, o_pair, o)
            outs.append(o_pair)
        a = jnp.concatenate(outs, axis=1)
        o_ref[0, pl.ds(pl.multiple_of(p * PAIR_TOKENS, PAIR_TOKENS), PAIR_TOKENS), :] = _rms(a, g_ref[...]).astype(BF16)
        return carry

    lax.fori_loop(0, pairs_per_step, pair_body, 0)


def _attention(q3, kt4, v3, bias, g_na):
    bsz, L, _ = q3.shape
    n_pairs = L // PAIR_TOKENS
    return pl.pallas_call(
        functools.partial(_attn_kernel, n_pairs=n_pairs),
        grid=(bsz, L // ATT_TOKENS),
        in_specs=[pl.BlockSpec((1, ATT_TOKENS, NA_WIDTH), lambda b, s: (b, s, 0)),
                  pl.BlockSpec((1, n_pairs, NA_WIDTH, PAIR_TOKENS), lambda b, s: (b, 0, 0, 0)),
                  pl.BlockSpec((1, L, NA_WIDTH), lambda b, s: (b, 0, 0)),
                  pl.BlockSpec((N_BIAS_VARIANTS, NA_HEADS, PAIR_TOKENS, SLAB_TOKENS), lambda b, s: (0, 0, 0, 0)),
                  pl.BlockSpec((1, NA_WIDTH), lambda b, s: (0, 0))],
        out_specs=pl.BlockSpec((1, ATT_TOKENS, NA_WIDTH), lambda b, s: (b, s, 0)),
        out_shape=jax.ShapeDtypeStruct((bsz, L, NA_WIDTH), BF16),
        compiler_params=_cparams(("arbitrary", "arbitrary")),
    )(q3, kt4, v3, bias, g_na)


def _dft_tables(L):
    j = jnp.arange(L, dtype=jnp.int32)
    ang = ((j[:, None] * j[None, :]) % L).astype(F32) * (2.0 * np.pi / L)
    return jnp.concatenate([jnp.cos(ang), -jnp.sin(ang)], axis=1).astype(BF16)


def _channel_table():
    j = jnp.arange(FN_GROUP_DIM, dtype=jnp.int32)
    ang = ((j[:, None] * j[None, :]) % FN_GROUP_DIM).astype(F32) * (2.0 * np.pi / FN_GROUP_DIM)
    return jnp.concatenate([jnp.cos(ang), jnp.sin(ang)], axis=1).astype(BF16)


def _fourier_kernel(tab_ref, ucs_ref, wf_ref, g_ref, o_ref):
    z = jnp.dot(tab_ref[...], ucs_ref[0], preferred_element_type=F32).astype(BF16)
    ys = [jnp.dot(z[:, g * FN_GROUP_DIM:(g + 1) * FN_GROUP_DIM], wf_ref[g], preferred_element_type=F32)
          for g in range(FN_GROUPS)]
    o_ref[0] = _rms(jnp.concatenate(ys, axis=1), g_ref[...]).astype(BF16)


def _fourier(tab, ucs3, wf, g_fn):
    bsz, L2, _ = ucs3.shape
    L = L2 // 2
    return pl.pallas_call(
        _fourier_kernel,
        grid=(bsz, L // TM),
        in_specs=[pl.BlockSpec((TM, L2), lambda b, t: (t, 0)),
                  pl.BlockSpec((1, L2, FN_WIDTH), lambda b, t: (b, 0, 0)),
                  pl.BlockSpec((FN_GROUPS, FN_GROUP_DIM, FN_GROUP_DIM), lambda b, t: (0, 0, 0)),
                  pl.BlockSpec((1, FN_WIDTH), lambda b, t: (0, 0))],
        out_specs=pl.BlockSpec((1, TM, FN_WIDTH), lambda b, t: (b, t, 0)),
        out_shape=jax.ShapeDtypeStruct((bsz, L, FN_WIDTH), BF16),
        compiler_params=_cparams(("arbitrary", "arbitrary")),
    )(tab, ucs3, wf, g_fn)


def _outproj_kernel(na_ref, fn_ref, x_ref, gt1_ref, sh2_ref, sc2_ref, gpost_ref, gpre_ref,
                    woa_ref, wof_ref, wr_ref, br_ref,
                    x1_ref, h2_ref, gate_ref, pos_ref, cnt_ref):
    mix = (jnp.dot(na_ref[...], woa_ref[...], preferred_element_type=F32)
           + jnp.dot(fn_ref[...], wof_ref[...], preferred_element_type=F32))
    x1 = x_ref[...] + gt1_ref[0] * _rms(mix, gpost_ref[...])
    x1_ref[...] = x1
    h2 = _rms(x1, gpre_ref[...]) * (1.0 + sc2_ref[0]) + sh2_ref[0]
    hi = h2.astype(BF16)
    h2_ref[...] = hi

    lo = (h2 - hi.astype(F32)).astype(BF16)
    nt = (((1,), (1,)), ((), ()))
    a = lax.dot_general(wr_ref[...], hi, nt, preferred_element_type=F32)
    b = lax.dot_general(wr_ref[0:N_EXPERTS, :], lo, nt, preferred_element_type=F32)
    logits = a[:N_EXPERTS] + a[N_EXPERTS:] + b + br_ref[:, 0:1]

    eidx = lax.broadcasted_iota(jnp.int32, (N_EXPERTS, TM), 0)
    vals, onehots = [], []
    cur = logits
    for k in range(TOP_K):
        m = jnp.max(cur, axis=0, keepdims=True)
        idx = jnp.min(jnp.where(cur == m, eidx, N_EXPERTS), axis=0, keepdims=True)
        hit = eidx == idx
        vals.append(m)
        onehots.append(hit)
        cur = jnp.where(hit, -jnp.inf, cur)
    exps = [jnp.exp(v - vals[0]) for v in vals]
    den = exps[0] + exps[1] + exps[2] + exps[3]
    for k in range(TOP_K):
        gate_ref[k:k + 1, :] = exps[k] / den

    oh = [h.astype(F32) for h in onehots]
    oh_sum = oh[0] + oh[1] + oh[2] + oh[3]
    s_i = lax.broadcasted_iota(jnp.int32, (TM, TM), 0)
    t_i = lax.broadcasted_iota(jnp.int32, (TM, TM), 1)
    earlier = jnp.where(s_i < t_i, 1.0, 0.0).astype(BF16)
    same_expert_before = jnp.dot(oh_sum.astype(BF16), earlier, preferred_element_type=F32)
    counts = jnp.broadcast_to(jnp.sum(oh_sum, axis=1, keepdims=True), (N_EXPERTS, LANES))
    cnt_ref[...] = counts
    run_tiles = jnp.floor((counts + (SUBLANES - 1)) * (1.0 / SUBLANES))
    e_r = lax.broadcasted_iota(jnp.int32, (N_EXPERTS, N_EXPERTS), 0)
    e_c = lax.broadcasted_iota(jnp.int32, (N_EXPERTS, N_EXPERTS), 1)
    lower = jnp.where(e_c < e_r, 1.0, 0.0).astype(BF16)
    run_start = SUBLANES * jnp.dot(lower, run_tiles.astype(BF16), preferred_element_type=F32)[:, 0:1]
    base = run_start + same_expert_before
    for k in range(TOP_K):
        pos_ref[k:k + 1, :] = jnp.sum(oh[k] * base, axis=0, keepdims=True).astype(jnp.int32)


def _outproj(na2d, fn2d, x2d, ada3, ada_row0, bsz, L, g_post, g_pre, woa, wof, wr2, br):
    T = bsz * L
    tpb = L // TM
    row = lambda j: (lambda i: ((ada_row0 + i // tpb) * 6 + j, 0, 0))
    const2 = lambda i: (0, 0)
    tok = lambda i: (i, 0)
    lanes = lambda i: (0, i)
    return pl.pallas_call(
        _outproj_kernel,
        grid=(T // TM,),
        in_specs=[pl.BlockSpec((TM, NA_WIDTH), tok),
                  pl.BlockSpec((TM, FN_WIDTH), tok),
                  pl.BlockSpec((TM, D_MODEL), tok),
                  pl.BlockSpec((1, 1, D_MODEL), row(2)),
                  pl.BlockSpec((1, 1, D_MODEL), row(3)),
                  pl.BlockSpec((1, 1, D_MODEL), row(4)),
                  pl.BlockSpec((1, D_MODEL), const2),
                  pl.BlockSpec((1, D_MODEL), const2),
                  pl.BlockSpec((NA_WIDTH, D_MODEL), const2),
                  pl.BlockSpec((FN_WIDTH, D_MODEL), const2),
                  pl.BlockSpec((2 * N_EXPERTS, D_MODEL), const2),
                  pl.BlockSpec((N_EXPERTS, LANES), const2)],
        out_specs=[pl.BlockSpec((TM, D_MODEL), tok),
                   pl.BlockSpec((TM, D_MODEL), tok),
                   pl.BlockSpec((TOP_K, TM), lanes),
                   pl.BlockSpec((TOP_K, TM), lanes),
                   pl.BlockSpec((N_EXPERTS, LANES), tok)],
        out_shape=[jax.ShapeDtypeStruct((T, D_MODEL), F32),
                   jax.ShapeDtypeStruct((T, D_MODEL), BF16),
                   jax.ShapeDtypeStruct((TOP_K, T), F32),
                   jax.ShapeDtypeStruct((TOP_K, T), jnp.int32),
                   jax.ShapeDtypeStruct((T // TM * N_EXPERTS, LANES), F32)],
        compiler_params=_cparams(("arbitrary",)),
    )(na2d, fn2d, x2d, ada3, ada3, ada3, g_post, g_pre, woa, wof, wr2, br)


def _zero_fill(pad_lo_ref, pad_n_ref, tail_ref, xs_hbm, zbuf, zsem, n_blocks, wait):
    def run(cp):
        cp.wait() if wait else cp.start()

    def expert(e, c):
        pos = pad_lo_ref[e]
        n = pad_n_ref[e]
        for size in _pow2_sizes(ZERO_ROWS):
            @pl.when((n & size) != 0)
            def _(pos=pos, size=size):
                dst = xs_hbm.at[pl.ds(pl.multiple_of(pos, SUBLANES), size)]
                run(pltpu.make_async_copy(zbuf.at[pl.ds(0, size)], dst, zsem))
            pos = pos + (n & size)
        return c

    lax.fori_loop(0, N_EXPERTS, expert, 0)

    def tail(blk, c):
        for half in range(MOE_BLOCK // ZERO_ROWS):
            row0 = pl.multiple_of(blk * MOE_BLOCK + half * ZERO_ROWS, ZERO_ROWS)
            run(pltpu.make_async_copy(zbuf, xs_hbm.at[pl.ds(row0, ZERO_ROWS)], zsem))
        return c

    lax.fori_loop(tail_ref[0], n_blocks, tail, 0)


def _tile_pieces(cnt_ref, off_ref, dst_ref, tile, local_buf, sorted_hbm, sem, to_sorted, wait):
    def expert(e, c):
        j = tile * N_EXPERTS + e
        n = cnt_ref[j]
        src = off_ref[j]
        dst = dst_ref[j]
        for size in _pow2_sizes(TM):
            @pl.when((n & size) != 0)
            def _(src=src, dst=dst, size=size):
                loc = local_buf.at[pl.ds(pl.multiple_of(src, SUBLANES), size)]
                glob = sorted_hbm.at[pl.ds(pl.multiple_of(dst, SUBLANES), size)]
                cp = pltpu.make_async_copy(loc, glob, sem) if to_sorted else pltpu.make_async_copy(glob, loc, sem)
                cp.wait() if wait else cp.start()
            src = src + (n & size)
            dst = dst + (n & size)
        return c

    lax.fori_loop(0, N_EXPERTS, expert, 0)


def _dispatch_kernel(pad_lo_ref, pad_n_ref, tail_ref, cnt_ref, off_ref, dst_ref,
                     pos_ref, h2a_ref, h2b_ref, xs_hbm, sbuf, zbuf, sem, zsem, *, n_blocks, tiles_a, n_tiles):
    i = pl.program_id(0)
    buf = i % 2

    @pl.when(i == 0)
    def _():
        zbuf[...] = jnp.zeros_like(zbuf)
        _zero_fill(pad_lo_ref, pad_n_ref, tail_ref, xs_hbm, zbuf, zsem, n_blocks, wait=False)
        _zero_fill(pad_lo_ref, pad_n_ref, tail_ref, xs_hbm, zbuf, zsem, n_blocks, wait=True)

    @pl.when(i >= 2)
    def _():
        _tile_pieces(cnt_ref, off_ref, dst_ref, i - 2, sbuf.at[buf], xs_hbm, sem.at[buf], True, wait=True)

    def sort_tile(h2_ref):
        h = h2_ref[...]
        pos = pos_ref[...]
        for c in range(TILE_ROWS // SORT_CHUNK):
            row = lax.broadcasted_iota(jnp.int32, (SORT_CHUNK, TM), 0) + c * SORT_CHUNK
            sel = jnp.zeros((SORT_CHUNK, TM), F32)
            for k in range(TOP_K):
                sel = jnp.where(row == pos[k:k + 1, :], 1.0, sel)
            sbuf[buf, c * SORT_CHUNK:(c + 1) * SORT_CHUNK, :] = jnp.dot(sel.astype(BF16), h,
                                                                        preferred_element_type=F32)

    @pl.when(i < tiles_a)
    def _():
        sort_tile(h2a_ref)

    @pl.when(i >= tiles_a)
    def _():
        sort_tile(h2b_ref)

    _tile_pieces(cnt_ref, off_ref, dst_ref, i, sbuf.at[buf], xs_hbm, sem.at[buf], True, wait=False)

    @pl.when(i == n_tiles - 1)
    def _():
        if n_tiles >= 2:
            _tile_pieces(cnt_ref, off_ref, dst_ref, i - 1, sbuf.at[1 - buf], xs_hbm, sem.at[1 - buf], True, wait=True)
        _tile_pieces(cnt_ref, off_ref, dst_ref, i, sbuf.at[buf], xs_hbm, sem.at[buf], True, wait=True)


def _dispatch(pad_lo, pad_n, tail, cnt, off, dst, pos, h2a, h2b, n_blocks):
    tiles_a = h2a.shape[0] // TM
    tiles_b = h2b.shape[0] // TM
    n_tiles = tiles_a + tiles_b
    P = n_blocks * MOE_BLOCK
    kern = functools.partial(_dispatch_kernel, n_blocks=n_blocks, tiles_a=tiles_a, n_tiles=n_tiles)
    return pl.pallas_call(
        kern,
        grid_spec=pltpu.PrefetchScalarGridSpec(
            num_scalar_prefetch=6,
            grid=(n_tiles,),
            in_specs=[pl.BlockSpec((TOP_K, TM), lambda i, *_: (0, i)),
                      pl.BlockSpec((TM, D_MODEL), lambda i, *_: (jnp.minimum(i, tiles_a - 1), 0)),
                      pl.BlockSpec((TM, D_MODEL), lambda i, *_: (jnp.maximum(i - tiles_a, 0), 0))],
            out_specs=pl.BlockSpec(memory_space=pl.ANY),
            scratch_shapes=[pltpu.VMEM((2, TILE_ROWS, D_MODEL), F32),
                            pltpu.VMEM((ZERO_ROWS, D_MODEL), F32),
                            pltpu.SemaphoreType.DMA((2,)),
                            pltpu.SemaphoreType.DMA]),
        out_shape=jax.ShapeDtypeStruct((P, D_MODEL), F32),
        compiler_params=_cparams(("arbitrary",)),
    )(pad_lo, pad_n, tail, cnt, off, dst, pos, h2a, h2b)


def _expert_kernel(be_ref, used_ref, x_ref, wgu_ref, bgu_ref, wd_ref, bd_ref, y_ref):
    del be_ref
    live = pl.program_id(0) < used_ref[0]

    @pl.when(live)
    def _():
        x = x_ref[...].astype(BF16)
        gu = jnp.dot(x, wgu_ref[0], preferred_element_type=F32) + bgu_ref[0]
        glu = jnp.minimum(gu[:, :D_FF], SWIGLU_LIMIT)
        lin = jnp.clip(gu[:, D_FF:], -SWIGLU_LIMIT, SWIGLU_LIMIT)
        act = glu * jax.nn.sigmoid(SWIGLU_ALPHA * glu) * (lin + 1.0)
        y_ref[...] = jnp.dot(act.astype(BF16), wd_ref[0], preferred_element_type=F32) + bd_ref[0]

    @pl.when(jnp.logical_not(live))
    def _():
        y_ref[...] = jnp.zeros_like(y_ref)


def _experts(block_e, n_used, xs, wgu, bgu, wd, bd):
    n_blocks = block_e.shape[0]
    x_map = lambda i, be, used: (jnp.minimum(i, used[0] - 1), 0)
    return pl.pallas_call(
        _expert_kernel,
        grid_spec=pltpu.PrefetchScalarGridSpec(
            num_scalar_prefetch=2,
            grid=(n_blocks,),
            in_specs=[pl.BlockSpec((MOE_BLOCK, D_MODEL), x_map),
                      pl.BlockSpec((1, D_MODEL, 2 * D_FF), lambda i, be, used: (be[i], 0, 0)),
                      pl.BlockSpec((1, 1, 2 * D_FF), lambda i, be, used: (be[i], 0, 0)),
                      pl.BlockSpec((1, D_FF, D_MODEL), lambda i, be, used: (be[i], 0, 0)),
                      pl.BlockSpec((1, 1, D_MODEL), lambda i, be, used: (be[i], 0, 0))],
            out_specs=pl.BlockSpec((MOE_BLOCK, D_MODEL), lambda i, be, used: (i, 0))),
        out_shape=jax.ShapeDtypeStruct(xs.shape, F32),
        compiler_params=_cparams(("arbitrary",)),
    )(block_e, n_used, xs, wgu, bgu, wd, bd)


def _combine_kernel(cnt_ref, off_ref, dst_ref, ys_hbm, pos_ref, gate_ref, x1_ref, gt2_ref, g_ref, o_ref,
                    ybuf, sem, *, tile0, n_tiles):
    i = pl.program_id(0)
    buf = i % 2
    fetch = functools.partial(_tile_pieces, cnt_ref, off_ref, dst_ref)

    @pl.when(i == 0)
    def _():
        fetch(tile0, ybuf.at[0], ys_hbm, sem.at[0], False, wait=False)

    @pl.when(i + 1 < n_tiles)
    def _():
        fetch(tile0 + i + 1, ybuf.at[1 - buf], ys_hbm, sem.at[1 - buf], False, wait=False)

    fetch(tile0 + i, ybuf.at[buf], ys_hbm, sem.at[buf], False, wait=True)

    pos = pos_ref[...]
    gate = gate_ref[...]
    last = (tile0 + i) * N_EXPERTS + N_EXPERTS - 1
    n_local = off_ref[last] + cnt_ref[last]
    moe = jnp.zeros((TM, D_MODEL), F32)
    for c in range(TILE_ROWS // SORT_CHUNK):
        col = lax.broadcasted_iota(jnp.int32, (TM, SORT_CHUNK), 1) + c * SORT_CHUNK
        w = jnp.zeros((TM, SORT_CHUNK), F32)
        for k in range(TOP_K):
            w = jnp.where(col == pos[:, k:k + 1], gate[:, k:k + 1], w)
        rows = ybuf[buf, c * SORT_CHUNK:(c + 1) * SORT_CHUNK, :]
        if (c + 1) * SORT_CHUNK > TM * TOP_K:
            r = lax.broadcasted_iota(jnp.int32, (SORT_CHUNK, 1), 0) + c * SORT_CHUNK
            rows = jnp.where(r < n_local, rows, 0.0)
        moe = moe + jnp.dot(w.astype(BF16), rows.astype(BF16), preferred_element_type=F32)
    o_ref[...] = x1_ref[...] + gt2_ref[0] * _rms(moe, g_ref[...])


def _combine(cnt, off, dst, ys, tile0, pos_tk, gates_tk, x1, ada3, ada_row0, bsz, L, g_post):
    T = bsz * L
    tpb = L // TM
    n_tiles = T // TM
    return pl.pallas_call(
        functools.partial(_combine_kernel, tile0=tile0, n_tiles=n_tiles),
        grid_spec=pltpu.PrefetchScalarGridSpec(
            num_scalar_prefetch=3,
            grid=(n_tiles,),
            in_specs=[pl.BlockSpec(memory_space=pl.ANY),
                      pl.BlockSpec((TM, TOP_K), lambda i, *_: (i, 0)),
                      pl.BlockSpec((TM, TOP_K), lambda i, *_: (i, 0)),
                      pl.BlockSpec((TM, D_MODEL), lambda i, *_: (i, 0)),
                      pl.BlockSpec((1, 1, D_MODEL), lambda i, *_: ((ada_row0 + i // tpb) * 6 + 5, 0, 0)),
                      pl.BlockSpec((1, D_MODEL), lambda i, *_: (0, 0))],
            out_specs=pl.BlockSpec((TM, D_MODEL), lambda i, *_: (i, 0)),
            scratch_shapes=[pltpu.VMEM((2, TILE_ROWS, D_MODEL), F32),
                            pltpu.SemaphoreType.DMA((2,))]),
        out_shape=jax.ShapeDtypeStruct((T, D_MODEL), F32),
        compiler_params=_cparams(("arbitrary",)),
    )(cnt, off, dst, ys, pos_tk, gates_tk, x1, ada3, g_post)


def kernel(x_prompt, x_sample, c_prompt, c_sample, w_ada, b_ada, g_pre_mix, g_post_mix, g_pre_ffn, g_post_ffn,
           w_in, rpb, w_fourier, g_na, g_fn, w_out, w_router, b_router, w_gate_up, b_gate_up, w_down, b_down):
    assert w_ada.shape[0] == 1, "single layer"
    groups = [(x_prompt, c_prompt), (x_sample, c_sample)]
    for x, _ in groups:
        assert x.shape[1] % ATT_TOKENS == 0 and x.shape[1] // GRID_W >= 12 and x.shape[2] == D_MODEL

    row2 = lambda a: a[0].reshape(1, -1)
    w_in0 = w_in[0]
    wq = (w_in0[:, :NA_WIDTH] * HEAD_DIM ** -0.5).astype(BF16)
    wkt = w_in0[:, NA_WIDTH:2 * NA_WIDTH].T.astype(BF16)
    wv = w_in0[:, 2 * NA_WIDTH:3 * NA_WIDTH].astype(BF16)
    wu = w_in0[:, 3 * NA_WIDTH:].astype(BF16)
    cs = _channel_table()
    bias = _bias_table(rpb[0])
    wf = w_fourier[0].astype(BF16)
    woa = w_out[0, :NA_WIDTH].astype(BF16)
    wof = w_out[0, NA_WIDTH:].astype(BF16)
    wr_t = w_router[0].T
    wr_hi = wr_t.astype(BF16)
    wr_lo = (wr_t - wr_hi.astype(F32)).astype(BF16)
    wr2 = jnp.concatenate([wr_hi, wr_lo], axis=0)
    br = jnp.broadcast_to(b_router[0][:, None], (N_EXPERTS, LANES))
    wgu = w_gate_up[0].astype(BF16)
    wd = w_down[0].astype(BF16)
    bgu = b_gate_up[0].reshape(N_EXPERTS, 1, 2 * D_FF)
    bd = b_down[0].reshape(N_EXPERTS, 1, D_MODEL)

    c_all = jnp.concatenate([c_prompt, c_sample], axis=0)
    ada3 = _ada(c_all, w_ada[0], b_ada[0]).reshape(c_all.shape[0] * 6, 1, D_MODEL)

    per_group = []
    ada_row0 = 0
    for x, _ in groups:
        bsz, L, _ = x.shape
        x2d = x.reshape(bsz * L, D_MODEL)
        q, kt4, v, ucs = _inproj(x2d, ada3, ada_row0, bsz, L, row2(g_pre_mix), wq, wkt, wv, wu, cs)
        na = _attention(q.reshape(bsz, L, NA_WIDTH), kt4, v.reshape(bsz, L, NA_WIDTH), bias, row2(g_na))
        fn = _fourier(_dft_tables(L), ucs.reshape(bsz, 2 * L, FN_WIDTH), wf, row2(g_fn))
        x1, h2, gate_t, pos_t, cnt = _outproj(
            na.reshape(bsz * L, NA_WIDTH), fn.reshape(bsz * L, FN_WIDTH), x2d, ada3, ada_row0, bsz, L,
            row2(g_post_mix), row2(g_pre_ffn), woa, wof, wr2, br)
        per_group.append((x1, h2, gate_t, pos_t, cnt, bsz, L, ada_row0))
        ada_row0 += bsz

    T = sum(g[5] * g[6] for g in per_group)
    max_rows = T * TOP_K + (T // TM) * N_EXPERTS * (SUBLANES - 1) + N_EXPERTS * (MOE_BLOCK - 1)
    n_blocks = -(-max_rows // MOE_BLOCK)
    tile_cnt = jnp.concatenate([g[4][:, 0].reshape(-1, N_EXPERTS) for g in per_group], axis=0).astype(jnp.int32)
    tile_cnt = _round_up(tile_cnt, SUBLANES)
    total = jnp.sum(tile_cnt, axis=0)
    padded = ((total + MOE_BLOCK - 1) // MOE_BLOCK) * MOE_BLOCK
    pend = jnp.cumsum(padded)
    pstart = pend - padded
    run = jnp.cumsum(tile_cnt, axis=0) - tile_cnt
    tile_dst = (pstart[None, :] + run).reshape(-1)
    tile_off = (jnp.cumsum(tile_cnt, axis=1) - tile_cnt).reshape(-1)
    tile_cnt = tile_cnt.reshape(-1)
    block_e = jnp.minimum(
        jnp.searchsorted(pend, jnp.arange(n_blocks, dtype=jnp.int32) * MOE_BLOCK, side='right'),
        N_EXPERTS - 1).astype(jnp.int32)
    pad_lo = pstart + total
    pad_n = padded - total
    tail = (pend[-1:] // MOE_BLOCK).astype(jnp.int32)

    pos_all = jnp.concatenate([g[3] for g in per_group], axis=1)
    xs = _dispatch(pad_lo, pad_n, tail, tile_cnt, tile_off, tile_dst, pos_all,
                   per_group[0][1], per_group[1][1], n_blocks)
    ys = _experts(block_e, tail, xs, wgu, bgu, wd, bd)

    outs = []
    tile0 = 0
    for x1, _, gate_t, pos_t, _, bsz, L, row0 in per_group:
        out = _combine(tile_cnt, tile_off, tile_dst, ys, tile0, pos_t.T, gate_t.T, x1, ada3, row0, bsz, L,
                       row2(g_post_ffn))
        outs.append(out.reshape(bsz, L, D_MODEL))
        tile0 += bsz * L // TM
    return tuple(outs)
```

```python
import functools

import numpy as np
import jax
import jax.numpy as jnp
from jax import lax
from jax.experimental import pallas as pl
from jax.experimental.pallas import tpu as pltpu

F32 = jnp.float32
BF16 = jnp.bfloat16

D_MODEL = 1024
GRID_W = 64
NA_WIDTH = 512
HEAD_DIM = 64
NA_HEADS = 8
FN_WIDTH = 512
FN_GROUPS = 4
FN_GROUP_DIM = 128
WIN_ROWS = 8
WIN_COLS = 16
N_EXPERTS = 32
TOP_K = 4
D_FF = 1024
SWIGLU_ALPHA = 1.702
SWIGLU_LIMIT = 7.0
MOE_BLOCK = 512
EXPERT_ROWS = 256
EPS = 1e-6

LANES = 128
PAIR_TOKENS = 2 * GRID_W
SLAB_ROWS = 10
SLAB_TOKENS = SLAB_ROWS * GRID_W
N_BIAS_VARIANTS = 5
MASK_VALUE = -1e30

TM = 512
SUBLANES = 8
SORT_CHUNK = 256
TILE_ROWS = -(-(TM * TOP_K + N_EXPERTS * (SUBLANES - 1)) // SORT_CHUNK) * SORT_CHUNK
ATT_TOKENS = 1024
DFT_LANES = 4096
DFT_K1 = 8
ZERO_ROWS = MOE_BLOCK // 2
VMEM_LIMIT = 56 * 1024 * 1024


def _cparams(sem):
    return pltpu.CompilerParams(dimension_semantics=sem, vmem_limit_bytes=VMEM_LIMIT)


def _rms(x, g):
    return x * lax.rsqrt(jnp.mean(x * x, axis=-1, keepdims=True) + EPS) * g


def _pow2_sizes(limit):
    sizes = []
    while limit >= SUBLANES:
        sizes.append(limit)
        limit //= 2
    return sizes


def _round_up(n, m):
    return (n + m - 1) // m * m


def _ada_kernel(c_ref, w_ref, b_ref, o_ref):
    c = c_ref[...]
    s = c * jax.nn.sigmoid(c)
    o_ref[...] = jnp.dot(s.astype(BF16), w_ref[...].astype(BF16), preferred_element_type=F32) + b_ref[...]


def _ada(c_all, w_ada, b_ada):
    nb = c_all.shape[0]
    n_out = w_ada.shape[1]
    tn = 1536
    return pl.pallas_call(
        _ada_kernel,
        grid=(n_out // tn,),
        in_specs=[pl.BlockSpec((nb, D_MODEL), lambda j: (0, 0)),
                  pl.BlockSpec((D_MODEL, tn), lambda j: (0, j)),
                  pl.BlockSpec((1, tn), lambda j: (0, j))],
        out_specs=pl.BlockSpec((nb, tn), lambda j: (0, j)),
        out_shape=jax.ShapeDtypeStruct((nb, n_out), F32),
        compiler_params=_cparams(("arbitrary",)),
    )(c_all, w_ada, b_ada.reshape(1, n_out))


def _inproj_kernel(x_ref, sh_ref, sc_ref, g_ref, wq_ref, wkt_ref, wv_ref, wu_ref,
                   q_ref, kt_ref, v_ref, u_ref):
    h = _rms(x_ref[...], g_ref[...]) * (1.0 + sc_ref[0]) + sh_ref[0]
    hb = h.astype(BF16)
    q_ref[...] = jnp.dot(hb, wq_ref[...], preferred_element_type=F32).astype(BF16)
    v_ref[...] = jnp.dot(hb, wv_ref[...], preferred_element_type=F32).astype(BF16)
    kt = lax.dot_general(wkt_ref[...], hb, (((1,), (1,)), ((), ())), preferred_element_type=F32).astype(BF16)
    for c in range(TM // PAIR_TOKENS):
        kt_ref[0, c] = kt[:, c * PAIR_TOKENS:(c + 1) * PAIR_TOKENS]
    u_ref[...] = jnp.dot(hb, wu_ref[...], preferred_element_type=F32).astype(BF16)


def _inproj(x2d, ada3, ada_row0, bsz, L, g_pre, wq, wkt, wv, wu):
    T = bsz * L
    tpb = L // TM
    row = lambda j: (lambda i: ((ada_row0 + i // tpb) * 6 + j, 0, 0))
    const2 = lambda i: (0, 0)
    return pl.pallas_call(
        _inproj_kernel,
        grid=(T // TM,),
        in_specs=[pl.BlockSpec((TM, D_MODEL), lambda i: (i, 0)),
                  pl.BlockSpec((1, 1, D_MODEL), row(0)),
                  pl.BlockSpec((1, 1, D_MODEL), row(1)),
                  pl.BlockSpec((1, D_MODEL), const2),
                  pl.BlockSpec((D_MODEL, NA_WIDTH), const2),
                  pl.BlockSpec((NA_WIDTH, D_MODEL), const2),
                  pl.BlockSpec((D_MODEL, NA_WIDTH), const2),
                  pl.BlockSpec((D_MODEL, FN_WIDTH), const2)],
        out_specs=[pl.BlockSpec((TM, NA_WIDTH), lambda i: (i, 0)),
                   pl.BlockSpec((1, TM // PAIR_TOKENS, NA_WIDTH, PAIR_TOKENS),
                                lambda i: (i // tpb, i % tpb, 0, 0)),
                   pl.BlockSpec((TM, NA_WIDTH), lambda i: (i, 0)),
                   pl.BlockSpec((TM, FN_WIDTH), lambda i: (i, 0))],
        out_shape=[jax.ShapeDtypeStruct((T, NA_WIDTH), BF16),
                   jax.ShapeDtypeStruct((bsz, L // PAIR_TOKENS, NA_WIDTH, PAIR_TOKENS), BF16),
                   jax.ShapeDtypeStruct((T, NA_WIDTH), BF16),
                   jax.ShapeDtypeStruct((T, FN_WIDTH), BF16)],
        compiler_params=_cparams(("arbitrary",)),
    )(x2d, ada3, ada3, g_pre, wq, wkt, wv, wu)


def _bias_table(rpb):
    rows = 32
    n_pairs = rows // 2
    c = np.arange(GRID_W)
    kc0 = np.clip(c - WIN_COLS // 2, 0, GRID_W - WIN_COLS)
    col_ok = (c[None, :] >= kc0[:, None]) & (c[None, :] < kc0[:, None] + WIN_COLS)
    ci = c[None, :] - c[:, None] + WIN_COLS - 1
    pick = ((ci[None] == np.arange(2 * WIN_COLS - 1)[:, None, None]) & col_ok[None]).astype(np.float32)
    toe = jnp.einsum('hrk,kcd->hrcd', rpb, jnp.asarray(pick), precision=lax.Precision.HIGHEST)
    toe = jnp.where(col_ok, toe, MASK_VALUE)
    masked = jnp.full((NA_HEADS, GRID_W, GRID_W), MASK_VALUE, F32)
    variants = []
    for j in (0, 1, 2, n_pairs - 2, n_pairs - 1):
        start = int(np.clip(2 * j - 4, 0, rows - SLAB_ROWS))
        halves = []
        for rl in range(2):
            r = 2 * j + rl
            kr0 = int(np.clip(r - WIN_ROWS // 2, 0, rows - WIN_ROWS))
            blocks = []
            for i in range(SLAB_ROWS):
                kr = start + i
                blocks.append(toe[:, kr - r + WIN_ROWS - 1] if kr0 <= kr < kr0 + WIN_ROWS else masked)
            halves.append(jnp.concatenate(blocks, axis=2))
        variants.append(jnp.concatenate(halves, axis=1))
    return jnp.stack(variants).astype(BF16)


def _attn_kernel(q_ref, kt_ref, v_ref, bias_ref, g_ref, o_ref, *, n_pairs):
    step = pl.program_id(1)
    pairs_per_step = ATT_TOKENS // PAIR_TOKENS
    lane = lax.broadcasted_iota(jnp.int32, (PAIR_TOKENS, 2 * HEAD_DIM), 1)
    first_head = lane < HEAD_DIM

    def pair_body(p, carry):
        j = step * pairs_per_step + p
        start2 = jnp.clip(j - 2, 0, n_pairs - SLAB_ROWS // 2)
        variant = jnp.where(j < 2, j, jnp.where(j >= n_pairs - 2, j - (n_pairs - 2) + 3, 2))
        qp = q_ref[0, pl.ds(pl.multiple_of(p * PAIR_TOKENS, PAIR_TOKENS), PAIR_TOKENS), :]
        kt5 = kt_ref[0, pl.ds(start2, SLAB_ROWS // 2)]
        vs = v_ref[0, pl.ds(pl.multiple_of(start2 * PAIR_TOKENS, PAIR_TOKENS), SLAB_TOKENS), :]
        outs = []
        for hp in range(NA_HEADS // 2):
            hs = slice(hp * 2 * HEAD_DIM, (hp + 1) * 2 * HEAD_DIM)
            q2 = qp[:, hs]
            kt2 = jnp.concatenate([kt5[c, hs, :] for c in range(SLAB_ROWS // 2)], axis=1)
            v2 = vs[:, hs]
            o_pair = None
            for hh in range(2):
                qm = jnp.where(first_head if hh == 0 else ~first_head, q2, jnp.zeros_like(q2))
                s = jnp.dot(qm, kt2, preferred_element_type=F32)
                s = s + bias_ref[variant, hp * 2 + hh].astype(F32)
                m = jnp.max(s, axis=-1, keepdims=True)
                e = jnp.exp(s - m)
                l = jnp.sum(e, axis=-1, keepdims=True)
                o = jnp.dot(e.astype(BF16), v2, preferred_element_type=F32) / l
                o_pair = o if hh == 0 else jnp.where(first_head, o_pair, o)
            outs.append(o_pair)
        a = jnp.concatenate(outs, axis=1)
        o_ref[0, pl.ds(pl.multiple_of(p * PAIR_TOKENS, PAIR_TOKENS), PAIR_TOKENS), :] = _rms(a, g_ref[...]).astype(BF16)
        return carry

    lax.fori_loop(0, pairs_per_step, pair_body, 0)


def _attention(q3, kt4, v3, bias, g_na):
    bsz, L, _ = q3.shape
    n_pairs = L // PAIR_TOKENS
    return pl.pallas_call(
        functools.partial(_attn_kernel, n_pairs=n_pairs),
        grid=(bsz, L // ATT_TOKENS),
        in_specs=[pl.BlockSpec((1, ATT_TOKENS, NA_WIDTH), lambda b, s: (b, s, 0)),
                  pl.BlockSpec((1, n_pairs, NA_WIDTH, PAIR_TOKENS), lambda b, s: (b, 0, 0, 0)),
                  pl.BlockSpec((1, L, NA_WIDTH), lambda b, s: (b, 0, 0)),
                  pl.BlockSpec((N_BIAS_VARIANTS, NA_HEADS, PAIR_TOKENS, SLAB_TOKENS), lambda b, s: (0, 0, 0, 0)),
                  pl.BlockSpec((1, NA_WIDTH), lambda b, s: (0, 0))],
        out_specs=pl.BlockSpec((1, ATT_TOKENS, NA_WIDTH), lambda b, s: (b, s, 0)),
        out_shape=jax.ShapeDtypeStruct((bsz, L, NA_WIDTH), BF16),
        compiler_params=_cparams(("arbitrary", "arbitrary")),
    )(q3, kt4, v3, bias, g_na)


def _angles(j, k, n):
    return ((j * k) % n).astype(F32) * (2.0 * np.pi / n)


def _row_stage_table(R):
    k = jnp.arange(R, dtype=jnp.int32)
    ang = _angles(k[:, None], k[None, :], R)
    return jnp.concatenate([jnp.cos(ang), -jnp.sin(ang)], axis=0).astype(BF16)


def _col_stage_table(L):
    R = L // GRID_W
    k1 = jnp.arange(R, dtype=jnp.int32)[:, None, None]
    k2 = jnp.arange(GRID_W, dtype=jnp.int32)[None, :, None]
    c = jnp.arange(GRID_W, dtype=jnp.int32)[None, None, :]
    ang = _angles(c, k1 + R * k2, L)
    cc, ss = jnp.cos(ang), jnp.sin(ang)
    top = jnp.concatenate([cc, ss], axis=2)
    bot = jnp.concatenate([-ss, cc], axis=2)
    return jnp.concatenate([top, bot], axis=1).astype(BF16)


def _channel_table():
    j = jnp.arange(FN_GROUP_DIM, dtype=jnp.int32)
    ang = _angles(j[:, None], j[None, :], FN_GROUP_DIM)
    return jnp.concatenate([jnp.cos(ang), jnp.sin(ang)], axis=0).astype(BF16)


def _dft_rows_kernel(f_ref, x_ref, y_ref):
    y_ref[0] = jnp.dot(f_ref[...], x_ref[0], preferred_element_type=F32).astype(BF16)


def _dft_rows(f_rows, u3):
    bsz, R, width = u3.shape
    return pl.pallas_call(
        _dft_rows_kernel,
        grid=(bsz, width // DFT_LANES),
        in_specs=[pl.BlockSpec((2 * R, R), lambda b, t: (0, 0)),
                  pl.BlockSpec((1, R, DFT_LANES), lambda b, t: (b, 0, t))],
        out_specs=pl.BlockSpec((1, 2 * R, DFT_LANES), lambda b, t: (b, 0, t)),
        out_shape=jax.ShapeDtypeStruct((bsz, 2 * R, width), BF16),
        compiler_params=_cparams(("arbitrary", "arbitrary")),
    )(f_rows, u3)


def _dft_cols_kernel(yr_ref, yi_ref, m_ref, cs_ref, wf_ref, g_ref, o_ref):
    re, im = [], []
    for j in range(DFT_K1):
        y = jnp.concatenate([yr_ref[0, j], yi_ref[0, j]], axis=0)
        x = jnp.dot(m_ref[j], y, preferred_element_type=F32)
        re.append(x[:GRID_W])
        im.append(x[GRID_W:])
    a = jnp.concatenate(re, axis=0).astype(BF16)
    b = jnp.concatenate(im, axis=0).astype(BF16)
    ys = []
    for g in range(FN_GROUPS):
        sl = slice(g * FN_GROUP_DIM, (g + 1) * FN_GROUP_DIM)
        ab = jnp.concatenate([a[:, sl], b[:, sl]], axis=1)
        z = jnp.dot(ab, cs_ref[...], preferred_element_type=F32).astype(BF16)
        ys.append(jnp.dot(z, wf_ref[g], preferred_element_type=F32))
    out = _rms(jnp.concatenate(ys, axis=1), g_ref[...]).astype(BF16)
    for j in range(DFT_K1):
        o_ref[0, :, j * FN_WIDTH:(j + 1) * FN_WIDTH] = out[j * GRID_W:(j + 1) * GRID_W, :]


def _dft_cols(y4, m_cols, cs, wf, g_fn):
    bsz, R2, _, _ = y4.shape
    R = R2 // 2
    steps = R // DFT_K1
    return pl.pallas_call(
        _dft_cols_kernel,
        grid=(bsz, steps),
        in_specs=[pl.BlockSpec((1, DFT_K1, GRID_W, FN_WIDTH), lambda b, t: (b, t, 0, 0)),
                  pl.BlockSpec((1, DFT_K1, GRID_W, FN_WIDTH), lambda b, t: (b, steps + t, 0, 0)),
                  pl.BlockSpec((DFT_K1, 2 * GRID_W, 2 * GRID_W), lambda b, t: (t, 0, 0)),
                  pl.BlockSpec((2 * FN_GROUP_DIM, FN_GROUP_DIM), lambda b, t: (0, 0)),
                  pl.BlockSpec((FN_GROUPS, FN_GROUP_DIM, FN_GROUP_DIM), lambda b, t: (0, 0, 0)),
                  pl.BlockSpec((1, FN_WIDTH), lambda b, t: (0, 0))],
        out_specs=pl.BlockSpec((1, GRID_W, DFT_K1 * FN_WIDTH), lambda b, t: (b, 0, t)),
        out_shape=jax.ShapeDtypeStruct((bsz, GRID_W, R * FN_WIDTH), BF16),
        compiler_params=_cparams(("arbitrary", "arbitrary")),
    )(y4, y4, m_cols, cs, wf, g_fn)


def _fourier(u, bsz, L, cs, wf, g_fn):
    R = L // GRID_W
    y = _dft_rows(_row_stage_table(R), u.reshape(bsz, R, GRID_W * FN_WIDTH))
    fn = _dft_cols(y.reshape(bsz, 2 * R, GRID_W, FN_WIDTH), _col_stage_table(L), cs, wf, g_fn)
    return fn.reshape(bsz * L, FN_WIDTH)


def _outproj_kernel(na_ref, fn_ref, x_ref, gt1_ref, sh2_ref, sc2_ref, gpost_ref, gpre_ref,
                    woa_ref, wof_ref, wr_ref, br_ref,
                    x1_ref, h2_ref, gate_ref, pos_ref, cnt_ref):
    mix = (jnp.dot(na_ref[...], woa_ref[...], preferred_element_type=F32)
           + jnp.dot(fn_ref[...], wof_ref[...], preferred_element_type=F32))
    x1 = x_ref[...] + gt1_ref[0] * _rms(mix, gpost_ref[...])
    x1_ref[...] = x1
    h2 = _rms(x1, gpre_ref[...]) * (1.0 + sc2_ref[0]) + sh2_ref[0]
    hi = h2.astype(BF16)
    h2_ref[...] = hi

    lo = (h2 - hi.astype(F32)).astype(BF16)
    nt = (((1,), (1,)), ((), ()))
    a = lax.dot_general(wr_ref[...], hi, nt, preferred_element_type=F32)
    b = lax.dot_general(wr_ref[0:N_EXPERTS, :], lo, nt, preferred_element_type=F32)
    logits = a[:N_EXPERTS] + a[N_EXPERTS:] + b + br_ref[:, 0:1]

    eidx = lax.broadcasted_iota(jnp.int32, (N_EXPERTS, TM), 0)
    vals, onehots = [], []
    cur = logits
    for k in range(TOP_K):
        m = jnp.max(cur, axis=0, keepdims=True)
        idx = jnp.min(jnp.where(cur == m, eidx, N_EXPERTS), axis=0, keepdims=True)
        hit = eidx == idx
        vals.append(m)
        onehots.append(hit)
        cur = jnp.where(hit, -jnp.inf, cur)
    exps = [jnp.exp(v - vals[0]) for v in vals]
    den = exps[0] + exps[1] + exps[2] + exps[3]
    for k in range(TOP_K):
        gate_ref[k:k + 1, :] = exps[k] / den

    oh = [h.astype(F32) for h in onehots]
    oh_sum = oh[0] + oh[1] + oh[2] + oh[3]
    s_i = lax.broadcasted_iota(jnp.int32, (TM, TM), 0)
    t_i = lax.broadcasted_iota(jnp.int32, (TM, TM), 1)
    earlier = jnp.where(s_i < t_i, 1.0, 0.0).astype(BF16)
    same_expert_before = jnp.dot(oh_sum.astype(BF16), earlier, preferred_element_type=F32)
    counts = jnp.broadcast_to(jnp.sum(oh_sum, axis=1, keepdims=True), (N_EXPERTS, LANES))
    cnt_ref[...] = counts
    run_tiles = jnp.floor((counts + (SUBLANES - 1)) * (1.0 / SUBLANES))
    e_r = lax.broadcasted_iota(jnp.int32, (N_EXPERTS, N_EXPERTS), 0)
    e_c = lax.broadcasted_iota(jnp.int32, (N_EXPERTS, N_EXPERTS), 1)
    lower = jnp.where(e_c < e_r, 1.0, 0.0).astype(BF16)
    run_start = SUBLANES * jnp.dot(lower, run_tiles.astype(BF16), preferred_element_type=F32)[:, 0:1]
    base = run_start + same_expert_before
    for k in range(TOP_K):
        pos_ref[k:k + 1, :] = jnp.sum(oh[k] * base, axis=0, keepdims=True).astype(jnp.int32)


def _outproj(na2d, fn2d, x2d, ada3, ada_row0, bsz, L, g_post, g_pre, woa, wof, wr2, br):
    T = bsz * L
    tpb = L // TM
    row = lambda j: (lambda i: ((ada_row0 + i // tpb) * 6 + j, 0, 0))
    const2 = lambda i: (0, 0)
    tok = lambda i: (i, 0)
    lanes = lambda i: (0, i)
    return pl.pallas_call(
        _outproj_kernel,
        grid=(T // TM,),
        in_specs=[pl.BlockSpec((TM, NA_WIDTH), tok),
                  pl.BlockSpec((TM, FN_WIDTH), tok),
                  pl.BlockSpec((TM, D_MODEL), tok),
                  pl.BlockSpec((1, 1, D_MODEL), row(2)),
                  pl.BlockSpec((1, 1, D_MODEL), row(3)),
                  pl.BlockSpec((1, 1, D_MODEL), row(4)),
                  pl.BlockSpec((1, D_MODEL), const2),
                  pl.BlockSpec((1, D_MODEL), const2),
                  pl.BlockSpec((NA_WIDTH, D_MODEL), const2),
                  pl.BlockSpec((FN_WIDTH, D_MODEL), const2),
                  pl.BlockSpec((2 * N_EXPERTS, D_MODEL), const2),
                  pl.BlockSpec((N_EXPERTS, LANES), const2)],
        out_specs=[pl.BlockSpec((TM, D_MODEL), tok),
                   pl.BlockSpec((TM, D_MODEL), tok),
                   pl.BlockSpec((TOP_K, TM), lanes),
                   pl.BlockSpec((TOP_K, TM), lanes),
                   pl.BlockSpec((N_EXPERTS, LANES), tok)],
        out_shape=[jax.ShapeDtypeStruct((T, D_MODEL), F32),
                   jax.ShapeDtypeStruct((T, D_MODEL), BF16),
                   jax.ShapeDtypeStruct((TOP_K, T), F32),
                   jax.ShapeDtypeStruct((TOP_K, T), jnp.int32),
                   jax.ShapeDtypeStruct((T // TM * N_EXPERTS, LANES), F32)],
        compiler_params=_cparams(("arbitrary",)),
    )(na2d, fn2d, x2d, ada3, ada3, ada3, g_post, g_pre, woa, wof, wr2, br)


def _zero_fill(pad_lo_ref, pad_n_ref, tail_ref, xs_hbm, zbuf, zsem, n_blocks, wait):
    def run(cp):
        cp.wait() if wait else cp.start()

    def expert(e, c):
        pos = pad_lo_ref[e]
        n = pad_n_ref[e]
        for size in _pow2_sizes(ZERO_ROWS):
            @pl.when((n & size) != 0)
            def _(pos=pos, size=size):
                dst = xs_hbm.at[pl.ds(pl.multiple_of(pos, SUBLANES), size)]
                run(pltpu.make_async_copy(zbuf.at[pl.ds(0, size)], dst, zsem))
            pos = pos + (n & size)
        return c

    lax.fori_loop(0, N_EXPERTS, expert, 0)

    def tail(blk, c):
        for half in range(MOE_BLOCK // ZERO_ROWS):
            row0 = pl.multiple_of(blk * MOE_BLOCK + half * ZERO_ROWS, ZERO_ROWS)
            run(pltpu.make_async_copy(zbuf, xs_hbm.at[pl.ds(row0, ZERO_ROWS)], zsem))
        return c

    lax.fori_loop(tail_ref[0], n_blocks, tail, 0)


def _tile_pieces(cnt_ref, off_ref, dst_ref, tile, local_buf, sorted_hbm, sem, to_sorted, wait):
    def expert(e, c):
        j = tile * N_EXPERTS + e
        n = cnt_ref[j]
        src = off_ref[j]
        dst = dst_ref[j]
        for size in _pow2_sizes(TM):
            @pl.when((n & size) != 0)
            def _(src=src, dst=dst, size=size):
                loc = local_buf.at[pl.ds(pl.multiple_of(src, SUBLANES), size)]
                glob = sorted_hbm.at[pl.ds(pl.multiple_of(dst, SUBLANES), size)]
                cp = pltpu.make_async_copy(loc, glob, sem) if to_sorted else pltpu.make_async_copy(glob, loc, sem)
                cp.wait() if wait else cp.start()
            src = src + (n & size)
            dst = dst + (n & size)
        return c

    lax.fori_loop(0, N_EXPERTS, expert, 0)


def _dispatch_kernel(pad_lo_ref, pad_n_ref, tail_ref, cnt_ref, off_ref, dst_ref,
                     pos_ref, h2a_ref, h2b_ref, xs_hbm, sbuf, zbuf, sem, zsem, *, n_blocks, tiles_a, n_tiles):
    i = pl.program_id(0)
    buf = i % 2

    @pl.when(i == 0)
    def _():
        zbuf[...] = jnp.zeros_like(zbuf)
        _zero_fill(pad_lo_ref, pad_n_ref, tail_ref, xs_hbm, zbuf, zsem, n_blocks, wait=False)
        _zero_fill(pad_lo_ref, pad_n_ref, tail_ref, xs_hbm, zbuf, zsem, n_blocks, wait=True)

    @pl.when(i >= 2)
    def _():
        _tile_pieces(cnt_ref, off_ref, dst_ref, i - 2, sbuf.at[buf], xs_hbm, sem.at[buf], True, wait=True)

    def sort_tile(h2_ref):
        h = h2_ref[...]
        pos = pos_ref[...]
        for c in range(TILE_ROWS // SORT_CHUNK):
            row = lax.broadcasted_iota(jnp.int32, (SORT_CHUNK, TM), 0) + c * SORT_CHUNK
            sel = jnp.zeros((SORT_CHUNK, TM), F32)
            for k in range(TOP_K):
                sel = jnp.where(row == pos[k:k + 1, :], 1.0, sel)
            sbuf[buf, c * SORT_CHUNK:(c + 1) * SORT_CHUNK, :] = jnp.dot(sel.astype(BF16), h,
                                                                        preferred_element_type=F32)

    @pl.when(i < tiles_a)
    def _():
        sort_tile(h2a_ref)

    @pl.when(i >= tiles_a)
    def _():
        sort_tile(h2b_ref)

    _tile_pieces(cnt_ref, off_ref, dst_ref, i, sbuf.at[buf], xs_hbm, sem.at[buf], True, wait=False)

    @pl.when(i == n_tiles - 1)
    def _():
        if n_tiles >= 2:
            _tile_pieces(cnt_ref, off_ref, dst_ref, i - 1, sbuf.at[1 - buf], xs_hbm, sem.at[1 - buf], True, wait=True)
        _tile_pieces(cnt_ref, off_ref, dst_ref, i, sbuf.at[buf], xs_hbm, sem.at[buf], True, wait=True)


def _dispatch(pad_lo, pad_n, tail, cnt, off, dst, pos, h2a, h2b, n_blocks):
    tiles_a = h2a.shape[0] // TM
    tiles_b = h2b.shape[0] // TM
    n_tiles = tiles_a + tiles_b
    P = n_blocks * MOE_BLOCK
    kern = functools.partial(_dispatch_kernel, n_blocks=n_blocks, tiles_a=tiles_a, n_tiles=n_tiles)
    return pl.pallas_call(
        kern,
        grid_spec=pltpu.PrefetchScalarGridSpec(
            num_scalar_prefetch=6,
            grid=(n_tiles,),
            in_specs=[pl.BlockSpec((TOP_K, TM), lambda i, *_: (0, i)),
                      pl.BlockSpec((TM, D_MODEL), lambda i, *_: (jnp.minimum(i, tiles_a - 1), 0)),
                      pl.BlockSpec((TM, D_MODEL), lambda i, *_: (jnp.maximum(i - tiles_a, 0), 0))],
            out_specs=pl.BlockSpec(memory_space=pl.ANY),
            scratch_shapes=[pltpu.VMEM((2, TILE_ROWS, D_MODEL), F32),
                            pltpu.VMEM((ZERO_ROWS, D_MODEL), F32),
                            pltpu.SemaphoreType.DMA((2,)),
                            pltpu.SemaphoreType.DMA]),
        out_shape=jax.ShapeDtypeStruct((P, D_MODEL), F32),
        compiler_params=_cparams(("arbitrary",)),
    )(pad_lo, pad_n, tail, cnt, off, dst, pos, h2a, h2b)


def _expert_kernel(be_ref, used_ref, x_ref, wgu_ref, bgu_ref, wd_ref, bd_ref, y_ref):
    del be_ref
    live = pl.program_id(0) < used_ref[0]

    @pl.when(live)
    def _():
        for part in range(MOE_BLOCK // EXPERT_ROWS):
            rows = slice(part * EXPERT_ROWS, (part + 1) * EXPERT_ROWS)
            x = x_ref[rows, :].astype(BF16)
            gu = jnp.dot(x, wgu_ref[0], preferred_element_type=F32) + bgu_ref[0]
            glu = jnp.minimum(gu[:, :D_FF], SWIGLU_LIMIT)
            lin = jnp.clip(gu[:, D_FF:], -SWIGLU_LIMIT, SWIGLU_LIMIT)
            act = glu * jax.nn.sigmoid(SWIGLU_ALPHA * glu) * (lin + 1.0)
            y_ref[rows, :] = jnp.dot(act.astype(BF16), wd_ref[0], preferred_element_type=F32) + bd_ref[0]

    @pl.when(jnp.logical_not(live))
    def _():
        y_ref[...] = jnp.zeros_like(y_ref)


def _experts(block_e, n_used, xs, wgu, bgu, wd, bd):
    n_blocks = block_e.shape[0]
    x_map = lambda i, be, used: (jnp.minimum(i, used[0] - 1), 0)
    return pl.pallas_call(
        _expert_kernel,
        grid_spec=pltpu.PrefetchScalarGridSpec(
            num_scalar_prefetch=2,
            grid=(n_blocks,),
            in_specs=[pl.BlockSpec((MOE_BLOCK, D_MODEL), x_map),
                      pl.BlockSpec((1, D_MODEL, 2 * D_FF), lambda i, be, used: (be[i], 0, 0)),
                      pl.BlockSpec((1, 1, 2 * D_FF), lambda i, be, used: (be[i], 0, 0)),
                      pl.BlockSpec((1, D_FF, D_MODEL), lambda i, be, used: (be[i], 0, 0)),
                      pl.BlockSpec((1, 1, D_MODEL), lambda i, be, used: (be[i], 0, 0))],
            out_specs=pl.BlockSpec((MOE_BLOCK, D_MODEL), lambda i, be, used: (i, 0))),
        out_shape=jax.ShapeDtypeStruct(xs.shape, F32),
        compiler_params=_cparams(("arbitrary",)),
    )(block_e, n_used, xs, wgu, bgu, wd, bd)


def _combine_kernel(cnt_ref, off_ref, dst_ref, ys_hbm, pos_ref, gate_ref, x1_ref, gt2_ref, g_ref, o_ref,
                    ybuf, sem, *, tile0, n_tiles):
    i = pl.program_id(0)
    buf = i % 2
    fetch = functools.partial(_tile_pieces, cnt_ref, off_ref, dst_ref)

    @pl.when(i == 0)
    def _():
        fetch(tile0, ybuf.at[0], ys_hbm, sem.at[0], False, wait=False)

    @pl.when(i + 1 < n_tiles)
    def _():
        fetch(tile0 + i + 1, ybuf.at[1 - buf], ys_hbm, sem.at[1 - buf], False, wait=False)

    fetch(tile0 + i, ybuf.at[buf], ys_hbm, sem.at[buf], False, wait=True)

    pos = pos_ref[...]
    gate = gate_ref[...]
    last = (tile0 + i) * N_EXPERTS + N_EXPERTS - 1
    n_local = off_ref[last] + cnt_ref[last]
    moe = jnp.zeros((TM, D_MODEL), F32)
    for c in range(TILE_ROWS // SORT_CHUNK):
        col = lax.broadcasted_iota(jnp.int32, (TM, SORT_CHUNK), 1) + c * SORT_CHUNK
        w = jnp.zeros((TM, SORT_CHUNK), F32)
        for k in range(TOP_K):
            w = jnp.where(col == pos[:, k:k + 1], gate[:, k:k + 1], w)
        rows = ybuf[buf, c * SORT_CHUNK:(c + 1) * SORT_CHUNK, :]
        if (c + 1) * SORT_CHUNK > TM * TOP_K:
            r = lax.broadcasted_iota(jnp.int32, (SORT_CHUNK, 1), 0) + c * SORT_CHUNK
            rows = jnp.where(r < n_local, rows, 0.0)
        moe = moe + jnp.dot(w.astype(BF16), rows.astype(BF16), preferred_element_type=F32)
    o_ref[...] = x1_ref[...] + gt2_ref[0] * _rms(moe, g_ref[...])


def _combine(cnt, off, dst, ys, tile0, pos_tk, gates_tk, x1, ada3, ada_row0, bsz, L, g_post):
    T = bsz * L
    tpb = L // TM
    n_tiles = T // TM
    return pl.pallas_call(
        functools.partial(_combine_kernel, tile0=tile0, n_tiles=n_tiles),
        grid_spec=pltpu.PrefetchScalarGridSpec(
            num_scalar_prefetch=3,
            grid=(n_tiles,),
            in_specs=[pl.BlockSpec(memory_space=pl.ANY),
                      pl.BlockSpec((TM, TOP_K), lambda i, *_: (i, 0)),
                      pl.BlockSpec((TM, TOP_K), lambda i, *_: (i, 0)),
                      pl.BlockSpec((TM, D_MODEL), lambda i, *_: (i, 0)),
                      pl.BlockSpec((1, 1, D_MODEL), lambda i, *_: ((ada_row0 + i // tpb) * 6 + 5, 0, 0)),
                      pl.BlockSpec((1, D_MODEL), lambda i, *_: (0, 0))],
            out_specs=pl.BlockSpec((TM, D_MODEL), lambda i, *_: (i, 0)),
            scratch_shapes=[pltpu.VMEM((2, TILE_ROWS, D_MODEL), F32),
                            pltpu.SemaphoreType.DMA((2,))]),
        out_shape=jax.ShapeDtypeStruct((T, D_MODEL), F32),
        compiler_params=_cparams(("arbitrary",)),
    )(cnt, off, dst, ys, pos_tk, gates_tk, x1, ada3, g_post)


def kernel(x_prompt, x_sample, c_prompt, c_sample, w_ada, b_ada, g_pre_mix, g_post_mix, g_pre_ffn, g_post_ffn,
           w_in, rpb, w_fourier, g_na, g_fn, w_out, w_router, b_router, w_gate_up, b_gate_up, w_down, b_down):
    assert w_ada.shape[0] == 1, "single layer"
    groups = [(x_prompt, c_prompt), (x_sample, c_sample)]
    for x, _ in groups:
        assert x.shape[1] % ATT_TOKENS == 0 and x.shape[1] // GRID_W >= 12 and x.shape[2] == D_MODEL

    row2 = lambda a: a[0].reshape(1, -1)
    w_in0 = w_in[0]
    wq = (w_in0[:, :NA_WIDTH] * HEAD_DIM ** -0.5).astype(BF16)
    wkt = w_in0[:, NA_WIDTH:2 * NA_WIDTH].T.astype(BF16)
    wv = w_in0[:, 2 * NA_WIDTH:3 * NA_WIDTH].astype(BF16)
    wu = w_in0[:, 3 * NA_WIDTH:].astype(BF16)
    cs = _channel_table()
    bias = _bias_table(rpb[0])
    wf = w_fourier[0].astype(BF16)
    woa = w_out[0, :NA_WIDTH].astype(BF16)
    wof = w_out[0, NA_WIDTH:].astype(BF16)
    wr_t = w_router[0].T
    wr_hi = wr_t.astype(BF16)
    wr_lo = (wr_t - wr_hi.astype(F32)).astype(BF16)
    wr2 = jnp.concatenate([wr_hi, wr_lo], axis=0)
    br = jnp.broadcast_to(b_router[0][:, None], (N_EXPERTS, LANES))
    wgu = w_gate_up[0].astype(BF16)
    wd = w_down[0].astype(BF16)
    bgu = b_gate_up[0].reshape(N_EXPERTS, 1, 2 * D_FF)
    bd = b_down[0].reshape(N_EXPERTS, 1, D_MODEL)

    c_all = jnp.concatenate([c_prompt, c_sample], axis=0)
    ada3 = _ada(c_all, w_ada[0], b_ada[0]).reshape(c_all.shape[0] * 6, 1, D_MODEL)

    per_group = []
    ada_row0 = 0
    for x, _ in groups:
        bsz, L, _ = x.shape
        x2d = x.reshape(bsz * L, D_MODEL)
        q, kt4, v, u = _inproj(x2d, ada3, ada_row0, bsz, L, row2(g_pre_mix), wq, wkt, wv, wu)
        na = _attention(q.reshape(bsz, L, NA_WIDTH), kt4, v.reshape(bsz, L, NA_WIDTH), bias, row2(g_na))
        fn = _fourier(u, bsz, L, cs, wf, row2(g_fn))
        x1, h2, gate_t, pos_t, cnt = _outproj(
            na.reshape(bsz * L, NA_WIDTH), fn, x2d, ada3, ada_row0, bsz, L,
            row2(g_post_mix), row2(g_pre_ffn), woa, wof, wr2, br)
        per_group.append((x1, h2, gate_t, pos_t, cnt, bsz, L, ada_row0))
        ada_row0 += bsz

    T = sum(g[5] * g[6] for g in per_group)
    max_rows = T * TOP_K + (T // TM) * N_EXPERTS * (SUBLANES - 1) + N_EXPERTS * (MOE_BLOCK - 1)
    n_blocks = -(-max_rows // MOE_BLOCK)
    tile_cnt = jnp.concatenate([g[4][:, 0].reshape(-1, N_EXPERTS) for g in per_group], axis=0).astype(jnp.int32)
    tile_cnt = _round_up(tile_cnt, SUBLANES)
    total = jnp.sum(tile_cnt, axis=0)
    padded = ((total + MOE_BLOCK - 1) // MOE_BLOCK) * MOE_BLOCK
    pend = jnp.cumsum(padded)
    pstart = pend - padded
    run = jnp.cumsum(tile_cnt, axis=0) - tile_cnt
    tile_dst = (pstart[None, :] + run).reshape(-1)
    tile_off = (jnp.cumsum(tile_cnt, axis=1) - tile_cnt).reshape(-1)
    tile_cnt = tile_cnt.reshape(-1)
    block_row0 = jnp.arange(n_blocks, dtype=jnp.int32) * MOE_BLOCK
    block_e = jnp.minimum(jnp.sum((pend[None, :] <= block_row0[:, None]).astype(jnp.int32), axis=1), N_EXPERTS - 1)
    pad_lo = pstart + total
    pad_n = padded - total
    tail = (pend[-1:] // MOE_BLOCK).astype(jnp.int32)

    pos_all = jnp.concatenate([g[3] for g in per_group], axis=1)
    xs = _dispatch(pad_lo, pad_n, tail, tile_cnt, tile_off, tile_dst, pos_all,
                   per_group[0][1], per_group[1][1], n_blocks)
    ys = _experts(block_e, tail, xs, wgu, bgu, wd, bd)

    outs = []
    tile0 = 0
    for x1, _, gate_t, pos_t, _, bsz, L, row0 in per_group:
        out = _combine(tile_cnt, tile_off, tile_dst, ys, tile0, pos_t.T, gate_t.T, x1, ada3, row0, bsz, L,
                       row2(g_post_ffn))
        outs.append(out.reshape(bsz, L, D_MODEL))
        tile0 += bsz * L // TM
    return tuple(outs)
```

```python
import functools

import numpy as np
import jax
import jax.numpy as jnp
from jax import lax
from jax.experimental import pallas as pl
from jax.experimental.pallas import tpu as pltpu

F32 = jnp.float32
BF16 = jnp.bfloat16

D_MODEL = 1024
GRID_W = 64
NA_WIDTH = 512
HEAD_DIM = 64
NA_HEADS = 8
FN_WIDTH = 512
FN_GROUPS = 4
FN_GROUP_DIM = 128
WIN_ROWS = 8
WIN_COLS = 16
N_EXPERTS = 32
TOP_K = 4
D_FF = 1024
SWIGLU_ALPHA = 1.702
SWIGLU_LIMIT = 7.0
MOE_BLOCK = 512
EXPERT_ROWS = 256
EPS = 1e-6

LANES = 128
PAIR_TOKENS = 2 * GRID_W
SLAB_ROWS = 10
SLAB_TOKENS = SLAB_ROWS * GRID_W
N_BIAS_VARIANTS = 5
MASK_VALUE = -1e30
LOG2_E = float(np.log2(np.e))

TM = 512
SUBLANES = 8
SORT_CHUNK = 256
TILE_ROWS = -(-(TM * TOP_K + N_EXPERTS * (SUBLANES - 1)) // SORT_CHUNK) * SORT_CHUNK
FILL_ROWS = TILE_ROWS - TM * TOP_K
ATT_TOKENS = 1024
DFT_LANES = 4096
DFT_K1 = 8
ZERO_ROWS = MOE_BLOCK // 2
VMEM_LIMIT = 56 * 1024 * 1024


def _cparams(sem):
    return pltpu.CompilerParams(dimension_semantics=sem, vmem_limit_bytes=VMEM_LIMIT)


def _rms(x, g):
    return x * lax.rsqrt(jnp.mean(x * x, axis=-1, keepdims=True) + EPS) * g


def _pow2_sizes(limit):
    sizes = []
    while limit >= SUBLANES:
        sizes.append(limit)
        limit //= 2
    return sizes


def _round_up(n, m):
    return (n + m - 1) // m * m


def _ada_kernel(c_ref, w_ref, b_ref, o_ref):
    c = c_ref[...]
    s = c * jax.nn.sigmoid(c)
    o_ref[...] = jnp.dot(s.astype(BF16), w_ref[...].astype(BF16), preferred_element_type=F32) + b_ref[...]


def _ada(c_all, w_ada, b_ada):
    nb = c_all.shape[0]
    n_out = w_ada.shape[1]
    tn = 1536
    return pl.pallas_call(
        _ada_kernel,
        grid=(n_out // tn,),
        in_specs=[pl.BlockSpec((nb, D_MODEL), lambda j: (0, 0)),
                  pl.BlockSpec((D_MODEL, tn), lambda j: (0, j)),
                  pl.BlockSpec((1, tn), lambda j: (0, j))],
        out_specs=pl.BlockSpec((nb, tn), lambda j: (0, j)),
        out_shape=jax.ShapeDtypeStruct((nb, n_out), F32),
        compiler_params=_cparams(("arbitrary",)),
    )(c_all, w_ada, b_ada.reshape(1, n_out))


def _inproj_kernel(x_ref, sh_ref, sc_ref, g_ref, wq_ref, wkt_ref, wv_ref, wu_ref,
                   q_ref, kt_ref, v_ref, u_ref):
    h = _rms(x_ref[...], g_ref[...]) * (1.0 + sc_ref[0]) + sh_ref[0]
    hb = h.astype(BF16)
    q_ref[...] = jnp.dot(hb, wq_ref[...], preferred_element_type=F32).astype(BF16)
    v_ref[...] = jnp.dot(hb, wv_ref[...], preferred_element_type=F32).astype(BF16)
    kt = lax.dot_general(wkt_ref[...], hb, (((1,), (1,)), ((), ())), preferred_element_type=F32).astype(BF16)
    for c in range(TM // PAIR_TOKENS):
        kt_ref[0, c] = kt[:, c * PAIR_TOKENS:(c + 1) * PAIR_TOKENS]
    u_ref[...] = jnp.dot(hb, wu_ref[...], preferred_element_type=F32).astype(BF16)


def _inproj(x2d, ada3, ada_row0, bsz, L, g_pre, wq, wkt, wv, wu):
    T = bsz * L
    tpb = L // TM
    row = lambda j: (lambda i: ((ada_row0 + i // tpb) * 6 + j, 0, 0))
    const2 = lambda i: (0, 0)
    return pl.pallas_call(
        _inproj_kernel,
        grid=(T // TM,),
        in_specs=[pl.BlockSpec((TM, D_MODEL), lambda i: (i, 0)),
                  pl.BlockSpec((1, 1, D_MODEL), row(0)),
                  pl.BlockSpec((1, 1, D_MODEL), row(1)),
                  pl.BlockSpec((1, D_MODEL), const2),
                  pl.BlockSpec((D_MODEL, NA_WIDTH), const2),
                  pl.BlockSpec((NA_WIDTH, D_MODEL), const2),
                  pl.BlockSpec((D_MODEL, NA_WIDTH), const2),
                  pl.BlockSpec((D_MODEL, FN_WIDTH), const2)],
        out_specs=[pl.BlockSpec((TM, NA_WIDTH), lambda i: (i, 0)),
                   pl.BlockSpec((1, TM // PAIR_TOKENS, NA_WIDTH, PAIR_TOKENS),
                                lambda i: (i // tpb, i % tpb, 0, 0)),
                   pl.BlockSpec((TM, NA_WIDTH), lambda i: (i, 0)),
                   pl.BlockSpec((TM, FN_WIDTH), lambda i: (i, 0))],
        out_shape=[jax.ShapeDtypeStruct((T, NA_WIDTH), BF16),
                   jax.ShapeDtypeStruct((bsz, L // PAIR_TOKENS, NA_WIDTH, PAIR_TOKENS), BF16),
                   jax.ShapeDtypeStruct((T, NA_WIDTH), BF16),
                   jax.ShapeDtypeStruct((T, FN_WIDTH), BF16)],
        compiler_params=_cparams(("arbitrary",)),
    )(x2d, ada3, ada3, g_pre, wq, wkt, wv, wu)


def _bias_table(rpb):
    rows = 32
    n_pairs = rows // 2
    c = np.arange(GRID_W)
    kc0 = np.clip(c - WIN_COLS // 2, 0, GRID_W - WIN_COLS)
    col_ok = (c[None, :] >= kc0[:, None]) & (c[None, :] < kc0[:, None] + WIN_COLS)
    ci = c[None, :] - c[:, None] + WIN_COLS - 1
    pick = ((ci[None] == np.arange(2 * WIN_COLS - 1)[:, None, None]) & col_ok[None]).astype(np.float32)
    toe = jnp.einsum('hrk,kcd->hrcd', rpb, jnp.asarray(pick), precision=lax.Precision.HIGHEST)
    toe = jnp.where(col_ok, toe * LOG2_E, MASK_VALUE)
    masked = jnp.full((NA_HEADS, GRID_W, GRID_W), MASK_VALUE, F32)
    variants = []
    for j in (0, 1, 2, n_pairs - 2, n_pairs - 1):
        start = int(np.clip(2 * j - 4, 0, rows - SLAB_ROWS))
        halves = []
        for rl in range(2):
            r = 2 * j + rl
            kr0 = int(np.clip(r - WIN_ROWS // 2, 0, rows - WIN_ROWS))
            blocks = []
            for i in range(SLAB_ROWS):
                kr = start + i
                blocks.append(toe[:, kr - r + WIN_ROWS - 1] if kr0 <= kr < kr0 + WIN_ROWS else masked)
            halves.append(jnp.concatenate(blocks, axis=2))
        variants.append(jnp.concatenate(halves, axis=1))
    return jnp.stack(variants)


def _attn_kernel(q_ref, kt_ref, v_ref, bias_ref, g_ref, o_ref, *, n_pairs):
    step = pl.program_id(1)
    pairs_per_step = ATT_TOKENS // PAIR_TOKENS
    lane = lax.broadcasted_iota(jnp.int32, (PAIR_TOKENS, 2 * HEAD_DIM), 1)
    first_head = lane < HEAD_DIM

    def pair_body(p, carry):
        j = step * pairs_per_step + p
        start2 = jnp.clip(j - 2, 0, n_pairs - SLAB_ROWS // 2)
        variant = jnp.where(j < 2, j, jnp.where(j >= n_pairs - 2, j - (n_pairs - 2) + 3, 2))
        qp = q_ref[0, pl.ds(pl.multiple_of(p * PAIR_TOKENS, PAIR_TOKENS), PAIR_TOKENS), :]
        kt5 = kt_ref[0, pl.ds(start2, SLAB_ROWS // 2)]
        vs = v_ref[0, pl.ds(pl.multiple_of(start2 * PAIR_TOKENS, PAIR_TOKENS), SLAB_TOKENS), :]
        head_pairs = range(NA_HEADS // 2)
        dims = [slice(hp * 2 * HEAD_DIM, (hp + 1) * 2 * HEAD_DIM) for hp in head_pairs]
        scores, probs, sums, outs = {}, {}, {}, {}

        def score_phase(hp):
            q2 = qp[:, dims[hp]]
            kt2 = jnp.concatenate([kt5[c, dims[hp], :] for c in range(SLAB_ROWS // 2)], axis=1)
            zero = jnp.zeros_like(q2)
            qm = jnp.concatenate([jnp.where(first_head, q2, zero), jnp.where(first_head, zero, q2)], axis=0)
            scores[hp] = jnp.dot(qm, kt2, preferred_element_type=F32)

        def softmax_phase(hp):
            b2 = jnp.concatenate([bias_ref[variant, hp * 2], bias_ref[variant, hp * 2 + 1]], axis=0)
            s = scores[hp] + b2
            e = jnp.exp2(s - jnp.max(s, axis=-1, keepdims=True))
            sums[hp] = jnp.sum(e, axis=-1, keepdims=True)
            probs[hp] = e.astype(BF16)

        def value_phase(hp):
            o = jnp.dot(probs[hp], vs[:, dims[hp]], preferred_element_type=F32) / sums[hp]
            outs[hp] = jnp.where(first_head, o[:PAIR_TOKENS], o[PAIR_TOKENS:])

        n_hp = NA_HEADS // 2
        for t in range(n_hp + 2):
            if t < n_hp:
                score_phase(t)
            if 0 <= t - 1 < n_hp:
                softmax_phase(t - 1)
            if 0 <= t - 2 < n_hp:
                value_phase(t - 2)
        a = jnp.concatenate([outs[hp] for hp in head_pairs], axis=1)
        o_ref[0, pl.ds(pl.multiple_of(p * PAIR_TOKENS, PAIR_TOKENS), PAIR_TOKENS), :] = _rms(a, g_ref[...]).astype(BF16)
        return carry

    lax.fori_loop(0, pairs_per_step, pair_body, 0)


def _attention(q3, kt4, v3, bias, g_na):
    bsz, L, _ = q3.shape
    n_pairs = L // PAIR_TOKENS
    return pl.pallas_call(
        functools.partial(_attn_kernel, n_pairs=n_pairs),
        grid=(bsz, L // ATT_TOKENS),
        in_specs=[pl.BlockSpec((1, ATT_TOKENS, NA_WIDTH), lambda b, s: (b, s, 0)),
                  pl.BlockSpec((1, n_pairs, NA_WIDTH, PAIR_TOKENS), lambda b, s: (b, 0, 0, 0)),
                  pl.BlockSpec((1, L, NA_WIDTH), lambda b, s: (b, 0, 0)),
                  pl.BlockSpec((N_BIAS_VARIANTS, NA_HEADS, PAIR_TOKENS, SLAB_TOKENS), lambda b, s: (0, 0, 0, 0),
                               pipeline_mode=pl.Buffered(1)),
                  pl.BlockSpec((1, NA_WIDTH), lambda b, s: (0, 0))],
        out_specs=pl.BlockSpec((1, ATT_TOKENS, NA_WIDTH), lambda b, s: (b, s, 0)),
        out_shape=jax.ShapeDtypeStruct((bsz, L, NA_WIDTH), BF16),
        compiler_params=_cparams(("arbitrary", "arbitrary")),
    )(q3, kt4, v3, bias, g_na)


def _angles(j, k, n):
    return ((j * k) % n).astype(F32) * (2.0 * np.pi / n)


def _row_stage_table(R):
    k = jnp.arange(R, dtype=jnp.int32)
    ang = _angles(k[:, None], k[None, :], R)
    return jnp.concatenate([jnp.cos(ang), -jnp.sin(ang)], axis=0).astype(BF16)


def _col_stage_table(L):
    R = L // GRID_W
    k1 = jnp.arange(R, dtype=jnp.int32)[:, None, None]
    k2 = jnp.arange(GRID_W, dtype=jnp.int32)[None, :, None]
    c = jnp.arange(GRID_W, dtype=jnp.int32)[None, None, :]
    ang = _angles(c, k1 + R * k2, L)
    cc, ss = jnp.cos(ang), jnp.sin(ang)
    top = jnp.concatenate([cc, ss], axis=2)
    bot = jnp.concatenate([-ss, cc], axis=2)
    return jnp.concatenate([top, bot], axis=1).astype(BF16)


def _channel_table():
    j = jnp.arange(FN_GROUP_DIM, dtype=jnp.int32)
    ang = _angles(j[:, None], j[None, :], FN_GROUP_DIM)
    return jnp.concatenate([jnp.cos(ang), jnp.sin(ang)], axis=0).astype(BF16)


def _dft_rows_kernel(f_ref, x_ref, y_ref):
    y_ref[0] = jnp.dot(f_ref[...], x_ref[0], preferred_element_type=F32).astype(BF16)


def _dft_rows(f_rows, u3):
    bsz, R, width = u3.shape
    return pl.pallas_call(
        _dft_rows_kernel,
        grid=(bsz, width // DFT_LANES),
        in_specs=[pl.BlockSpec((2 * R, R), lambda b, t: (0, 0)),
                  pl.BlockSpec((1, R, DFT_LANES), lambda b, t: (b, 0, t))],
        out_specs=pl.BlockSpec((1, 2 * R, DFT_LANES), lambda b, t: (b, 0, t)),
        out_shape=jax.ShapeDtypeStruct((bsz, 2 * R, width), BF16),
        compiler_params=_cparams(("arbitrary", "arbitrary")),
    )(f_rows, u3)


def _dft_cols_kernel(yr_ref, yi_ref, m_ref, cs_ref, wf_ref, g_ref, o_ref):
    re, im = [], []
    for j in range(DFT_K1):
        y = jnp.concatenate([yr_ref[0, j], yi_ref[0, j]], axis=0)
        x = jnp.dot(m_ref[j], y, preferred_element_type=F32)
        re.append(x[:GRID_W])
        im.append(x[GRID_W:])
    a = jnp.concatenate(re, axis=0).astype(BF16)
    b = jnp.concatenate(im, axis=0).astype(BF16)
    ys = []
    for g in range(FN_GROUPS):
        sl = slice(g * FN_GROUP_DIM, (g + 1) * FN_GROUP_DIM)
        ab = jnp.concatenate([a[:, sl], b[:, sl]], axis=1)
        z = jnp.dot(ab, cs_ref[...], preferred_element_type=F32).astype(BF16)
        ys.append(jnp.dot(z, wf_ref[g], preferred_element_type=F32))
    out = _rms(jnp.concatenate(ys, axis=1), g_ref[...]).astype(BF16)
    for j in range(DFT_K1):
        o_ref[0, :, j * FN_WIDTH:(j + 1) * FN_WIDTH] = out[j * GRID_W:(j + 1) * GRID_W, :]


def _dft_cols(y4, m_cols, cs, wf, g_fn):
    bsz, R2, _, _ = y4.shape
    R = R2 // 2
    steps = R // DFT_K1
    return pl.pallas_call(
        _dft_cols_kernel,
        grid=(bsz, steps),
        in_specs=[pl.BlockSpec((1, DFT_K1, GRID_W, FN_WIDTH), lambda b, t: (b, t, 0, 0)),
                  pl.BlockSpec((1, DFT_K1, GRID_W, FN_WIDTH), lambda b, t: (b, steps + t, 0, 0)),
                  pl.BlockSpec((DFT_K1, 2 * GRID_W, 2 * GRID_W), lambda b, t: (t, 0, 0)),
                  pl.BlockSpec((2 * FN_GROUP_DIM, FN_GROUP_DIM), lambda b, t: (0, 0)),
                  pl.BlockSpec((FN_GROUPS, FN_GROUP_DIM, FN_GROUP_DIM), lambda b, t: (0, 0, 0)),
                  pl.BlockSpec((1, FN_WIDTH), lambda b, t: (0, 0))],
        out_specs=pl.BlockSpec((1, GRID_W, DFT_K1 * FN_WIDTH), lambda b, t: (b, 0, t)),
        out_shape=jax.ShapeDtypeStruct((bsz, GRID_W, R * FN_WIDTH), BF16),
        compiler_params=_cparams(("arbitrary", "arbitrary")),
    )(y4, y4, m_cols, cs, wf, g_fn)


def _fourier(u, bsz, L, cs, wf, g_fn):
    R = L // GRID_W
    y = _dft_rows(_row_stage_table(R), u.reshape(bsz, R, GRID_W * FN_WIDTH))
    fn = _dft_cols(y.reshape(bsz, 2 * R, GRID_W, FN_WIDTH), _col_stage_table(L), cs, wf, g_fn)
    return fn.reshape(bsz * L, FN_WIDTH)


def _outproj_kernel(na_ref, fn_ref, x_ref, gt1_ref, sh2_ref, sc2_ref, gpost_ref, gpre_ref,
                    woa_ref, wof_ref, wr_ref, br_ref,
                    x1_ref, h2_ref, gate_ref, pos_ref, cnt_ref):
    mix = (jnp.dot(na_ref[...], woa_ref[...], preferred_element_type=F32)
           + jnp.dot(fn_ref[...], wof_ref[...], preferred_element_type=F32))
    x1 = x_ref[...] + gt1_ref[0] * _rms(mix, gpost_ref[...])
    x1_ref[...] = x1
    h2 = _rms(x1, gpre_ref[...]) * (1.0 + sc2_ref[0]) + sh2_ref[0]
    hi = h2.astype(BF16)
    h2_ref[...] = hi

    lo = (h2 - hi.astype(F32)).astype(BF16)
    nt = (((1,), (1,)), ((), ()))
    a = lax.dot_general(wr_ref[...], hi, nt, preferred_element_type=F32)
    b = lax.dot_general(wr_ref[0:N_EXPERTS, :], lo, nt, preferred_element_type=F32)
    logits = a[:N_EXPERTS] + a[N_EXPERTS:] + b + br_ref[:, 0:1]

    eidx = lax.broadcasted_iota(jnp.int32, (N_EXPERTS, TM), 0)
    vals, onehots = [], []
    cur = logits
    for k in range(TOP_K):
        m = jnp.max(cur, axis=0, keepdims=True)
        idx = jnp.min(jnp.where(cur == m, eidx, N_EXPERTS), axis=0, keepdims=True)
        hit = eidx == idx
        vals.append(m)
        onehots.append(hit)
        cur = jnp.where(hit, -jnp.inf, cur)
    exps = [jnp.exp(v - vals[0]) for v in vals]
    den = exps[0] + exps[1] + exps[2] + exps[3]
    for k in range(TOP_K):
        gate_ref[k:k + 1, :] = exps[k] / den

    oh = [h.astype(F32) for h in onehots]
    oh_sum = oh[0] + oh[1] + oh[2] + oh[3]
    s_i = lax.broadcasted_iota(jnp.int32, (TM, TM), 0)
    t_i = lax.broadcasted_iota(jnp.int32, (TM, TM), 1)
    earlier = jnp.where(s_i < t_i, 1.0, 0.0).astype(BF16)
    same_expert_before = jnp.dot(oh_sum.astype(BF16), earlier, preferred_element_type=F32)
    counts = jnp.broadcast_to(jnp.sum(oh_sum, axis=1, keepdims=True), (N_EXPERTS, LANES))
    cnt_ref[...] = counts
    run_tiles = jnp.floor((counts + (SUBLANES - 1)) * (1.0 / SUBLANES))
    e_r = lax.broadcasted_iota(jnp.int32, (N_EXPERTS, N_EXPERTS), 0)
    e_c = lax.broadcasted_iota(jnp.int32, (N_EXPERTS, N_EXPERTS), 1)
    lower = jnp.where(e_c < e_r, 1.0, 0.0).astype(BF16)
    run_start = SUBLANES * jnp.dot(lower, run_tiles.astype(BF16), preferred_element_type=F32)[:, 0:1]
    base = run_start + same_expert_before
    for k in range(TOP_K):
        pos_ref[k:k + 1, :] = jnp.sum(oh[k] * base, axis=0, keepdims=True).astype(jnp.int32)


def _outproj(na2d, fn2d, x2d, ada3, ada_row0, bsz, L, g_post, g_pre, woa, wof, wr2, br):
    T = bsz * L
    tpb = L // TM
    row = lambda j: (lambda i: ((ada_row0 + i // tpb) * 6 + j, 0, 0))
    const2 = lambda i: (0, 0)
    tok = lambda i: (i, 0)
    lanes = lambda i: (0, i)
    return pl.pallas_call(
        _outproj_kernel,
        grid=(T // TM,),
        in_specs=[pl.BlockSpec((TM, NA_WIDTH), tok),
                  pl.BlockSpec((TM, FN_WIDTH), tok),
                  pl.BlockSpec((TM, D_MODEL), tok),
                  pl.BlockSpec((1, 1, D_MODEL), row(2)),
                  pl.BlockSpec((1, 1, D_MODEL), row(3)),
                  pl.BlockSpec((1, 1, D_MODEL), row(4)),
                  pl.BlockSpec((1, D_MODEL), const2),
                  pl.BlockSpec((1, D_MODEL), const2),
                  pl.BlockSpec((NA_WIDTH, D_MODEL), const2),
                  pl.BlockSpec((FN_WIDTH, D_MODEL), const2),
                  pl.BlockSpec((2 * N_EXPERTS, D_MODEL), const2),
                  pl.BlockSpec((N_EXPERTS, LANES), const2)],
        out_specs=[pl.BlockSpec((TM, D_MODEL), tok),
                   pl.BlockSpec((TM, D_MODEL), tok),
                   pl.BlockSpec((TOP_K, TM), lanes),
                   pl.BlockSpec((TOP_K, TM), lanes),
                   pl.BlockSpec((N_EXPERTS, LANES), tok)],
        out_shape=[jax.ShapeDtypeStruct((T, D_MODEL), F32),
                   jax.ShapeDtypeStruct((T, D_MODEL), BF16),
                   jax.ShapeDtypeStruct((TOP_K, T), F32),
                   jax.ShapeDtypeStruct((TOP_K, T), jnp.int32),
                   jax.ShapeDtypeStruct((T // TM * N_EXPERTS, LANES), F32)],
        compiler_params=_cparams(("arbitrary",)),
    )(na2d, fn2d, x2d, ada3, ada3, ada3, g_post, g_pre, woa, wof, wr2, br)


def _zero_fill(pad_lo_ref, pad_n_ref, tail_ref, xs_hbm, zbuf, zsem, n_blocks, wait):
    def run(cp):
        cp.wait() if wait else cp.start()

    def expert(e, c):
        pos = pad_lo_ref[e]
        n = pad_n_ref[e]
        for size in _pow2_sizes(ZERO_ROWS):
            @pl.when((n & size) != 0)
            def _(pos=pos, size=size):
                dst = xs_hbm.at[pl.ds(pl.multiple_of(pos, SUBLANES), size)]
                run(pltpu.make_async_copy(zbuf.at[pl.ds(0, size)], dst, zsem))
            pos = pos + (n & size)
        return c

    lax.fori_loop(0, N_EXPERTS, expert, 0)

    def tail(blk, c):
        for half in range(MOE_BLOCK // ZERO_ROWS):
            row0 = pl.multiple_of(blk * MOE_BLOCK + half * ZERO_ROWS, ZERO_ROWS)
            run(pltpu.make_async_copy(zbuf, xs_hbm.at[pl.ds(row0, ZERO_ROWS)], zsem))
        return c

    lax.fori_loop(tail_ref[0], n_blocks, tail, 0)


def _tile_pieces(cnt_ref, off_ref, dst_ref, tile, local_buf, sorted_hbm, sem, to_sorted, fill_copy):
    def expert(e, c):
        j = tile * N_EXPERTS + e
        n = cnt_ref[j]
        src = off_ref[j]
        dst = dst_ref[j]
        for size in _pow2_sizes(TM):
            @pl.when((n & size) != 0)
            def _(src=src, dst=dst, size=size):
                loc = local_buf.at[pl.ds(pl.multiple_of(src, SUBLANES), size)]
                glob = sorted_hbm.at[pl.ds(pl.multiple_of(dst, SUBLANES), size)]
                cp = pltpu.make_async_copy(loc, glob, sem) if to_sorted else pltpu.make_async_copy(glob, loc, sem)
                cp.start()
            src = src + (n & size)
            dst = dst + (n & size)
        return c

    lax.fori_loop(0, N_EXPERTS, expert, 0)

    last = tile * N_EXPERTS + N_EXPERTS - 1
    moved = off_ref[last] + cnt_ref[last]
    fill = TILE_ROWS - moved
    done = 0
    for size in _pow2_sizes(FILL_ROWS):
        @pl.when((fill & size) != 0)
        def _(done=done, size=size):
            fill_copy(pl.multiple_of(moved + done, SUBLANES), pl.multiple_of(done, SUBLANES), size).start()
        done = done + (fill & size)


def _tile_wait(local_buf, sorted_hbm, sem):
    for c in range(TILE_ROWS // SORT_CHUNK):
        rows = pl.ds(c * SORT_CHUNK, SORT_CHUNK)
        pltpu.make_async_copy(local_buf.at[rows], sorted_hbm.at[rows], sem).wait()


def _dispatch_kernel(pad_lo_ref, pad_n_ref, tail_ref, cnt_ref, off_ref, dst_ref,
                     pos_ref, h2a_ref, h2b_ref, xs_hbm, spare_hbm, sbuf, zbuf, sem, zsem,
                     *, n_blocks, tiles_a, n_tiles):
    i = pl.program_id(0)
    buf = i % 2

    @pl.when(i == 0)
    def _():
        zbuf[...] = jnp.zeros_like(zbuf)
        _zero_fill(pad_lo_ref, pad_n_ref, tail_ref, xs_hbm, zbuf, zsem, n_blocks, wait=False)
        _zero_fill(pad_lo_ref, pad_n_ref, tail_ref, xs_hbm, zbuf, zsem, n_blocks, wait=True)
        for b in range(2):
            pltpu.make_async_copy(zbuf.at[pl.ds(0, FILL_ROWS)], spare_hbm.at[b], zsem).start()
        for b in range(2):
            pltpu.make_async_copy(zbuf.at[pl.ds(0, FILL_ROWS)], spare_hbm.at[b], zsem).wait()

    def fill_copy(local_row, fill_row, size):
        del local_row
        return pltpu.make_async_copy(zbuf.at[pl.ds(0, size)], spare_hbm.at[buf, pl.ds(fill_row, size)], sem.at[buf])

    @pl.when(i >= 2)
    def _():
        _tile_wait(sbuf.at[buf], xs_hbm, sem.at[buf])

    def sort_tile(h2_ref):
        h = h2_ref[...]
        pos = pos_ref[...]
        for c in range(TILE_ROWS // SORT_CHUNK):
            row = lax.broadcasted_iota(jnp.int32, (SORT_CHUNK, TM), 0) + c * SORT_CHUNK
            sel = jnp.zeros((SORT_CHUNK, TM), F32)
            for k in range(TOP_K):
                sel = jnp.where(row == pos[k:k + 1, :], 1.0, sel)
            sbuf[buf, c * SORT_CHUNK:(c + 1) * SORT_CHUNK, :] = jnp.dot(sel.astype(BF16), h,
                                                                        preferred_element_type=F32)

    @pl.when(i < tiles_a)
    def _():
        sort_tile(h2a_ref)

    @pl.when(i >= tiles_a)
    def _():
        sort_tile(h2b_ref)

    _tile_pieces(cnt_ref, off_ref, dst_ref, i, sbuf.at[buf], xs_hbm, sem.at[buf], True, fill_copy)

    @pl.when(i == n_tiles - 1)
    def _():
        if n_tiles >= 2:
            _tile_wait(sbuf.at[1 - buf], xs_hbm, sem.at[1 - buf])
        _tile_wait(sbuf.at[buf], xs_hbm, sem.at[buf])


def _dispatch(pad_lo, pad_n, tail, cnt, off, dst, pos, h2a, h2b, n_blocks):
    tiles_a = h2a.shape[0] // TM
    tiles_b = h2b.shape[0] // TM
    n_tiles = tiles_a + tiles_b
    P = n_blocks * MOE_BLOCK
    kern = functools.partial(_dispatch_kernel, n_blocks=n_blocks, tiles_a=tiles_a, n_tiles=n_tiles)
    return pl.pallas_call(
        kern,
        grid_spec=pltpu.PrefetchScalarGridSpec(
            num_scalar_prefetch=6,
            grid=(n_tiles,),
            in_specs=[pl.BlockSpec((TOP_K, TM), lambda i, *_: (0, i)),
                      pl.BlockSpec((TM, D_MODEL), lambda i, *_: (jnp.minimum(i, tiles_a - 1), 0)),
                      pl.BlockSpec((TM, D_MODEL), lambda i, *_: (jnp.maximum(i - tiles_a, 0), 0))],
            out_specs=[pl.BlockSpec(memory_space=pl.ANY), pl.BlockSpec(memory_space=pl.ANY)],
            scratch_shapes=[pltpu.VMEM((2, TILE_ROWS, D_MODEL), F32),
                            pltpu.VMEM((ZERO_ROWS, D_MODEL), F32),
                            pltpu.SemaphoreType.DMA((2,)),
                            pltpu.SemaphoreType.DMA]),
        out_shape=[jax.ShapeDtypeStruct((P, D_MODEL), F32),
                   jax.ShapeDtypeStruct((2, FILL_ROWS, D_MODEL), F32)],
        compiler_params=_cparams(("arbitrary",)),
    )(pad_lo, pad_n, tail, cnt, off, dst, pos, h2a, h2b)[0]


def _expert_kernel(be_ref, used_ref, x_ref, wgu_ref, bgu_ref, wd_ref, bd_ref, y_ref):
    del be_ref
    live = pl.program_id(0) < used_ref[0]

    @pl.when(live)
    def _():
        for part in range(MOE_BLOCK // EXPERT_ROWS):
            rows = slice(part * EXPERT_ROWS, (part + 1) * EXPERT_ROWS)
            x = x_ref[rows, :].astype(BF16)
            gu = jnp.dot(x, wgu_ref[0], preferred_element_type=F32) + bgu_ref[0]
            glu = jnp.minimum(gu[:, :D_FF], SWIGLU_LIMIT)
            lin = jnp.clip(gu[:, D_FF:], -SWIGLU_LIMIT, SWIGLU_LIMIT)
            act = glu * jax.nn.sigmoid(SWIGLU_ALPHA * glu) * (lin + 1.0)
            y_ref[rows, :] = jnp.dot(act.astype(BF16), wd_ref[0], preferred_element_type=F32) + bd_ref[0]

    @pl.when(jnp.logical_not(live))
    def _():
        y_ref[...] = jnp.zeros_like(y_ref)


def _experts(block_e, n_used, xs, wgu, bgu, wd, bd):
    n_blocks = block_e.shape[0]
    x_map = lambda i, be, used: (jnp.minimum(i, used[0] - 1), 0)
    return pl.pallas_call(
        _expert_kernel,
        grid_spec=pltpu.PrefetchScalarGridSpec(
            num_scalar_prefetch=2,
            grid=(n_blocks,),
            in_specs=[pl.BlockSpec((MOE_BLOCK, D_MODEL), x_map),
                      pl.BlockSpec((1, D_MODEL, 2 * D_FF), lambda i, be, used: (be[i], 0, 0)),
                      pl.BlockSpec((1, 1, 2 * D_FF), lambda i, be, used: (be[i], 0, 0)),
                      pl.BlockSpec((1, D_FF, D_MODEL), lambda i, be, used: (be[i], 0, 0)),
                      pl.BlockSpec((1, 1, D_MODEL), lambda i, be, used: (be[i], 0, 0))],
            out_specs=pl.BlockSpec((MOE_BLOCK, D_MODEL), lambda i, be, used: (i, 0))),
        out_shape=jax.ShapeDtypeStruct(xs.shape, F32),
        compiler_params=_cparams(("arbitrary",)),
    )(block_e, n_used, xs, wgu, bgu, wd, bd)


def _combine_kernel(cnt_ref, off_ref, dst_ref, ys_hbm, pos_ref, gate_ref, x1_ref, gt2_ref, g_ref, o_ref,
                    ybuf, sem, *, tile0, n_tiles):
    i = pl.program_id(0)
    buf = i % 2

    def fetch(tile, b):
        def fill_copy(local_row, fill_row, size):
            return pltpu.make_async_copy(ys_hbm.at[pl.ds(fill_row, size)], ybuf.at[b, pl.ds(local_row, size)],
                                         sem.at[b])
        _tile_pieces(cnt_ref, off_ref, dst_ref, tile, ybuf.at[b], ys_hbm, sem.at[b], False, fill_copy)

    @pl.when(i == 0)
    def _():
        fetch(tile0, 0)

    @pl.when(i + 1 < n_tiles)
    def _():
        fetch(tile0 + i + 1, 1 - buf)

    _tile_wait(ybuf.at[buf], ys_hbm, sem.at[buf])

    pos = pos_ref[...]
    gate = gate_ref[...]
    moe = jnp.zeros((TM, D_MODEL), F32)
    for c in range(TILE_ROWS // SORT_CHUNK):
        col = lax.broadcasted_iota(jnp.int32, (TM, SORT_CHUNK), 1) + c * SORT_CHUNK
        w = jnp.zeros((TM, SORT_CHUNK), F32)
        for k in range(TOP_K):
            w = jnp.where(col == pos[:, k:k + 1], gate[:, k:k + 1], w)
        rows = ybuf[buf, c * SORT_CHUNK:(c + 1) * SORT_CHUNK, :]
        moe = moe + jnp.dot(w.astype(BF16), rows.astype(BF16), preferred_element_type=F32)
    o_ref[...] = x1_ref[...] + gt2_ref[0] * _rms(moe, g_ref[...])


def _combine(cnt, off, dst, ys, tile0, pos_tk, gates_tk, x1, ada3, ada_row0, bsz, L, g_post):
    T = bsz * L
    tpb = L // TM
    n_tiles = T // TM
    return pl.pallas_call(
        functools.partial(_combine_kernel, tile0=tile0, n_tiles=n_tiles),
        grid_spec=pltpu.PrefetchScalarGridSpec(
            num_scalar_prefetch=3,
            grid=(n_tiles,),
            in_specs=[pl.BlockSpec(memory_space=pl.ANY),
                      pl.BlockSpec((TM, TOP_K), lambda i, *_: (i, 0)),
                      pl.BlockSpec((TM, TOP_K), lambda i, *_: (i, 0)),
                      pl.BlockSpec((TM, D_MODEL), lambda i, *_: (i, 0)),
                      pl.BlockSpec((1, 1, D_MODEL), lambda i, *_: ((ada_row0 + i // tpb) * 6 + 5, 0, 0)),
                      pl.BlockSpec((1, D_MODEL), lambda i, *_: (0, 0))],
            out_specs=pl.BlockSpec((TM, D_MODEL), lambda i, *_: (i, 0)),
            scratch_shapes=[pltpu.VMEM((2, TILE_ROWS, D_MODEL), F32),
                            pltpu.SemaphoreType.DMA((2,))]),
        out_shape=jax.ShapeDtypeStruct((T, D_MODEL), F32),
        compiler_params=_cparams(("arbitrary",)),
    )(cnt, off, dst, ys, pos_tk, gates_tk, x1, ada3, g_post)


def kernel(x_prompt, x_sample, c_prompt, c_sample, w_ada, b_ada, g_pre_mix, g_post_mix, g_pre_ffn, g_post_ffn,
           w_in, rpb, w_fourier, g_na, g_fn, w_out, w_router, b_router, w_gate_up, b_gate_up, w_down, b_down):
    assert w_ada.shape[0] == 1, "single layer"
    groups = [(x_prompt, c_prompt), (x_sample, c_sample)]
    for x, _ in groups:
        assert x.shape[1] % ATT_TOKENS == 0 and x.shape[1] // GRID_W >= 12 and x.shape[2] == D_MODEL

    row2 = lambda a: a[0].reshape(1, -1)
    w_in0 = w_in[0]
    wq = (w_in0[:, :NA_WIDTH] * (HEAD_DIM ** -0.5 * LOG2_E)).astype(BF16)
    wkt = w_in0[:, NA_WIDTH:2 * NA_WIDTH].T.astype(BF16)
    wv = w_in0[:, 2 * NA_WIDTH:3 * NA_WIDTH].astype(BF16)
    wu = w_in0[:, 3 * NA_WIDTH:].astype(BF16)
    cs = _channel_table()
    bias = _bias_table(rpb[0])
    wf = w_fourier[0].astype(BF16)
    woa = w_out[0, :NA_WIDTH].astype(BF16)
    wof = w_out[0, NA_WIDTH:].astype(BF16)
    wr_t = w_router[0].T
    wr_hi = wr_t.astype(BF16)
    wr_lo = (wr_t - wr_hi.astype(F32)).astype(BF16)
    wr2 = jnp.concatenate([wr_hi, wr_lo], axis=0)
    br = jnp.broadcast_to(b_router[0][:, None], (N_EXPERTS, LANES))
    wgu = w_gate_up[0].astype(BF16)
    wd = w_down[0].astype(BF16)
    bgu = b_gate_up[0].reshape(N_EXPERTS, 1, 2 * D_FF)
    bd = b_down[0].reshape(N_EXPERTS, 1, D_MODEL)

    c_all = jnp.concatenate([c_prompt, c_sample], axis=0)
    ada3 = _ada(c_all, w_ada[0], b_ada[0]).reshape(c_all.shape[0] * 6, 1, D_MODEL)

    per_group = []
    ada_row0 = 0
    for x, _ in groups:
        bsz, L, _ = x.shape
        x2d = x.reshape(bsz * L, D_MODEL)
        q, kt4, v, u = _inproj(x2d, ada3, ada_row0, bsz, L, row2(g_pre_mix), wq, wkt, wv, wu)
        na = _attention(q.reshape(bsz, L, NA_WIDTH), kt4, v.reshape(bsz, L, NA_WIDTH), bias, row2(g_na))
        fn = _fourier(u, bsz, L, cs, wf, row2(g_fn))
        x1, h2, gate_t, pos_t, cnt = _outproj(
            na.reshape(bsz * L, NA_WIDTH), fn, x2d, ada3, ada_row0, bsz, L,
            row2(g_post_mix), row2(g_pre_ffn), woa, wof, wr2, br)
        per_group.append((x1, h2, gate_t, pos_t, cnt, bsz, L, ada_row0))
        ada_row0 += bsz

    T = sum(g[5] * g[6] for g in per_group)
    max_rows = T * TOP_K + (T // TM) * N_EXPERTS * (SUBLANES - 1) + N_EXPERTS * (MOE_BLOCK - 1)
    n_blocks = -(-max_rows // MOE_BLOCK)
    tile_cnt = jnp.concatenate([g[4][:, 0].reshape(-1, N_EXPERTS) for g in per_group], axis=0).astype(jnp.int32)
    tile_cnt = _round_up(tile_cnt, SUBLANES)
    total = jnp.sum(tile_cnt, axis=0)
    padded = ((total + MOE_BLOCK - 1) // MOE_BLOCK) * MOE_BLOCK
    pend = jnp.cumsum(padded)
    pstart = pend - padded
    run = jnp.cumsum(tile_cnt, axis=0) - tile_cnt
    tile_dst = (pstart[None, :] + run).reshape(-1)
    tile_off = (jnp.cumsum(tile_cnt, axis=1) - tile_cnt).reshape(-1)
    tile_cnt = tile_cnt.reshape(-1)
    block_row0 = jnp.arange(n_blocks, dtype=jnp.int32) * MOE_BLOCK
    block_e = jnp.minimum(jnp.sum((pend[None, :] <= block_row0[:, None]).astype(jnp.int32), axis=1), N_EXPERTS - 1)
    pad_lo = pstart + total
    pad_n = padded - total
    tail = (pend[-1:] // MOE_BLOCK).astype(jnp.int32)

    pos_all = jnp.concatenate([g[3] for g in per_group], axis=1)
    xs = _dispatch(pad_lo, pad_n, tail, tile_cnt, tile_off, tile_dst, pos_all,
                   per_group[0][1], per_group[1][1], n_blocks)
    ys = _experts(block_e, tail, xs, wgu, bgu, wd, bd)

    outs = []
    tile0 = 0
    for x1, _, gate_t, pos_t, _, bsz, L, row0 in per_group:
        out = _combine(tile_cnt, tile_off, tile_dst, ys, tile0, pos_t.T, gate_t.T, x1, ada3, row0, bsz, L,
                       row2(g_post_ffn))
        outs.append(out.reshape(bsz, L, D_MODEL))
        tile0 += bsz * L // TM
    return tuple(outs)
```

```python
import functools

import numpy as np
import jax
import jax.numpy as jnp
from jax import lax
from jax.experimental import pallas as pl
from jax.experimental.pallas import tpu as pltpu

F32 = jnp.float32
BF16 = jnp.bfloat16

D_MODEL = 1024
GRID_W = 64
NA_WIDTH = 512
HEAD_DIM = 64
NA_HEADS = 8
FN_WIDTH = 512
FN_GROUPS = 4
FN_GROUP_DIM = 128
WIN_ROWS = 8
WIN_COLS = 16
N_EXPERTS = 32
TOP_K = 4
D_FF = 1024
SWIGLU_ALPHA = 1.702
SWIGLU_LIMIT = 7.0
MOE_BLOCK = 512
EXPERT_ROWS = 256
EPS = 1e-6

LANES = 128
PAIR_TOKENS = 2 * GRID_W
SLAB_ROWS = 10
SLAB_TOKENS = SLAB_ROWS * GRID_W
N_BIAS_VARIANTS = 5
MASK_VALUE = -1e30
LOG2_E = float(np.log2(np.e))

TM = 512
SUBLANES = 8
SORT_CHUNK = 256
TILE_ROWS = -(-(TM * TOP_K + N_EXPERTS * (SUBLANES - 1)) // SORT_CHUNK) * SORT_CHUNK
FILL_ROWS = TILE_ROWS - TM * TOP_K
ATT_TOKENS = 1024
DFT_LANES = 4096
DFT_K1 = 8
ZERO_ROWS = MOE_BLOCK // 2
VMEM_LIMIT = 56 * 1024 * 1024


def _cparams(sem):
    return pltpu.CompilerParams(dimension_semantics=sem, vmem_limit_bytes=VMEM_LIMIT)


def _rms(x, g):
    return x * lax.rsqrt(jnp.mean(x * x, axis=-1, keepdims=True) + EPS) * g


def _pow2_sizes(limit):
    sizes = []
    while limit >= SUBLANES:
        sizes.append(limit)
        limit //= 2
    return sizes


def _round_up(n, m):
    return (n + m - 1) // m * m


def _ada_kernel(c_ref, w_ref, b_ref, o_ref):
    c = c_ref[...]
    s = c * jax.nn.sigmoid(c)
    o_ref[...] = jnp.dot(s.astype(BF16), w_ref[...].astype(BF16), preferred_element_type=F32) + b_ref[...]


def _ada(c_all, w_ada, b_ada):
    nb = c_all.shape[0]
    n_out = w_ada.shape[1]
    tn = 1536
    return pl.pallas_call(
        _ada_kernel,
        grid=(n_out // tn,),
        in_specs=[pl.BlockSpec((nb, D_MODEL), lambda j: (0, 0)),
                  pl.BlockSpec((D_MODEL, tn), lambda j: (0, j)),
                  pl.BlockSpec((1, tn), lambda j: (0, j))],
        out_specs=pl.BlockSpec((nb, tn), lambda j: (0, j)),
        out_shape=jax.ShapeDtypeStruct((nb, n_out), F32),
        compiler_params=_cparams(("arbitrary",)),
    )(c_all, w_ada, b_ada.reshape(1, n_out))


def _inproj_kernel(x_ref, sh_ref, sc_ref, g_ref, wq_ref, wkt_ref, wv_ref, wu_ref,
                   q_ref, kt_ref, v_ref, u_ref):
    h = _rms(x_ref[...], g_ref[...]) * (1.0 + sc_ref[0]) + sh_ref[0]
    hb = h.astype(BF16)
    q_ref[...] = jnp.dot(hb, wq_ref[...], preferred_element_type=F32).astype(BF16)
    v_ref[...] = jnp.dot(hb, wv_ref[...], preferred_element_type=F32).astype(BF16)
    kt = lax.dot_general(wkt_ref[...], hb, (((1,), (1,)), ((), ())), preferred_element_type=F32).astype(BF16)
    for c in range(TM // PAIR_TOKENS):
        kt_ref[0, c] = kt[:, c * PAIR_TOKENS:(c + 1) * PAIR_TOKENS]
    u_ref[...] = jnp.dot(hb, wu_ref[...], preferred_element_type=F32).astype(BF16)


def _inproj(x2d, ada3, ada_row0, bsz, L, g_pre, wq, wkt, wv, wu):
    T = bsz * L
    tpb = L // TM
    row = lambda j: (lambda i: ((ada_row0 + i // tpb) * 6 + j, 0, 0))
    const2 = lambda i: (0, 0)
    return pl.pallas_call(
        _inproj_kernel,
        grid=(T // TM,),
        in_specs=[pl.BlockSpec((TM, D_MODEL), lambda i: (i, 0)),
                  pl.BlockSpec((1, 1, D_MODEL), row(0)),
                  pl.BlockSpec((1, 1, D_MODEL), row(1)),
                  pl.BlockSpec((1, D_MODEL), const2),
                  pl.BlockSpec((D_MODEL, NA_WIDTH), const2),
                  pl.BlockSpec((NA_WIDTH, D_MODEL), const2),
                  pl.BlockSpec((D_MODEL, NA_WIDTH), const2),
                  pl.BlockSpec((D_MODEL, FN_WIDTH), const2)],
        out_specs=[pl.BlockSpec((TM, NA_WIDTH), lambda i: (i, 0)),
                   pl.BlockSpec((1, TM // PAIR_TOKENS, NA_WIDTH, PAIR_TOKENS),
                                lambda i: (i // tpb, i % tpb, 0, 0)),
                   pl.BlockSpec((TM, NA_WIDTH), lambda i: (i, 0)),
                   pl.BlockSpec((TM, FN_WIDTH), lambda i: (i, 0))],
        out_shape=[jax.ShapeDtypeStruct((T, NA_WIDTH), BF16),
                   jax.ShapeDtypeStruct((bsz, L // PAIR_TOKENS, NA_WIDTH, PAIR_TOKENS), BF16),
                   jax.ShapeDtypeStruct((T, NA_WIDTH), BF16),
                   jax.ShapeDtypeStruct((T, FN_WIDTH), BF16)],
        compiler_params=_cparams(("arbitrary",)),
    )(x2d, ada3, ada3, g_pre, wq, wkt, wv, wu)


def _bias_table(rpb):
    rows = 32
    n_pairs = rows // 2
    c = np.arange(GRID_W)
    kc0 = np.clip(c - WIN_COLS // 2, 0, GRID_W - WIN_COLS)
    col_ok = (c[None, :] >= kc0[:, None]) & (c[None, :] < kc0[:, None] + WIN_COLS)
    ci = c[None, :] - c[:, None] + WIN_COLS - 1
    pick = ((ci[None] == np.arange(2 * WIN_COLS - 1)[:, None, None]) & col_ok[None]).astype(np.float32)
    toe = jnp.einsum('hrk,kcd->hrcd', rpb, jnp.asarray(pick), precision=lax.Precision.HIGHEST)
    toe = jnp.where(col_ok, toe * LOG2_E, MASK_VALUE)
    masked = jnp.full((NA_HEADS, GRID_W, GRID_W), MASK_VALUE, F32)
    variants = []
    for j in (0, 1, 2, n_pairs - 2, n_pairs - 1):
        start = int(np.clip(2 * j - 4, 0, rows - SLAB_ROWS))
        halves = []
        for rl in range(2):
            r = 2 * j + rl
            kr0 = int(np.clip(r - WIN_ROWS // 2, 0, rows - WIN_ROWS))
            blocks = []
            for i in range(SLAB_ROWS):
                kr = start + i
                blocks.append(toe[:, kr - r + WIN_ROWS - 1] if kr0 <= kr < kr0 + WIN_ROWS else masked)
            halves.append(jnp.concatenate(blocks, axis=2))
        variants.append(jnp.concatenate(halves, axis=1))
    return jnp.stack(variants)


def _attn_kernel(q_ref, kt_ref, v_ref, bias_ref, g_ref, o_ref, *, n_pairs):
    step = pl.program_id(1)
    pairs_per_step = ATT_TOKENS // PAIR_TOKENS
    lane = lax.broadcasted_iota(jnp.int32, (PAIR_TOKENS, 2 * HEAD_DIM), 1)
    first_head = lane < HEAD_DIM

    def pair_body(p, carry):
        j = step * pairs_per_step + p
        start2 = jnp.clip(j - 2, 0, n_pairs - SLAB_ROWS // 2)
        variant = jnp.where(j < 2, j, jnp.where(j >= n_pairs - 2, j - (n_pairs - 2) + 3, 2))
        qp = q_ref[0, pl.ds(pl.multiple_of(p * PAIR_TOKENS, PAIR_TOKENS), PAIR_TOKENS), :]
        kt5 = kt_ref[0, pl.ds(start2, SLAB_ROWS // 2)]
        vs = v_ref[0, pl.ds(pl.multiple_of(start2 * PAIR_TOKENS, PAIR_TOKENS), SLAB_TOKENS), :]
        head_pairs = range(NA_HEADS // 2)
        dims = [slice(hp * 2 * HEAD_DIM, (hp + 1) * 2 * HEAD_DIM) for hp in head_pairs]
        scores, probs, sums, outs = {}, {}, {}, {}

        def score_phase(hp):
            q2 = qp[:, dims[hp]]
            kt2 = jnp.concatenate([kt5[c, dims[hp], :] for c in range(SLAB_ROWS // 2)], axis=1)
            zero = jnp.zeros_like(q2)
            qm = jnp.concatenate([jnp.where(first_head, q2, zero), jnp.where(first_head, zero, q2)], axis=0)
            scores[hp] = jnp.dot(qm, kt2, preferred_element_type=F32)

        def softmax_phase(hp):
            b2 = jnp.concatenate([bias_ref[variant, hp * 2], bias_ref[variant, hp * 2 + 1]], axis=0)
            s = scores[hp] + b2
            e = jnp.exp2(s - jnp.max(s, axis=-1, keepdims=True))
            sums[hp] = jnp.sum(e, axis=-1, keepdims=True)
            probs[hp] = e.astype(BF16)

        def value_phase(hp):
            o = jnp.dot(probs[hp], vs[:, dims[hp]], preferred_element_type=F32) / sums[hp]
            outs[hp] = jnp.where(first_head, o[:PAIR_TOKENS], o[PAIR_TOKENS:])

        n_hp = NA_HEADS // 2
        for t in range(n_hp + 2):
            if t < n_hp:
                score_phase(t)
            if 0 <= t - 1 < n_hp:
                softmax_phase(t - 1)
            if 0 <= t - 2 < n_hp:
                value_phase(t - 2)
        a = jnp.concatenate([outs[hp] for hp in head_pairs], axis=1)
        o_ref[0, pl.ds(pl.multiple_of(p * PAIR_TOKENS, PAIR_TOKENS), PAIR_TOKENS), :] = _rms(a, g_ref[...]).astype(BF16)
        return carry

    lax.fori_loop(0, pairs_per_step, pair_body, 0)


def _attention(q3, kt4, v3, bias, g_na):
    bsz, L, _ = q3.shape
    n_pairs = L // PAIR_TOKENS
    return pl.pallas_call(
        functools.partial(_attn_kernel, n_pairs=n_pairs),
        grid=(bsz, L // ATT_TOKENS),
        in_specs=[pl.BlockSpec((1, ATT_TOKENS, NA_WIDTH), lambda b, s: (b, s, 0)),
                  pl.BlockSpec((1, n_pairs, NA_WIDTH, PAIR_TOKENS), lambda b, s: (b, 0, 0, 0)),
                  pl.BlockSpec((1, L, NA_WIDTH), lambda b, s: (b, 0, 0)),
                  pl.BlockSpec((N_BIAS_VARIANTS, NA_HEADS, PAIR_TOKENS, SLAB_TOKENS), lambda b, s: (0, 0, 0, 0),
                               pipeline_mode=pl.Buffered(1)),
                  pl.BlockSpec((1, NA_WIDTH), lambda b, s: (0, 0))],
        out_specs=pl.BlockSpec((1, ATT_TOKENS, NA_WIDTH), lambda b, s: (b, s, 0)),
        out_shape=jax.ShapeDtypeStruct((bsz, L, NA_WIDTH), BF16),
        compiler_params=_cparams(("arbitrary", "arbitrary")),
    )(q3, kt4, v3, bias, g_na)


def _angles(j, k, n):
    return ((j * k) % n).astype(F32) * (2.0 * np.pi / n)


def _row_stage_table(R):
    k = jnp.arange(R, dtype=jnp.int32)
    ang = _angles(k[:, None], k[None, :], R)
    return jnp.concatenate([jnp.cos(ang), -jnp.sin(ang)], axis=0).astype(BF16)


def _col_stage_table(L):
    R = L // GRID_W
    k1 = jnp.arange(R, dtype=jnp.int32)[:, None, None]
    k2 = jnp.arange(GRID_W, dtype=jnp.int32)[None, :, None]
    c = jnp.arange(GRID_W, dtype=jnp.int32)[None, None, :]
    ang = _angles(c, k1 + R * k2, L)
    cc, ss = jnp.cos(ang), jnp.sin(ang)
    top = jnp.concatenate([cc, ss], axis=2)
    bot = jnp.concatenate([-ss, cc], axis=2)
    return jnp.concatenate([top, bot], axis=1).astype(BF16)


def _channel_table():
    j = jnp.arange(FN_GROUP_DIM, dtype=jnp.int32)
    ang = _angles(j[:, None], j[None, :], FN_GROUP_DIM)
    return jnp.concatenate([jnp.cos(ang), jnp.sin(ang)], axis=0).astype(BF16)


def _dft_rows_kernel(f_ref, x_ref, y_ref):
    y_ref[0] = jnp.dot(f_ref[...], x_ref[0], preferred_element_type=F32).astype(BF16)


def _dft_rows(f_rows, u3):
    bsz, R, width = u3.shape
    return pl.pallas_call(
        _dft_rows_kernel,
        grid=(bsz, width // DFT_LANES),
        in_specs=[pl.BlockSpec((2 * R, R), lambda b, t: (0, 0)),
                  pl.BlockSpec((1, R, DFT_LANES), lambda b, t: (b, 0, t))],
        out_specs=pl.BlockSpec((1, 2 * R, DFT_LANES), lambda b, t: (b, 0, t)),
        out_shape=jax.ShapeDtypeStruct((bsz, 2 * R, width), BF16),
        compiler_params=_cparams(("arbitrary", "arbitrary")),
    )(f_rows, u3)


def _dft_cols_kernel(yr_ref, yi_ref, m_ref, cs_ref, wf_ref, g_ref, o_ref):
    re, im = [], []
    for j in range(DFT_K1):
        y = jnp.concatenate([yr_ref[0, j], yi_ref[0, j]], axis=0)
        x = jnp.dot(m_ref[j], y, preferred_element_type=F32)
        re.append(x[:GRID_W])
        im.append(x[GRID_W:])
    a = jnp.concatenate(re, axis=0).astype(BF16)
    b = jnp.concatenate(im, axis=0).astype(BF16)
    ys = []
    for g in range(FN_GROUPS):
        sl = slice(g * FN_GROUP_DIM, (g + 1) * FN_GROUP_DIM)
        ab = jnp.concatenate([a[:, sl], b[:, sl]], axis=1)
        z = jnp.dot(ab, cs_ref[...], preferred_element_type=F32).astype(BF16)
        ys.append(jnp.dot(z, wf_ref[g], preferred_element_type=F32))
    out = _rms(jnp.concatenate(ys, axis=1), g_ref[...]).astype(BF16)
    for j in range(DFT_K1):
        o_ref[0, :, j * FN_WIDTH:(j + 1) * FN_WIDTH] = out[j * GRID_W:(j + 1) * GRID_W, :]


def _dft_cols(y4, m_cols, cs, wf, g_fn):
    bsz, R2, _, _ = y4.shape
    R = R2 // 2
    steps = R // DFT_K1
    return pl.pallas_call(
        _dft_cols_kernel,
        grid=(bsz, steps),
        in_specs=[pl.BlockSpec((1, DFT_K1, GRID_W, FN_WIDTH), lambda b, t: (b, t, 0, 0)),
                  pl.BlockSpec((1, DFT_K1, GRID_W, FN_WIDTH), lambda b, t: (b, steps + t, 0, 0)),
                  pl.BlockSpec((DFT_K1, 2 * GRID_W, 2 * GRID_W), lambda b, t: (t, 0, 0)),
                  pl.BlockSpec((2 * FN_GROUP_DIM, FN_GROUP_DIM), lambda b, t: (0, 0)),
                  pl.BlockSpec((FN_GROUPS, FN_GROUP_DIM, FN_GROUP_DIM), lambda b, t: (0, 0, 0)),
                  pl.BlockSpec((1, FN_WIDTH), lambda b, t: (0, 0))],
        out_specs=pl.BlockSpec((1, GRID_W, DFT_K1 * FN_WIDTH), lambda b, t: (b, 0, t)),
        out_shape=jax.ShapeDtypeStruct((bsz, GRID_W, R * FN_WIDTH), BF16),
        compiler_params=_cparams(("arbitrary", "arbitrary")),
    )(y4, y4, m_cols, cs, wf, g_fn)


def _fourier(u, bsz, L, cs, wf, g_fn):
    R = L // GRID_W
    y = _dft_rows(_row_stage_table(R), u.reshape(bsz, R, GRID_W * FN_WIDTH))
    fn = _dft_cols(y.reshape(bsz, 2 * R, GRID_W, FN_WIDTH), _col_stage_table(L), cs, wf, g_fn)
    return fn.reshape(bsz * L, FN_WIDTH)


def _outproj_kernel(na_ref, fn_ref, x_ref, gt1_ref, sh2_ref, sc2_ref, gpost_ref, gpre_ref,
                    woa_ref, wof_ref, wr_ref, br_ref,
                    x1_ref, h2_ref, gate_ref, pos_ref, cnt_ref):
    mix = (jnp.dot(na_ref[...], woa_ref[...], preferred_element_type=F32)
           + jnp.dot(fn_ref[...], wof_ref[...], preferred_element_type=F32))
    x1 = x_ref[...] + gt1_ref[0] * _rms(mix, gpost_ref[...])
    x1_ref[...] = x1
    h2 = _rms(x1, gpre_ref[...]) * (1.0 + sc2_ref[0]) + sh2_ref[0]
    hi = h2.astype(BF16)
    h2_ref[...] = hi

    lo = (h2 - hi.astype(F32)).astype(BF16)
    nt = (((1,), (1,)), ((), ()))
    a = lax.dot_general(wr_ref[...], hi, nt, preferred_element_type=F32)
    b = lax.dot_general(wr_ref[0:N_EXPERTS, :], lo, nt, preferred_element_type=F32)
    logits = a[:N_EXPERTS] + a[N_EXPERTS:] + b + br_ref[:, 0:1]

    eidx = lax.broadcasted_iota(jnp.int32, (N_EXPERTS, TM), 0)
    vals, onehots = [], []
    cur = logits
    for k in range(TOP_K):
        m = jnp.max(cur, axis=0, keepdims=True)
        idx = jnp.min(jnp.where(cur == m, eidx, N_EXPERTS), axis=0, keepdims=True)
        hit = eidx == idx
        vals.append(m)
        onehots.append(hit)
        cur = jnp.where(hit, -jnp.inf, cur)
    exps = [jnp.exp(v - vals[0]) for v in vals]
    den = exps[0] + exps[1] + exps[2] + exps[3]
    for k in range(TOP_K):
        gate_ref[k:k + 1, :] = exps[k] / den

    oh = [h.astype(F32) for h in onehots]
    oh_sum = oh[0] + oh[1] + oh[2] + oh[3]
    s_i = lax.broadcasted_iota(jnp.int32, (TM, TM), 0)
    t_i = lax.broadcasted_iota(jnp.int32, (TM, TM), 1)
    earlier = jnp.where(s_i < t_i, 1.0, 0.0).astype(BF16)
    same_expert_before = jnp.dot(oh_sum.astype(BF16), earlier, preferred_element_type=F32)
    counts = jnp.broadcast_to(jnp.sum(oh_sum, axis=1, keepdims=True), (N_EXPERTS, LANES))
    cnt_ref[...] = counts
    run_tiles = jnp.floor((counts + (SUBLANES - 1)) * (1.0 / SUBLANES))
    e_r = lax.broadcasted_iota(jnp.int32, (N_EXPERTS, N_EXPERTS), 0)
    e_c = lax.broadcasted_iota(jnp.int32, (N_EXPERTS, N_EXPERTS), 1)
    lower = jnp.where(e_c < e_r, 1.0, 0.0).astype(BF16)
    run_start = SUBLANES * jnp.dot(lower, run_tiles.astype(BF16), preferred_element_type=F32)[:, 0:1]
    base = run_start + same_expert_before
    for k in range(TOP_K):
        pos_ref[k:k + 1, :] = jnp.sum(oh[k] * base, axis=0, keepdims=True).astype(jnp.int32)


def _outproj(na2d, fn2d, x2d, ada3, ada_row0, bsz, L, g_post, g_pre, woa, wof, wr2, br):
    T = bsz * L
    tpb = L // TM
    row = lambda j: (lambda i: ((ada_row0 + i // tpb) * 6 + j, 0, 0))
    const2 = lambda i: (0, 0)
    tok = lambda i: (i, 0)
    lanes = lambda i: (0, i)
    return pl.pallas_call(
        _outproj_kernel,
        grid=(T // TM,),
        in_specs=[pl.BlockSpec((TM, NA_WIDTH), tok),
                  pl.BlockSpec((TM, FN_WIDTH), tok),
                  pl.BlockSpec((TM, D_MODEL), tok),
                  pl.BlockSpec((1, 1, D_MODEL), row(2)),
                  pl.BlockSpec((1, 1, D_MODEL), row(3)),
                  pl.BlockSpec((1, 1, D_MODEL), row(4)),
                  pl.BlockSpec((1, D_MODEL), const2),
                  pl.BlockSpec((1, D_MODEL), const2),
                  pl.BlockSpec((NA_WIDTH, D_MODEL), const2),
                  pl.BlockSpec((FN_WIDTH, D_MODEL), const2),
                  pl.BlockSpec((2 * N_EXPERTS, D_MODEL), const2),
                  pl.BlockSpec((N_EXPERTS, LANES), const2)],
        out_specs=[pl.BlockSpec((TM, D_MODEL), tok),
                   pl.BlockSpec((TM, D_MODEL), tok),
                   pl.BlockSpec((TOP_K, TM), lanes),
                   pl.BlockSpec((TOP_K, TM), lanes),
                   pl.BlockSpec((N_EXPERTS, LANES), tok)],
        out_shape=[jax.ShapeDtypeStruct((T, D_MODEL), F32),
                   jax.ShapeDtypeStruct((T, D_MODEL), BF16),
                   jax.ShapeDtypeStruct((TOP_K, T), F32),
                   jax.ShapeDtypeStruct((TOP_K, T), jnp.int32),
                   jax.ShapeDtypeStruct((T // TM * N_EXPERTS, LANES), F32)],
        compiler_params=_cparams(("arbitrary",)),
    )(na2d, fn2d, x2d, ada3, ada3, ada3, g_post, g_pre, woa, wof, wr2, br)


def _zero_fill(pad_lo_ref, pad_n_ref, tail_ref, xs_hbm, zbuf, zsem, n_blocks, wait):
    def run(cp):
        cp.wait() if wait else cp.start()

    def expert(e, c):
        pos = pad_lo_ref[e]
        n = pad_n_ref[e]
        for size in _pow2_sizes(ZERO_ROWS):
            @pl.when((n & size) != 0)
            def _(pos=pos, size=size):
                dst = xs_hbm.at[pl.ds(pl.multiple_of(pos, SUBLANES), size)]
                run(pltpu.make_async_copy(zbuf.at[pl.ds(0, size)], dst, zsem))
            pos = pos + (n & size)
        return c

    lax.fori_loop(0, N_EXPERTS, expert, 0)

    def tail(blk, c):
        for half in range(MOE_BLOCK // ZERO_ROWS):
            row0 = pl.multiple_of(blk * MOE_BLOCK + half * ZERO_ROWS, ZERO_ROWS)
            run(pltpu.make_async_copy(zbuf, xs_hbm.at[pl.ds(row0, ZERO_ROWS)], zsem))
        return c

    lax.fori_loop(tail_ref[0], n_blocks, tail, 0)


def _tile_pieces(cnt_ref, off_ref, dst_ref, tile, local_buf, sorted_hbm, sem, to_sorted, fill_copy):
    def expert(e, c):
        j = tile * N_EXPERTS + e
        n = cnt_ref[j]
        src = off_ref[j]
        dst = dst_ref[j]
        for size in _pow2_sizes(TM):
            @pl.when((n & size) != 0)
            def _(src=src, dst=dst, size=size):
                loc = local_buf.at[pl.ds(pl.multiple_of(src, SUBLANES), size)]
                glob = sorted_hbm.at[pl.ds(pl.multiple_of(dst, SUBLANES), size)]
                cp = pltpu.make_async_copy(loc, glob, sem) if to_sorted else pltpu.make_async_copy(glob, loc, sem)
                cp.start()
            src = src + (n & size)
            dst = dst + (n & size)
        return c

    lax.fori_loop(0, N_EXPERTS, expert, 0)

    last = tile * N_EXPERTS + N_EXPERTS - 1
    moved = off_ref[last] + cnt_ref[last]
    fill = TILE_ROWS - moved
    done = 0
    for size in _pow2_sizes(FILL_ROWS):
        @pl.when((fill & size) != 0)
        def _(done=done, size=size):
            fill_copy(pl.multiple_of(moved + done, SUBLANES), pl.multiple_of(done, SUBLANES), size).start()
        done = done + (fill & size)


def _tile_wait(local_buf, sorted_hbm, sem):
    for c in range(TILE_ROWS // SORT_CHUNK):
        rows = pl.ds(c * SORT_CHUNK, SORT_CHUNK)
        pltpu.make_async_copy(local_buf.at[rows], sorted_hbm.at[rows], sem).wait()


def _dispatch_kernel(pad_lo_ref, pad_n_ref, tail_ref, cnt_ref, off_ref, dst_ref,
                     pos_ref, h2a_ref, h2b_ref, xs_hbm, spare_hbm, sbuf, zbuf, sem, zsem,
                     *, n_blocks, tiles_a, n_tiles):
    i = pl.program_id(0)
    buf = i % 2

    @pl.when(i == 0)
    def _():
        zbuf[...] = jnp.zeros_like(zbuf)
        _zero_fill(pad_lo_ref, pad_n_ref, tail_ref, xs_hbm, zbuf, zsem, n_blocks, wait=False)
        _zero_fill(pad_lo_ref, pad_n_ref, tail_ref, xs_hbm, zbuf, zsem, n_blocks, wait=True)
        for b in range(2):
            pltpu.make_async_copy(zbuf.at[pl.ds(0, FILL_ROWS)], spare_hbm.at[b], zsem).start()
        for b in range(2):
            pltpu.make_async_copy(zbuf.at[pl.ds(0, FILL_ROWS)], spare_hbm.at[b], zsem).wait()

    def fill_copy(local_row, fill_row, size):
        del local_row
        return pltpu.make_async_copy(zbuf.at[pl.ds(0, size)], spare_hbm.at[buf, pl.ds(fill_row, size)], sem.at[buf])

    @pl.when(i >= 2)
    def _():
        _tile_wait(sbuf.at[buf], xs_hbm, sem.at[buf])

    def sort_tile(h2_ref):
        h = h2_ref[...]
        pos = pos_ref[...]
        for c in range(TILE_ROWS // SORT_CHUNK):
            row = lax.broadcasted_iota(jnp.int32, (SORT_CHUNK, TM), 0) + c * SORT_CHUNK
            sel = jnp.zeros((SORT_CHUNK, TM), F32)
            for k in range(TOP_K):
                sel = jnp.where(row == pos[k:k + 1, :], 1.0, sel)
            sbuf[buf, c * SORT_CHUNK:(c + 1) * SORT_CHUNK, :] = jnp.dot(sel.astype(BF16), h,
                                                                        preferred_element_type=F32)

    @pl.when(i < tiles_a)
    def _():
        sort_tile(h2a_ref)

    @pl.when(i >= tiles_a)
    def _():
        sort_tile(h2b_ref)

    _tile_pieces(cnt_ref, off_ref, dst_ref, i, sbuf.at[buf], xs_hbm, sem.at[buf], True, fill_copy)

    @pl.when(i == n_tiles - 1)
    def _():
        if n_tiles >= 2:
            _tile_wait(sbuf.at[1 - buf], xs_hbm, sem.at[1 - buf])
        _tile_wait(sbuf.at[buf], xs_hbm, sem.at[buf])


def _dispatch(pad_lo, pad_n, tail, cnt, off, dst, pos, h2a, h2b, n_blocks):
    tiles_a = h2a.shape[0] // TM
    tiles_b = h2b.shape[0] // TM
    n_tiles = tiles_a + tiles_b
    P = n_blocks * MOE_BLOCK
    kern = functools.partial(_dispatch_kernel, n_blocks=n_blocks, tiles_a=tiles_a, n_tiles=n_tiles)
    return pl.pallas_call(
        kern,
        grid_spec=pltpu.PrefetchScalarGridSpec(
            num_scalar_prefetch=6,
            grid=(n_tiles,),
            in_specs=[pl.BlockSpec((TOP_K, TM), lambda i, *_: (0, i)),
                      pl.BlockSpec((TM, D_MODEL), lambda i, *_: (jnp.minimum(i, tiles_a - 1), 0)),
                      pl.BlockSpec((TM, D_MODEL), lambda i, *_: (jnp.maximum(i - tiles_a, 0), 0))],
            out_specs=[pl.BlockSpec(memory_space=pl.ANY), pl.BlockSpec(memory_space=pl.ANY)],
            scratch_shapes=[pltpu.VMEM((2, TILE_ROWS, D_MODEL), F32),
                            pltpu.VMEM((ZERO_ROWS, D_MODEL), F32),
                            pltpu.SemaphoreType.DMA((2,)),
                            pltpu.SemaphoreType.DMA]),
        out_shape=[jax.ShapeDtypeStruct((P, D_MODEL), F32),
                   jax.ShapeDtypeStruct((2, FILL_ROWS, D_MODEL), F32)],
        compiler_params=_cparams(("arbitrary",)),
    )(pad_lo, pad_n, tail, cnt, off, dst, pos, h2a, h2b)[0]


def _expert_kernel(be_ref, used_ref, x_ref, wgu_ref, bgu_ref, wd_ref, bd_ref, y_ref, wgu_bf, wd_bf):
    i = pl.program_id(0)
    live = i < used_ref[0]

    @pl.when(jnp.logical_or(i == 0, be_ref[i] != be_ref[jnp.maximum(i - 1, 0)]))
    def _():
        wgu_bf[...] = wgu_ref[0].astype(BF16)
        wd_bf[...] = wd_ref[0].astype(BF16)

    @pl.when(live)
    def _():
        for part in range(MOE_BLOCK // EXPERT_ROWS):
            rows = slice(part * EXPERT_ROWS, (part + 1) * EXPERT_ROWS)
            x = x_ref[rows, :].astype(BF16)
            gu = jnp.dot(x, wgu_bf[...], preferred_element_type=F32) + bgu_ref[0]
            glu = jnp.minimum(gu[:, :D_FF], SWIGLU_LIMIT)
            lin = jnp.clip(gu[:, D_FF:], -SWIGLU_LIMIT, SWIGLU_LIMIT)
            act = glu * jax.nn.sigmoid(SWIGLU_ALPHA * glu) * (lin + 1.0)
            y_ref[rows, :] = jnp.dot(act.astype(BF16), wd_bf[...], preferred_element_type=F32) + bd_ref[0]

    @pl.when(jnp.logical_not(live))
    def _():
        y_ref[...] = jnp.zeros_like(y_ref)


def _experts(block_e, n_used, xs, wgu, bgu, wd, bd):
    n_blocks = block_e.shape[0]
    x_map = lambda i, be, used: (jnp.minimum(i, used[0] - 1), 0)
    return pl.pallas_call(
        _expert_kernel,
        grid_spec=pltpu.PrefetchScalarGridSpec(
            num_scalar_prefetch=2,
            grid=(n_blocks,),
            in_specs=[pl.BlockSpec((MOE_BLOCK, D_MODEL), x_map),
                      pl.BlockSpec((1, D_MODEL, 2 * D_FF), lambda i, be, used: (be[i], 0, 0)),
                      pl.BlockSpec((1, 1, 2 * D_FF), lambda i, be, used: (be[i], 0, 0)),
                      pl.BlockSpec((1, D_FF, D_MODEL), lambda i, be, used: (be[i], 0, 0)),
                      pl.BlockSpec((1, 1, D_MODEL), lambda i, be, used: (be[i], 0, 0))],
            out_specs=pl.BlockSpec((MOE_BLOCK, D_MODEL), lambda i, be, used: (i, 0)),
            scratch_shapes=[pltpu.VMEM((D_MODEL, 2 * D_FF), BF16),
                            pltpu.VMEM((D_FF, D_MODEL), BF16)]),
        out_shape=jax.ShapeDtypeStruct(xs.shape, F32),
        compiler_params=_cparams(("arbitrary",)),
    )(block_e, n_used, xs, wgu, bgu, wd, bd)


def _combine_kernel(cnt_ref, off_ref, dst_ref, ys_hbm, pos_ref, gate_ref, x1_ref, gt2_ref, g_ref, o_ref,
                    ybuf, sem, *, tile0, n_tiles):
    i = pl.program_id(0)
    buf = i % 2

    def fetch(tile, b):
        def fill_copy(local_row, fill_row, size):
            return pltpu.make_async_copy(ys_hbm.at[pl.ds(fill_row, size)], ybuf.at[b, pl.ds(local_row, size)],
                                         sem.at[b])
        _tile_pieces(cnt_ref, off_ref, dst_ref, tile, ybuf.at[b], ys_hbm, sem.at[b], False, fill_copy)

    @pl.when(i == 0)
    def _():
        fetch(tile0, 0)

    @pl.when(i + 1 < n_tiles)
    def _():
        fetch(tile0 + i + 1, 1 - buf)

    _tile_wait(ybuf.at[buf], ys_hbm, sem.at[buf])

    pos = pos_ref[...]
    gate = gate_ref[...]
    moe = jnp.zeros((TM, D_MODEL), F32)
    for c in range(TILE_ROWS // SORT_CHUNK):
        col = lax.broadcasted_iota(jnp.int32, (TM, SORT_CHUNK), 1) + c * SORT_CHUNK
        w = jnp.zeros((TM, SORT_CHUNK), F32)
        for k in range(TOP_K):
            w = jnp.where(col == pos[:, k:k + 1], gate[:, k:k + 1], w)
        rows = ybuf[buf, c * SORT_CHUNK:(c + 1) * SORT_CHUNK, :]
        moe = moe + jnp.dot(w.astype(BF16), rows.astype(BF16), preferred_element_type=F32)
    o_ref[...] = x1_ref[...] + gt2_ref[0] * _rms(moe, g_ref[...])


def _combine(cnt, off, dst, ys, tile0, pos_tk, gates_tk, x1, ada3, ada_row0, bsz, L, g_post):
    T = bsz * L
    tpb = L // TM
    n_tiles = T // TM
    return pl.pallas_call(
        functools.partial(_combine_kernel, tile0=tile0, n_tiles=n_tiles),
        grid_spec=pltpu.PrefetchScalarGridSpec(
            num_scalar_prefetch=3,
            grid=(n_tiles,),
            in_specs=[pl.BlockSpec(memory_space=pl.ANY),
                      pl.BlockSpec((TM, TOP_K), lambda i, *_: (i, 0)),
                      pl.BlockSpec((TM, TOP_K), lambda i, *_: (i, 0)),
                      pl.BlockSpec((TM, D_MODEL), lambda i, *_: (i, 0)),
                      pl.BlockSpec((1, 1, D_MODEL), lambda i, *_: ((ada_row0 + i // tpb) * 6 + 5, 0, 0)),
                      pl.BlockSpec((1, D_MODEL), lambda i, *_: (0, 0))],
            out_specs=pl.BlockSpec((TM, D_MODEL), lambda i, *_: (i, 0)),
            scratch_shapes=[pltpu.VMEM((2, TILE_ROWS, D_MODEL), F32),
                            pltpu.SemaphoreType.DMA((2,))]),
        out_shape=jax.ShapeDtypeStruct((T, D_MODEL), F32),
        compiler_params=_cparams(("arbitrary",)),
    )(cnt, off, dst, ys, pos_tk, gates_tk, x1, ada3, g_post)


def kernel(x_prompt, x_sample, c_prompt, c_sample, w_ada, b_ada, g_pre_mix, g_post_mix, g_pre_ffn, g_post_ffn,
           w_in, rpb, w_fourier, g_na, g_fn, w_out, w_router, b_router, w_gate_up, b_gate_up, w_down, b_down):
    assert w_ada.shape[0] == 1, "single layer"
    groups = [(x_prompt, c_prompt), (x_sample, c_sample)]
    for x, _ in groups:
        assert x.shape[1] % ATT_TOKENS == 0 and x.shape[1] // GRID_W >= 12 and x.shape[2] == D_MODEL

    row2 = lambda a: a[0].reshape(1, -1)
    w_in0 = w_in[0]
    wq = (w_in0[:, :NA_WIDTH] * (HEAD_DIM ** -0.5 * LOG2_E)).astype(BF16)
    wkt = w_in0[:, NA_WIDTH:2 * NA_WIDTH].T.astype(BF16)
    wv = w_in0[:, 2 * NA_WIDTH:3 * NA_WIDTH].astype(BF16)
    wu = w_in0[:, 3 * NA_WIDTH:].astype(BF16)
    cs = _channel_table()
    bias = _bias_table(rpb[0])
    wf = w_fourier[0].astype(BF16)
    woa = w_out[0, :NA_WIDTH].astype(BF16)
    wof = w_out[0, NA_WIDTH:].astype(BF16)
    wr_t = w_router[0].T
    wr_hi = wr_t.astype(BF16)
    wr_lo = (wr_t - wr_hi.astype(F32)).astype(BF16)
    wr2 = jnp.concatenate([wr_hi, wr_lo], axis=0)
    br = jnp.broadcast_to(b_router[0][:, None], (N_EXPERTS, LANES))
    wgu = w_gate_up[0]
    wd = w_down[0]
    bgu = b_gate_up[0].reshape(N_EXPERTS, 1, 2 * D_FF)
    bd = b_down[0].reshape(N_EXPERTS, 1, D_MODEL)

    c_all = jnp.concatenate([c_prompt, c_sample], axis=0)
    ada3 = _ada(c_all, w_ada[0], b_ada[0]).reshape(c_all.shape[0] * 6, 1, D_MODEL)

    per_group = []
    ada_row0 = 0
    for x, _ in groups:
        bsz, L, _ = x.shape
        x2d = x.reshape(bsz * L, D_MODEL)
        q, kt4, v, u = _inproj(x2d, ada3, ada_row0, bsz, L, row2(g_pre_mix), wq, wkt, wv, wu)
        na = _attention(q.reshape(bsz, L, NA_WIDTH), kt4, v.reshape(bsz, L, NA_WIDTH), bias, row2(g_na))
        fn = _fourier(u, bsz, L, cs, wf, row2(g_fn))
        x1, h2, gate_t, pos_t, cnt = _outproj(
            na.reshape(bsz * L, NA_WIDTH), fn, x2d, ada3, ada_row0, bsz, L,
            row2(g_post_mix), row2(g_pre_ffn), woa, wof, wr2, br)
        per_group.append((x1, h2, gate_t, pos_t, cnt, bsz, L, ada_row0))
        ada_row0 += bsz

    T = sum(g[5] * g[6] for g in per_group)
    max_rows = T * TOP_K + (T // TM) * N_EXPERTS * (SUBLANES - 1) + N_EXPERTS * (MOE_BLOCK - 1)
    n_blocks = -(-max_rows // MOE_BLOCK)
    tile_cnt = jnp.concatenate([g[4][:, 0].reshape(-1, N_EXPERTS) for g in per_group], axis=0).astype(jnp.int32)
    tile_cnt = _round_up(tile_cnt, SUBLANES)
    total = jnp.sum(tile_cnt, axis=0)
    padded = ((total + MOE_BLOCK - 1) // MOE_BLOCK) * MOE_BLOCK
    pend = jnp.cumsum(padded)
    pstart = pend - padded
    run = jnp.cumsum(tile_cnt, axis=0) - tile_cnt
    tile_dst = (pstart[None, :] + run).reshape(-1)
    tile_off = (jnp.cumsum(tile_cnt, axis=1) - tile_cnt).reshape(-1)
    tile_cnt = tile_cnt.reshape(-1)
    block_row0 = jnp.arange(n_blocks, dtype=jnp.int32) * MOE_BLOCK
    block_e = jnp.minimum(jnp.sum((pend[None, :] <= block_row0[:, None]).astype(jnp.int32), axis=1), N_EXPERTS - 1)
    pad_lo = pstart + total
    pad_n = padded - total
    tail = (pend[-1:] // MOE_BLOCK).astype(jnp.int32)

    pos_all = jnp.concatenate([g[3] for g in per_group], axis=1)
    xs = _dispatch(pad_lo, pad_n, tail, tile_cnt, tile_off, tile_dst, pos_all,
                   per_group[0][1], per_group[1][1], n_blocks)
    ys = _experts(block_e, tail, xs, wgu, bgu, wd, bd)

    outs = []
    tile0 = 0
    for x1, _, gate_t, pos_t, _, bsz, L, row0 in per_group:
        out = _combine(tile_cnt, tile_off, tile_dst, ys, tile0, pos_t.T, gate_t.T, x1, ada3, row0, bsz, L,
                       row2(g_post_ffn))
        outs.append(out.reshape(bsz, L, D_MODEL))
        tile0 += bsz * L // TM
    return tuple(outs)
```

```python
import functools

import numpy as np
import jax
import jax.numpy as jnp
from jax import lax
from jax.experimental import pallas as pl
from jax.experimental.pallas import tpu as pltpu

F32 = jnp.float32
BF16 = jnp.bfloat16

D_MODEL = 1024
GRID_W = 64
NA_WIDTH = 512
HEAD_DIM = 64
NA_HEADS = 8
FN_WIDTH = 512
FN_GROUPS = 4
FN_GROUP_DIM = 128
WIN_ROWS = 8
WIN_COLS = 16
N_EXPERTS = 32
TOP_K = 4
D_FF = 1024
SWIGLU_ALPHA = 1.702
SWIGLU_LIMIT = 7.0
MOE_BLOCK = 512
EXPERT_ROWS = 256
EPS = 1e-6

LANES = 128
PAIR_TOKENS = 2 * GRID_W
SLAB_ROWS = 10
SLAB_TOKENS = SLAB_ROWS * GRID_W
N_BIAS_VARIANTS = 5
MASK_VALUE = -1e30
LOG2_E = float(np.log2(np.e))

TM = 512
SUBLANES = 8
SORT_CHUNK = 256
TILE_ROWS = -(-(TM * TOP_K + N_EXPERTS * (SUBLANES - 1)) // SORT_CHUNK) * SORT_CHUNK
FILL_ROWS = TILE_ROWS - TM * TOP_K
ATT_TOKENS = 1024
DFT_COLS = 16
DFT_K1 = 8
ZERO_ROWS = MOE_BLOCK // 2
VMEM_LIMIT = 56 * 1024 * 1024


def _cparams(sem):
    return pltpu.CompilerParams(dimension_semantics=sem, vmem_limit_bytes=VMEM_LIMIT)


def _rms(x, g):
    return x * lax.rsqrt(jnp.mean(x * x, axis=-1, keepdims=True) + EPS) * g


def _pow2_sizes(limit):
    sizes = []
    while limit >= SUBLANES:
        sizes.append(limit)
        limit //= 2
    return sizes


def _round_up(n, m):
    return (n + m - 1) // m * m


def _ada_kernel(c_ref, w_ref, b_ref, o_ref):
    c = c_ref[...]
    s = c * jax.nn.sigmoid(c)
    o_ref[...] = jnp.dot(s.astype(BF16), w_ref[...].astype(BF16), preferred_element_type=F32) + b_ref[...]


def _ada(c_all, w_ada, b_ada):
    nb = c_all.shape[0]
    n_out = w_ada.shape[1]
    tn = 1536
    return pl.pallas_call(
        _ada_kernel,
        grid=(n_out // tn,),
        in_specs=[pl.BlockSpec((nb, D_MODEL), lambda j: (0, 0)),
                  pl.BlockSpec((D_MODEL, tn), lambda j: (0, j)),
                  pl.BlockSpec((1, tn), lambda j: (0, j))],
        out_specs=pl.BlockSpec((nb, tn), lambda j: (0, j)),
        out_shape=jax.ShapeDtypeStruct((nb, n_out), F32),
        compiler_params=_cparams(("arbitrary",)),
    )(c_all, w_ada, b_ada.reshape(1, n_out))


def _inproj_kernel(x_ref, sh_ref, sc_ref, g_ref, wq_ref, wkt_ref, wv_ref, wu_ref,
                   q_ref, kt_ref, v_ref, u_ref):
    h = _rms(x_ref[...], g_ref[...]) * (1.0 + sc_ref[0]) + sh_ref[0]
    hb = h.astype(BF16)
    q_ref[...] = jnp.dot(hb, wq_ref[...], preferred_element_type=F32).astype(BF16)
    v_ref[...] = jnp.dot(hb, wv_ref[...], preferred_element_type=F32).astype(BF16)
    kt = lax.dot_general(wkt_ref[...], hb, (((1,), (1,)), ((), ())), preferred_element_type=F32).astype(BF16)
    for c in range(TM // PAIR_TOKENS):
        kt_ref[0, c] = kt[:, c * PAIR_TOKENS:(c + 1) * PAIR_TOKENS]
    u_ref[...] = jnp.dot(hb, wu_ref[...], preferred_element_type=F32).astype(BF16)


def _inproj(x2d, ada3, ada_row0, bsz, L, g_pre, wq, wkt, wv, wu):
    T = bsz * L
    tpb = L // TM
    row = lambda j: (lambda i: ((ada_row0 + i // tpb) * 6 + j, 0, 0))
    const2 = lambda i: (0, 0)
    return pl.pallas_call(
        _inproj_kernel,
        grid=(T // TM,),
        in_specs=[pl.BlockSpec((TM, D_MODEL), lambda i: (i, 0)),
                  pl.BlockSpec((1, 1, D_MODEL), row(0)),
                  pl.BlockSpec((1, 1, D_MODEL), row(1)),
                  pl.BlockSpec((1, D_MODEL), const2),
                  pl.BlockSpec((D_MODEL, NA_WIDTH), const2),
                  pl.BlockSpec((NA_WIDTH, D_MODEL), const2),
                  pl.BlockSpec((D_MODEL, NA_WIDTH), const2),
                  pl.BlockSpec((D_MODEL, FN_WIDTH), const2)],
        out_specs=[pl.BlockSpec((TM, NA_WIDTH), lambda i: (i, 0)),
                   pl.BlockSpec((1, TM // PAIR_TOKENS, NA_WIDTH, PAIR_TOKENS),
                                lambda i: (i // tpb, i % tpb, 0, 0)),
                   pl.BlockSpec((TM, NA_WIDTH), lambda i: (i, 0)),
                   pl.BlockSpec((TM, FN_WIDTH), lambda i: (i, 0))],
        out_shape=[jax.ShapeDtypeStruct((T, NA_WIDTH), BF16),
                   jax.ShapeDtypeStruct((bsz, L // PAIR_TOKENS, NA_WIDTH, PAIR_TOKENS), BF16),
                   jax.ShapeDtypeStruct((T, NA_WIDTH), BF16),
                   jax.ShapeDtypeStruct((T, FN_WIDTH), BF16)],
        compiler_params=_cparams(("arbitrary",)),
    )(x2d, ada3, ada3, g_pre, wq, wkt, wv, wu)


def _bias_table(rpb):
    rows = 32
    n_pairs = rows // 2
    c = np.arange(GRID_W)
    kc0 = np.clip(c - WIN_COLS // 2, 0, GRID_W - WIN_COLS)
    col_ok = (c[None, :] >= kc0[:, None]) & (c[None, :] < kc0[:, None] + WIN_COLS)
    ci = c[None, :] - c[:, None] + WIN_COLS - 1
    pick = ((ci[None] == np.arange(2 * WIN_COLS - 1)[:, None, None]) & col_ok[None]).astype(np.float32)
    toe = jnp.einsum('hrk,kcd->hrcd', rpb, jnp.asarray(pick), precision=lax.Precision.HIGHEST)
    toe = jnp.where(col_ok, toe * LOG2_E, MASK_VALUE)
    masked = jnp.full((NA_HEADS, GRID_W, GRID_W), MASK_VALUE, F32)
    variants = []
    for j in (0, 1, 2, n_pairs - 2, n_pairs - 1):
        start = int(np.clip(2 * j - 4, 0, rows - SLAB_ROWS))
        halves = []
        for rl in range(2):
            r = 2 * j + rl
            kr0 = int(np.clip(r - WIN_ROWS // 2, 0, rows - WIN_ROWS))
            blocks = []
            for i in range(SLAB_ROWS):
                kr = start + i
                blocks.append(toe[:, kr - r + WIN_ROWS - 1] if kr0 <= kr < kr0 + WIN_ROWS else masked)
            halves.append(jnp.concatenate(blocks, axis=2))
        variants.append(jnp.concatenate(halves, axis=1))
    return jnp.stack(variants)


def _attn_kernel(q_ref, kt_ref, v_ref, bias_ref, g_ref, o_ref, *, n_pairs):
    step = pl.program_id(1)
    pairs_per_step = ATT_TOKENS // PAIR_TOKENS
    lane = lax.broadcasted_iota(jnp.int32, (PAIR_TOKENS, 2 * HEAD_DIM), 1)
    first_head = lane < HEAD_DIM

    def pair_body(p, carry):
        j = step * pairs_per_step + p
        start2 = jnp.clip(j - 2, 0, n_pairs - SLAB_ROWS // 2)
        variant = jnp.where(j < 2, j, jnp.where(j >= n_pairs - 2, j - (n_pairs - 2) + 3, 2))
        qp = q_ref[0, pl.ds(pl.multiple_of(p * PAIR_TOKENS, PAIR_TOKENS), PAIR_TOKENS), :]
        kt5 = kt_ref[0, pl.ds(start2, SLAB_ROWS // 2)]
        vs = v_ref[0, pl.ds(pl.multiple_of(start2 * PAIR_TOKENS, PAIR_TOKENS), SLAB_TOKENS), :]
        head_pairs = range(NA_HEADS // 2)
        dims = [slice(hp * 2 * HEAD_DIM, (hp + 1) * 2 * HEAD_DIM) for hp in head_pairs]
        scores, probs, sums, outs = {}, {}, {}, {}

        def score_phase(hp):
            q2 = qp[:, dims[hp]]
            kt2 = jnp.concatenate([kt5[c, dims[hp], :] for c in range(SLAB_ROWS // 2)], axis=1)
            zero = jnp.zeros_like(q2)
            qm = jnp.concatenate([jnp.where(first_head, q2, zero), jnp.where(first_head, zero, q2)], axis=0)
            scores[hp] = jnp.dot(qm, kt2, preferred_element_type=F32)

        def softmax_phase(hp):
            b2 = jnp.concatenate([bias_ref[variant, hp * 2], bias_ref[variant, hp * 2 + 1]], axis=0)
            s = scores[hp] + b2
            e = jnp.exp2(s - jnp.max(s, axis=-1, keepdims=True))
            sums[hp] = jnp.sum(e, axis=-1, keepdims=True)
            probs[hp] = e.astype(BF16)

        def value_phase(hp):
            o = jnp.dot(probs[hp], vs[:, dims[hp]], preferred_element_type=F32) / sums[hp]
            outs[hp] = jnp.where(first_head, o[:PAIR_TOKENS], o[PAIR_TOKENS:])

        n_hp = NA_HEADS // 2
        for t in range(n_hp + 2):
            if t < n_hp:
                score_phase(t)
            if 0 <= t - 1 < n_hp:
                softmax_phase(t - 1)
            if 0 <= t - 2 < n_hp:
                value_phase(t - 2)
        a = jnp.concatenate([outs[hp] for hp in head_pairs], axis=1)
        o_ref[0, pl.ds(pl.multiple_of(p * PAIR_TOKENS, PAIR_TOKENS), PAIR_TOKENS), :] = _rms(a, g_ref[...]).astype(BF16)
        return carry

    lax.fori_loop(0, pairs_per_step, pair_body, 0)


def _attention(q3, kt4, v3, bias, g_na):
    bsz, L, _ = q3.shape
    n_pairs = L // PAIR_TOKENS
    return pl.pallas_call(
        functools.partial(_attn_kernel, n_pairs=n_pairs),
        grid=(bsz, L // ATT_TOKENS),
        in_specs=[pl.BlockSpec((1, ATT_TOKENS, NA_WIDTH), lambda b, s: (b, s, 0)),
                  pl.BlockSpec((1, n_pairs, NA_WIDTH, PAIR_TOKENS), lambda b, s: (b, 0, 0, 0)),
                  pl.BlockSpec((1, L, NA_WIDTH), lambda b, s: (b, 0, 0)),
                  pl.BlockSpec((N_BIAS_VARIANTS, NA_HEADS, PAIR_TOKENS, SLAB_TOKENS), lambda b, s: (0, 0, 0, 0),
                               pipeline_mode=pl.Buffered(1)),
                  pl.BlockSpec((1, NA_WIDTH), lambda b, s: (0, 0))],
        out_specs=pl.BlockSpec((1, ATT_TOKENS, NA_WIDTH), lambda b, s: (b, s, 0)),
        out_shape=jax.ShapeDtypeStruct((bsz, L, NA_WIDTH), BF16),
        compiler_params=_cparams(("arbitrary", "arbitrary")),
    )(q3, kt4, v3, bias, g_na)


def _angles(j, k, n):
    return ((j * k) % n).astype(F32) * (2.0 * np.pi / n)


def _row_stage_table(R):
    k = jnp.arange(R, dtype=jnp.int32)
    ang = _angles(k[:, None], k[None, :], R)
    f = jnp.concatenate([jnp.cos(ang), -jnp.sin(ang)], axis=0)
    eye = jnp.eye(DFT_COLS, dtype=F32)
    return (f[:, None, :, None] * eye[None, :, None, :]).reshape(2 * R * DFT_COLS, R * DFT_COLS).astype(BF16)


def _col_stage_table(L):
    R = L // GRID_W
    k1 = jnp.arange(R, dtype=jnp.int32)[:, None, None]
    k2 = jnp.arange(GRID_W, dtype=jnp.int32)[None, :, None]
    c = jnp.arange(GRID_W, dtype=jnp.int32)[None, None, :]
    ang = _angles(c, k1 + R * k2, L)
    cc, ss = jnp.cos(ang), jnp.sin(ang)
    top = jnp.concatenate([cc, ss], axis=2)
    bot = jnp.concatenate([-ss, cc], axis=2)
    return jnp.concatenate([top, bot], axis=1).astype(BF16)


def _channel_table():
    j = jnp.arange(FN_GROUP_DIM, dtype=jnp.int32)
    ang = _angles(j[:, None], j[None, :], FN_GROUP_DIM)
    return jnp.concatenate([jnp.cos(ang), jnp.sin(ang)], axis=0).astype(BF16)


def _dft_rows_kernel(f_ref, x_ref, y_ref):
    R = x_ref.shape[1]
    x = x_ref[0].reshape(R * DFT_COLS, FN_WIDTH)
    y = jnp.dot(f_ref[...], x, preferred_element_type=F32).astype(BF16)
    y_ref[0] = y.reshape(2 * R, DFT_COLS, FN_WIDTH)


def _dft_rows(f_rows, u4):
    bsz, R, _, _ = u4.shape
    return pl.pallas_call(
        _dft_rows_kernel,
        grid=(bsz, GRID_W // DFT_COLS),
        in_specs=[pl.BlockSpec((2 * R * DFT_COLS, R * DFT_COLS), lambda b, t: (0, 0)),
                  pl.BlockSpec((1, R, DFT_COLS, FN_WIDTH), lambda b, t: (b, 0, t, 0))],
        out_specs=pl.BlockSpec((1, 2 * R, DFT_COLS, FN_WIDTH), lambda b, t: (b, 0, t, 0)),
        out_shape=jax.ShapeDtypeStruct((bsz, 2 * R, GRID_W, FN_WIDTH), BF16),
        compiler_params=_cparams(("arbitrary", "arbitrary")),
    )(f_rows, u4)


def _dft_cols_kernel(yr_ref, yi_ref, m_ref, cs_ref, wf_ref, g_ref, o_ref):
    re, im = [], []
    for j in range(DFT_K1):
        y = jnp.concatenate([yr_ref[0, j], yi_ref[0, j]], axis=0)
        x = jnp.dot(m_ref[j], y, preferred_element_type=F32)
        re.append(x[:GRID_W])
        im.append(x[GRID_W:])
    a = jnp.concatenate(re, axis=0).astype(BF16)
    b = jnp.concatenate(im, axis=0).astype(BF16)
    ys = []
    for g in range(FN_GROUPS):
        sl = slice(g * FN_GROUP_DIM, (g + 1) * FN_GROUP_DIM)
        ab = jnp.concatenate([a[:, sl], b[:, sl]], axis=1)
        z = jnp.dot(ab, cs_ref[...], preferred_element_type=F32).astype(BF16)
        ys.append(jnp.dot(z, wf_ref[g], preferred_element_type=F32))
    out = _rms(jnp.concatenate(ys, axis=1), g_ref[...]).astype(BF16)
    for j in range(DFT_K1):
        o_ref[0, :, j * FN_WIDTH:(j + 1) * FN_WIDTH] = out[j * GRID_W:(j + 1) * GRID_W, :]


def _dft_cols(y4, m_cols, cs, wf, g_fn):
    bsz, R2, _, _ = y4.shape
    R = R2 // 2
    steps = R // DFT_K1
    return pl.pallas_call(
        _dft_cols_kernel,
        grid=(bsz, steps),
        in_specs=[pl.BlockSpec((1, DFT_K1, GRID_W, FN_WIDTH), lambda b, t: (b, t, 0, 0)),
                  pl.BlockSpec((1, DFT_K1, GRID_W, FN_WIDTH), lambda b, t: (b, steps + t, 0, 0)),
                  pl.BlockSpec((DFT_K1, 2 * GRID_W, 2 * GRID_W), lambda b, t: (t, 0, 0)),
                  pl.BlockSpec((2 * FN_GROUP_DIM, FN_GROUP_DIM), lambda b, t: (0, 0)),
                  pl.BlockSpec((FN_GROUPS, FN_GROUP_DIM, FN_GROUP_DIM), lambda b, t: (0, 0, 0)),
                  pl.BlockSpec((1, FN_WIDTH), lambda b, t: (0, 0))],
        out_specs=pl.BlockSpec((1, GRID_W, DFT_K1 * FN_WIDTH), lambda b, t: (b, 0, t)),
        out_shape=jax.ShapeDtypeStruct((bsz, GRID_W, R * FN_WIDTH), BF16),
        compiler_params=_cparams(("arbitrary", "arbitrary")),
    )(y4, y4, m_cols, cs, wf, g_fn)


def _fourier(u, bsz, L, cs, wf, g_fn):
    R = L // GRID_W
    y = _dft_rows(_row_stage_table(R), u.reshape(bsz, R, GRID_W, FN_WIDTH))
    fn = _dft_cols(y, _col_stage_table(L), cs, wf, g_fn)
    return fn.reshape(bsz * L, FN_WIDTH)


def _outproj_kernel(na_ref, fn_ref, x_ref, gt1_ref, sh2_ref, sc2_ref, gpost_ref, gpre_ref,
                    woa_ref, wof_ref, wr_ref, br_ref,
                    x1_ref, h2_ref, gate_ref, pos_ref, cnt_ref):
    mix = (jnp.dot(na_ref[...], woa_ref[...], preferred_element_type=F32)
           + jnp.dot(fn_ref[...], wof_ref[...], preferred_element_type=F32))
    x1 = x_ref[...] + gt1_ref[0] * _rms(mix, gpost_ref[...])
    x1_ref[...] = x1
    h2 = _rms(x1, gpre_ref[...]) * (1.0 + sc2_ref[0]) + sh2_ref[0]
    hi = h2.astype(BF16)
    h2_ref[...] = hi

    lo = (h2 - hi.astype(F32)).astype(BF16)
    nt = (((1,), (1,)), ((), ()))
    a = lax.dot_general(wr_ref[...], hi, nt, preferred_element_type=F32)
    b = lax.dot_general(wr_ref[0:N_EXPERTS, :], lo, nt, preferred_element_type=F32)
    logits = a[:N_EXPERTS] + a[N_EXPERTS:] + b + br_ref[:, 0:1]

    eidx = lax.broadcasted_iota(jnp.int32, (N_EXPERTS, TM), 0)
    vals, onehots = [], []
    cur = logits
    for k in range(TOP_K):
        m = jnp.max(cur, axis=0, keepdims=True)
        idx = jnp.min(jnp.where(cur == m, eidx, N_EXPERTS), axis=0, keepdims=True)
        hit = eidx == idx
        vals.append(m)
        onehots.append(hit)
        cur = jnp.where(hit, -jnp.inf, cur)
    exps = [jnp.exp(v - vals[0]) for v in vals]
    den = exps[0] + exps[1] + exps[2] + exps[3]
    for k in range(TOP_K):
        gate_ref[k:k + 1, :] = exps[k] / den

    oh = [h.astype(F32) for h in onehots]
    oh_sum = oh[0] + oh[1] + oh[2] + oh[3]
    s_i = lax.broadcasted_iota(jnp.int32, (TM, TM), 0)
    t_i = lax.broadcasted_iota(jnp.int32, (TM, TM), 1)
    earlier = jnp.where(s_i < t_i, 1.0, 0.0).astype(BF16)
    same_expert_before = jnp.dot(oh_sum.astype(BF16), earlier, preferred_element_type=F32)
    counts = jnp.broadcast_to(jnp.sum(oh_sum, axis=1, keepdims=True), (N_EXPERTS, LANES))
    cnt_ref[...] = counts
    run_tiles = jnp.floor((counts + (SUBLANES - 1)) * (1.0 / SUBLANES))
    e_r = lax.broadcasted_iota(jnp.int32, (N_EXPERTS, N_EXPERTS), 0)
    e_c = lax.broadcasted_iota(jnp.int32, (N_EXPERTS, N_EXPERTS), 1)
    lower = jnp.where(e_c < e_r, 1.0, 0.0).astype(BF16)
    run_start = SUBLANES * jnp.dot(lower, run_tiles.astype(BF16), preferred_element_type=F32)[:, 0:1]
    base = run_start + same_expert_before
    for k in range(TOP_K):
        pos_ref[k:k + 1, :] = jnp.sum(oh[k] * base, axis=0, keepdims=True).astype(jnp.int32)


def _outproj(na2d, fn2d, x2d, ada3, ada_row0, bsz, L, g_post, g_pre, woa, wof, wr2, br):
    T = bsz * L
    tpb = L // TM
    row = lambda j: (lambda i: ((ada_row0 + i // tpb) * 6 + j, 0, 0))
    const2 = lambda i: (0, 0)
    tok = lambda i: (i, 0)
    lanes = lambda i: (0, i)
    return pl.pallas_call(
        _outproj_kernel,
        grid=(T // TM,),
        in_specs=[pl.BlockSpec((TM, NA_WIDTH), tok),
                  pl.BlockSpec((TM, FN_WIDTH), tok),
                  pl.BlockSpec((TM, D_MODEL), tok),
                  pl.BlockSpec((1, 1, D_MODEL), row(2)),
                  pl.BlockSpec((1, 1, D_MODEL), row(3)),
                  pl.BlockSpec((1, 1, D_MODEL), row(4)),
                  pl.BlockSpec((1, D_MODEL), const2),
                  pl.BlockSpec((1, D_MODEL), const2),
                  pl.BlockSpec((NA_WIDTH, D_MODEL), const2),
                  pl.BlockSpec((FN_WIDTH, D_MODEL), const2),
                  pl.BlockSpec((2 * N_EXPERTS, D_MODEL), const2),
                  pl.BlockSpec((N_EXPERTS, LANES), const2)],
        out_specs=[pl.BlockSpec((TM, D_MODEL), tok),
                   pl.BlockSpec((TM, D_MODEL), tok),
                   pl.BlockSpec((TOP_K, TM), lanes),
                   pl.BlockSpec((TOP_K, TM), lanes),
                   pl.BlockSpec((N_EXPERTS, LANES), tok)],
        out_shape=[jax.ShapeDtypeStruct((T, D_MODEL), F32),
                   jax.ShapeDtypeStruct((T, D_MODEL), BF16),
                   jax.ShapeDtypeStruct((TOP_K, T), F32),
                   jax.ShapeDtypeStruct((TOP_K, T), jnp.int32),
                   jax.ShapeDtypeStruct((T // TM * N_EXPERTS, LANES), F32)],
        compiler_params=_cparams(("arbitrary",)),
    )(na2d, fn2d, x2d, ada3, ada3, ada3, g_post, g_pre, woa, wof, wr2, br)


def _zero_fill(pad_lo_ref, pad_n_ref, tail_ref, xs_hbm, zbuf, zsem, n_blocks, wait):
    def run(cp):
        cp.wait() if wait else cp.start()

    def expert(e, c):
        pos = pad_lo_ref[e]
        n = pad_n_ref[e]
        for size in _pow2_sizes(ZERO_ROWS):
            @pl.when((n & size) != 0)
            def _(pos=pos, size=size):
                dst = xs_hbm.at[pl.ds(pl.multiple_of(pos, SUBLANES), size)]
                run(pltpu.make_async_copy(zbuf.at[pl.ds(0, size)], dst, zsem))
            pos = pos + (n & size)
        return c

    lax.fori_loop(0, N_EXPERTS, expert, 0)

    def tail(blk, c):
        for half in range(MOE_BLOCK // ZERO_ROWS):
            row0 = pl.multiple_of(blk * MOE_BLOCK + half * ZERO_ROWS, ZERO_ROWS)
            run(pltpu.make_async_copy(zbuf, xs_hbm.at[pl.ds(row0, ZERO_ROWS)], zsem))
        return c

    lax.fori_loop(tail_ref[0], n_blocks, tail, 0)


def _tile_pieces(cnt_ref, off_ref, dst_ref, tile, local_buf, sorted_hbm, sem, to_sorted, fill_copy):
    def expert(e, c):
        j = tile * N_EXPERTS + e
        n = cnt_ref[j]
        src = off_ref[j]
        dst = dst_ref[j]
        for size in _pow2_sizes(TM):
            @pl.when((n & size) != 0)
            def _(src=src, dst=dst, size=size):
                loc = local_buf.at[pl.ds(pl.multiple_of(src, SUBLANES), size)]
                glob = sorted_hbm.at[pl.ds(pl.multiple_of(dst, SUBLANES), size)]
                cp = pltpu.make_async_copy(loc, glob, sem) if to_sorted else pltpu.make_async_copy(glob, loc, sem)
                cp.start()
            src = src + (n & size)
            dst = dst + (n & size)
        return c

    lax.fori_loop(0, N_EXPERTS, expert, 0)

    last = tile * N_EXPERTS + N_EXPERTS - 1
    moved = off_ref[last] + cnt_ref[last]
    fill = TILE_ROWS - moved
    done = 0
    for size in _pow2_sizes(FILL_ROWS):
        @pl.when((fill & size) != 0)
        def _(done=done, size=size):
            fill_copy(pl.multiple_of(moved + done, SUBLANES), pl.multiple_of(done, SUBLANES), size).start()
        done = done + (fill & size)


def _tile_wait(local_buf, sorted_hbm, sem):
    for c in range(TILE_ROWS // SORT_CHUNK):
        rows = pl.ds(c * SORT_CHUNK, SORT_CHUNK)
        pltpu.make_async_copy(local_buf.at[rows], sorted_hbm.at[rows], sem).wait()


def _dispatch_kernel(pad_lo_ref, pad_n_ref, tail_ref, cnt_ref, off_ref, dst_ref,
                     pos_ref, h2a_ref, h2b_ref, xs_hbm, spare_hbm, sbuf, zbuf, sem, zsem,
                     *, n_blocks, tiles_a, n_tiles):
    i = pl.program_id(0)
    buf = i % 2

    @pl.when(i == 0)
    def _():
        zbuf[...] = jnp.zeros_like(zbuf)
        _zero_fill(pad_lo_ref, pad_n_ref, tail_ref, xs_hbm, zbuf, zsem, n_blocks, wait=False)
        _zero_fill(pad_lo_ref, pad_n_ref, tail_ref, xs_hbm, zbuf, zsem, n_blocks, wait=True)
        for b in range(2):
            pltpu.make_async_copy(zbuf.at[pl.ds(0, FILL_ROWS)], spare_hbm.at[b], zsem).start()
        for b in range(2):
            pltpu.make_async_copy(zbuf.at[pl.ds(0, FILL_ROWS)], spare_hbm.at[b], zsem).wait()

    def fill_copy(local_row, fill_row, size):
        del local_row
        return pltpu.make_async_copy(zbuf.at[pl.ds(0, size)], spare_hbm.at[buf, pl.ds(fill_row, size)], sem.at[buf])

    @pl.when(i >= 2)
    def _():
        _tile_wait(sbuf.at[buf], xs_hbm, sem.at[buf])

    def sort_tile(h2_ref):
        h = h2_ref[...]
        pos = pos_ref[...]
        for c in range(TILE_ROWS // SORT_CHUNK):
            row = lax.broadcasted_iota(jnp.int32, (SORT_CHUNK, TM), 0) + c * SORT_CHUNK
            sel = jnp.zeros((SORT_CHUNK, TM), F32)
            for k in range(TOP_K):
                sel = jnp.where(row == pos[k:k + 1, :], 1.0, sel)
            sbuf[buf, c * SORT_CHUNK:(c + 1) * SORT_CHUNK, :] = jnp.dot(sel.astype(BF16), h,
                                                                        preferred_element_type=F32)

    @pl.when(i < tiles_a)
    def _():
        sort_tile(h2a_ref)

    @pl.when(i >= tiles_a)
    def _():
        sort_tile(h2b_ref)

    _tile_pieces(cnt_ref, off_ref, dst_ref, i, sbuf.at[buf], xs_hbm, sem.at[buf], True, fill_copy)

    @pl.when(i == n_tiles - 1)
    def _():
        if n_tiles >= 2:
            _tile_wait(sbuf.at[1 - buf], xs_hbm, sem.at[1 - buf])
        _tile_wait(sbuf.at[buf], xs_hbm, sem.at[buf])


def _dispatch(pad_lo, pad_n, tail, cnt, off, dst, pos, h2a, h2b, n_blocks):
    tiles_a = h2a.shape[0] // TM
    tiles_b = h2b.shape[0] // TM
    n_tiles = tiles_a + tiles_b
    P = n_blocks * MOE_BLOCK
    kern = functools.partial(_dispatch_kernel, n_blocks=n_blocks, tiles_a=tiles_a, n_tiles=n_tiles)
    return pl.pallas_call(
        kern,
        grid_spec=pltpu.PrefetchScalarGridSpec(
            num_scalar_prefetch=6,
            grid=(n_tiles,),
            in_specs=[pl.BlockSpec((TOP_K, TM), lambda i, *_: (0, i)),
                      pl.BlockSpec((TM, D_MODEL), lambda i, *_: (jnp.minimum(i, tiles_a - 1), 0)),
                      pl.BlockSpec((TM, D_MODEL), lambda i, *_: (jnp.maximum(i - tiles_a, 0), 0))],
            out_specs=[pl.BlockSpec(memory_space=pl.ANY), pl.BlockSpec(memory_space=pl.ANY)],
            scratch_shapes=[pltpu.VMEM((2, TILE_ROWS, D_MODEL), F32),
                            pltpu.VMEM((ZERO_ROWS, D_MODEL), F32),
                            pltpu.SemaphoreType.DMA((2,)),
                            pltpu.SemaphoreType.DMA]),
        out_shape=[jax.ShapeDtypeStruct((P, D_MODEL), F32),
                   jax.ShapeDtypeStruct((2, FILL_ROWS, D_MODEL), F32)],
        compiler_params=_cparams(("arbitrary",)),
    )(pad_lo, pad_n, tail, cnt, off, dst, pos, h2a, h2b)[0]


def _expert_kernel(be_ref, used_ref, x_ref, wgu_ref, bgu_ref, wd_ref, bd_ref, y_ref, wgu_bf, wd_bf):
    i = pl.program_id(0)
    live = i < used_ref[0]

    @pl.when(jnp.logical_or(i == 0, be_ref[i] != be_ref[jnp.maximum(i - 1, 0)]))
    def _():
        wgu_bf[...] = wgu_ref[0].astype(BF16)
        wd_bf[...] = wd_ref[0].astype(BF16)

    @pl.when(live)
    def _():
        for part in range(MOE_BLOCK // EXPERT_ROWS):
            rows = slice(part * EXPERT_ROWS, (part + 1) * EXPERT_ROWS)
            x = x_ref[rows, :].astype(BF16)
            gu = jnp.dot(x, wgu_bf[...], preferred_element_type=F32) + bgu_ref[0]
            glu = jnp.minimum(gu[:, :D_FF], SWIGLU_LIMIT)
            lin = jnp.clip(gu[:, D_FF:], -SWIGLU_LIMIT, SWIGLU_LIMIT)
            act = glu * jax.nn.sigmoid(SWIGLU_ALPHA * glu) * (lin + 1.0)
            y_ref[rows, :] = jnp.dot(act.astype(BF16), wd_bf[...], preferred_element_type=F32) + bd_ref[0]

    @pl.when(jnp.logical_not(live))
    def _():
        y_ref[...] = jnp.zeros_like(y_ref)


def _experts(block_e, n_used, xs, wgu, bgu, wd, bd):
    n_blocks = block_e.shape[0]
    x_map = lambda i, be, used: (jnp.minimum(i, used[0] - 1), 0)
    return pl.pallas_call(
        _expert_kernel,
        grid_spec=pltpu.PrefetchScalarGridSpec(
            num_scalar_prefetch=2,
            grid=(n_blocks,),
            in_specs=[pl.BlockSpec((MOE_BLOCK, D_MODEL), x_map),
                      pl.BlockSpec((1, D_MODEL, 2 * D_FF), lambda i, be, used: (be[i], 0, 0)),
                      pl.BlockSpec((1, 1, 2 * D_FF), lambda i, be, used: (be[i], 0, 0)),
                      pl.BlockSpec((1, D_FF, D_MODEL), lambda i, be, used: (be[i], 0, 0)),
                      pl.BlockSpec((1, 1, D_MODEL), lambda i, be, used: (be[i], 0, 0))],
            out_specs=pl.BlockSpec((MOE_BLOCK, D_MODEL), lambda i, be, used: (i, 0)),
            scratch_shapes=[pltpu.VMEM((D_MODEL, 2 * D_FF), BF16),
                            pltpu.VMEM((D_FF, D_MODEL), BF16)]),
        out_shape=jax.ShapeDtypeStruct(xs.shape, F32),
        compiler_params=_cparams(("arbitrary",)),
    )(block_e, n_used, xs, wgu, bgu, wd, bd)


def _combine_kernel(cnt_ref, off_ref, dst_ref, ys_hbm, pos_ref, gate_ref, x1_ref, gt2_ref, g_ref, o_ref,
                    ybuf, sem, *, tile0, n_tiles):
    i = pl.program_id(0)
    buf = i % 2

    def fetch(tile, b):
        def fill_copy(local_row, fill_row, size):
            return pltpu.make_async_copy(ys_hbm.at[pl.ds(fill_row, size)], ybuf.at[b, pl.ds(local_row, size)],
                                         sem.at[b])
        _tile_pieces(cnt_ref, off_ref, dst_ref, tile, ybuf.at[b], ys_hbm, sem.at[b], False, fill_copy)

    @pl.when(i == 0)
    def _():
        fetch(tile0, 0)

    @pl.when(i + 1 < n_tiles)
    def _():
        fetch(tile0 + i + 1, 1 - buf)

    _tile_wait(ybuf.at[buf], ys_hbm, sem.at[buf])

    pos = pos_ref[...]
    gate = gate_ref[...]
    moe = jnp.zeros((TM, D_MODEL), F32)
    for c in range(TILE_ROWS // SORT_CHUNK):
        col = lax.broadcasted_iota(jnp.int32, (TM, SORT_CHUNK), 1) + c * SORT_CHUNK
        w = jnp.zeros((TM, SORT_CHUNK), F32)
        for k in range(TOP_K):
            w = jnp.where(col == pos[:, k:k + 1], gate[:, k:k + 1], w)
        rows = ybuf[buf, c * SORT_CHUNK:(c + 1) * SORT_CHUNK, :]
        moe = moe + jnp.dot(w.astype(BF16), rows.astype(BF16), preferred_element_type=F32)
    o_ref[...] = x1_ref[...] + gt2_ref[0] * _rms(moe, g_ref[...])


def _combine(cnt, off, dst, ys, tile0, pos_tk, gates_tk, x1, ada3, ada_row0, bsz, L, g_post):
    T = bsz * L
    tpb = L // TM
    n_tiles = T // TM
    return pl.pallas_call(
        functools.partial(_combine_kernel, tile0=tile0, n_tiles=n_tiles),
        grid_spec=pltpu.PrefetchScalarGridSpec(
            num_scalar_prefetch=3,
            grid=(n_tiles,),
            in_specs=[pl.BlockSpec(memory_space=pl.ANY),
                      pl.BlockSpec((TM, TOP_K), lambda i, *_: (i, 0)),
                      pl.BlockSpec((TM, TOP_K), lambda i, *_: (i, 0)),
                      pl.BlockSpec((TM, D_MODEL), lambda i, *_: (i, 0)),
                      pl.BlockSpec((1, 1, D_MODEL), lambda i, *_: ((ada_row0 + i // tpb) * 6 + 5, 0, 0)),
                      pl.BlockSpec((1, D_MODEL), lambda i, *_: (0, 0))],
            out_specs=pl.BlockSpec((TM, D_MODEL), lambda i, *_: (i, 0)),
            scratch_shapes=[pltpu.VMEM((2, TILE_ROWS, D_MODEL), F32),
                            pltpu.SemaphoreType.DMA((2,))]),
        out_shape=jax.ShapeDtypeStruct((T, D_MODEL), F32),
        compiler_params=_cparams(("arbitrary",)),
    )(cnt, off, dst, ys, pos_tk, gates_tk, x1, ada3, g_post)


def kernel(x_prompt, x_sample, c_prompt, c_sample, w_ada, b_ada, g_pre_mix, g_post_mix, g_pre_ffn, g_post_ffn,
           w_in, rpb, w_fourier, g_na, g_fn, w_out, w_router, b_router, w_gate_up, b_gate_up, w_down, b_down):
    assert w_ada.shape[0] == 1, "single layer"
    groups = [(x_prompt, c_prompt), (x_sample, c_sample)]
    for x, _ in groups:
        assert x.shape[1] % ATT_TOKENS == 0 and x.shape[1] // GRID_W >= 12 and x.shape[2] == D_MODEL

    row2 = lambda a: a[0].reshape(1, -1)
    w_in0 = w_in[0]
    wq = (w_in0[:, :NA_WIDTH] * (HEAD_DIM ** -0.5 * LOG2_E)).astype(BF16)
    wkt = w_in0[:, NA_WIDTH:2 * NA_WIDTH].T.astype(BF16)
    wv = w_in0[:, 2 * NA_WIDTH:3 * NA_WIDTH].astype(BF16)
    wu = w_in0[:, 3 * NA_WIDTH:].astype(BF16)
    cs = _channel_table()
    bias = _bias_table(rpb[0])
    wf = w_fourier[0].astype(BF16)
    woa = w_out[0, :NA_WIDTH].astype(BF16)
    wof = w_out[0, NA_WIDTH:].astype(BF16)
    wr_t = w_router[0].T
    wr_hi = wr_t.astype(BF16)
    wr_lo = (wr_t - wr_hi.astype(F32)).astype(BF16)
    wr2 = jnp.concatenate([wr_hi, wr_lo], axis=0)
    br = jnp.broadcast_to(b_router[0][:, None], (N_EXPERTS, LANES))
    wgu = w_gate_up[0]
    wd = w_down[0]
    bgu = b_gate_up[0].reshape(N_EXPERTS, 1, 2 * D_FF)
    bd = b_down[0].reshape(N_EXPERTS, 1, D_MODEL)

    c_all = jnp.concatenate([c_prompt, c_sample], axis=0)
    ada3 = _ada(c_all, w_ada[0], b_ada[0]).reshape(c_all.shape[0] * 6, 1, D_MODEL)

    per_group = []
    ada_row0 = 0
    for x, _ in groups:
        bsz, L, _ = x.shape
        x2d = x.reshape(bsz * L, D_MODEL)
        q, kt4, v, u = _inproj(x2d, ada3, ada_row0, bsz, L, row2(g_pre_mix), wq, wkt, wv, wu)
        na = _attention(q.reshape(bsz, L, NA_WIDTH), kt4, v.reshape(bsz, L, NA_WIDTH), bias, row2(g_na))
        fn = _fourier(u, bsz, L, cs, wf, row2(g_fn))
        x1, h2, gate_t, pos_t, cnt = _outproj(
            na.reshape(bsz * L, NA_WIDTH), fn, x2d, ada3, ada_row0, bsz, L,
            row2(g_post_mix), row2(g_pre_ffn), woa, wof, wr2, br)
        per_group.append((x1, h2, gate_t, pos_t, cnt, bsz, L, ada_row0))
        ada_row0 += bsz

    T = sum(g[5] * g[6] for g in per_group)
    max_rows = T * TOP_K + (T // TM) * N_EXPERTS * (SUBLANES - 1) + N_EXPERTS * (MOE_BLOCK - 1)
    n_blocks = -(-max_rows // MOE_BLOCK)
    tile_cnt = jnp.concatenate([g[4][:, 0].reshape(-1, N_EXPERTS) for g in per_group], axis=0).astype(jnp.int32)
    tile_cnt = _round_up(tile_cnt, SUBLANES)
    total = jnp.sum(tile_cnt, axis=0)
    padded = ((total + MOE_BLOCK - 1) // MOE_BLOCK) * MOE_BLOCK
    pend = jnp.cumsum(padded)
    pstart = pend - padded
    run = jnp.cumsum(tile_cnt, axis=0) - tile_cnt
    tile_dst = (pstart[None, :] + run).reshape(-1)
    tile_off = (jnp.cumsum(tile_cnt, axis=1) - tile_cnt).reshape(-1)
    tile_cnt = tile_cnt.reshape(-1)
    block_row0 = jnp.arange(n_blocks, dtype=jnp.int32) * MOE_BLOCK
    block_e = jnp.minimum(jnp.sum((pend[None, :] <= block_row0[:, None]).astype(jnp.int32), axis=1), N_EXPERTS - 1)
    pad_lo = pstart + total
    pad_n = padded - total
    tail = (pend[-1:] // MOE_BLOCK).astype(jnp.int32)

    pos_all = jnp.concatenate([g[3] for g in per_group], axis=1)
    xs = _dispatch(pad_lo, pad_n, tail, tile_cnt, tile_off, tile_dst, pos_all,
                   per_group[0][1], per_group[1][1], n_blocks)
    ys = _experts(block_e, tail, xs, wgu, bgu, wd, bd)

    outs = []
    tile0 = 0
    for x1, _, gate_t, pos_t, _, bsz, L, row0 in per_group:
        out = _combine(tile_cnt, tile_off, tile_dst, ys, tile0, pos_t.T, gate_t.T, x1, ada3, row0, bsz, L,
                       row2(g_post_ffn))
        outs.append(out.reshape(bsz, L, D_MODEL))
        tile0 += bsz * L // TM
    return tuple(outs)
```

```python
import functools

import numpy as np
import jax
import jax.numpy as jnp
from jax import lax
from jax.experimental import pallas as pl
from jax.experimental.pallas import tpu as pltpu

F32 = jnp.float32
BF16 = jnp.bfloat16

D_MODEL = 1024
GRID_W = 64
NA_WIDTH = 512
HEAD_DIM = 64
NA_HEADS = 8
FN_WIDTH = 512
FN_GROUPS = 4
FN_GROUP_DIM = 128
WIN_ROWS = 8
WIN_COLS = 16
N_EXPERTS = 32
TOP_K = 4
D_FF = 1024
SWIGLU_ALPHA = 1.702
SWIGLU_LIMIT = 7.0
MOE_BLOCK = 512
EXPERT_ROWS = 256
EPS = 1e-6

LANES = 128
PAIR_TOKENS = 2 * GRID_W
SLAB_ROWS = 10
SLAB_TOKENS = SLAB_ROWS * GRID_W
N_BIAS_VARIANTS = 5
MASK_VALUE = -1e30
LOG2_E = float(np.log2(np.e))

TM = 512
SUBLANES = 8
SORT_CHUNK = 256
TILE_ROWS = -(-(TM * TOP_K + N_EXPERTS * (SUBLANES - 1)) // SORT_CHUNK) * SORT_CHUNK
FILL_ROWS = TILE_ROWS - TM * TOP_K
ATT_TOKENS = 1024
ATT_PAIRS = 2
DFT_COLS = 16
DFT_K1 = 8
ZERO_ROWS = MOE_BLOCK // 2
VMEM_LIMIT = 56 * 1024 * 1024


def _cparams(sem):
    return pltpu.CompilerParams(dimension_semantics=sem, vmem_limit_bytes=VMEM_LIMIT)


def _rms(x, g):
    return x * lax.rsqrt(jnp.mean(x * x, axis=-1, keepdims=True) + EPS) * g


def _pow2_sizes(limit):
    sizes = []
    while limit >= SUBLANES:
        sizes.append(limit)
        limit //= 2
    return sizes


def _round_up(n, m):
    return (n + m - 1) // m * m


def _ada_kernel(c_ref, w_ref, b_ref, o_ref):
    c = c_ref[...]
    s = c * jax.nn.sigmoid(c)
    o_ref[...] = jnp.dot(s.astype(BF16), w_ref[...].astype(BF16), preferred_element_type=F32) + b_ref[...]


def _ada(c_all, w_ada, b_ada):
    nb = c_all.shape[0]
    n_out = w_ada.shape[1]
    tn = 1536
    return pl.pallas_call(
        _ada_kernel,
        grid=(n_out // tn,),
        in_specs=[pl.BlockSpec((nb, D_MODEL), lambda j: (0, 0)),
                  pl.BlockSpec((D_MODEL, tn), lambda j: (0, j)),
                  pl.BlockSpec((1, tn), lambda j: (0, j))],
        out_specs=pl.BlockSpec((nb, tn), lambda j: (0, j)),
        out_shape=jax.ShapeDtypeStruct((nb, n_out), F32),
        compiler_params=_cparams(("arbitrary",)),
    )(c_all, w_ada, b_ada.reshape(1, n_out))


def _inproj_kernel(x_ref, sh_ref, sc_ref, g_ref, wq_ref, wkt_ref, wv_ref, wu_ref,
                   q_ref, kt_ref, v_ref, u_ref):
    h = _rms(x_ref[...], g_ref[...]) * (1.0 + sc_ref[0]) + sh_ref[0]
    hb = h.astype(BF16)
    q_ref[...] = jnp.dot(hb, wq_ref[...], preferred_element_type=F32).astype(BF16)
    v_ref[...] = jnp.dot(hb, wv_ref[...], preferred_element_type=F32).astype(BF16)
    kt = lax.dot_general(wkt_ref[...], hb, (((1,), (1,)), ((), ())), preferred_element_type=F32).astype(BF16)
    for c in range(TM // PAIR_TOKENS):
        kt_ref[0, c] = kt[:, c * PAIR_TOKENS:(c + 1) * PAIR_TOKENS]
    u_ref[...] = jnp.dot(hb, wu_ref[...], preferred_element_type=F32).astype(BF16)


def _inproj(x2d, ada3, ada_row0, bsz, L, g_pre, wq, wkt, wv, wu):
    T = bsz * L
    tpb = L // TM
    row = lambda j: (lambda i: ((ada_row0 + i // tpb) * 6 + j, 0, 0))
    const2 = lambda i: (0, 0)
    return pl.pallas_call(
        _inproj_kernel,
        grid=(T // TM,),
        in_specs=[pl.BlockSpec((TM, D_MODEL), lambda i: (i, 0)),
                  pl.BlockSpec((1, 1, D_MODEL), row(0)),
                  pl.BlockSpec((1, 1, D_MODEL), row(1)),
                  pl.BlockSpec((1, D_MODEL), const2),
                  pl.BlockSpec((D_MODEL, NA_WIDTH), const2),
                  pl.BlockSpec((NA_WIDTH, D_MODEL), const2),
                  pl.BlockSpec((D_MODEL, NA_WIDTH), const2),
                  pl.BlockSpec((D_MODEL, FN_WIDTH), const2)],
        out_specs=[pl.BlockSpec((TM, NA_WIDTH), lambda i: (i, 0)),
                   pl.BlockSpec((1, TM // PAIR_TOKENS, NA_WIDTH, PAIR_TOKENS),
                                lambda i: (i // tpb, i % tpb, 0, 0)),
                   pl.BlockSpec((TM, NA_WIDTH), lambda i: (i, 0)),
                   pl.BlockSpec((TM, FN_WIDTH), lambda i: (i, 0))],
        out_shape=[jax.ShapeDtypeStruct((T, NA_WIDTH), BF16),
                   jax.ShapeDtypeStruct((bsz, L // PAIR_TOKENS, NA_WIDTH, PAIR_TOKENS), BF16),
                   jax.ShapeDtypeStruct((T, NA_WIDTH), BF16),
                   jax.ShapeDtypeStruct((T, FN_WIDTH), BF16)],
        compiler_params=_cparams(("arbitrary",)),
    )(x2d, ada3, ada3, g_pre, wq, wkt, wv, wu)


def _bias_table(rpb):
    rows = 32
    n_pairs = rows // 2
    c = np.arange(GRID_W)
    kc0 = np.clip(c - WIN_COLS // 2, 0, GRID_W - WIN_COLS)
    col_ok = (c[None, :] >= kc0[:, None]) & (c[None, :] < kc0[:, None] + WIN_COLS)
    ci = c[None, :] - c[:, None] + WIN_COLS - 1
    pick = ((ci[None] == np.arange(2 * WIN_COLS - 1)[:, None, None]) & col_ok[None]).astype(np.float32)
    toe = jnp.einsum('hrk,kcd->hrcd', rpb, jnp.asarray(pick), precision=lax.Precision.HIGHEST)
    toe = jnp.where(col_ok, toe * LOG2_E, MASK_VALUE)
    masked = jnp.full((NA_HEADS, GRID_W, GRID_W), MASK_VALUE, F32)
    variants = []
    for j in (0, 1, 2, n_pairs - 2, n_pairs - 1):
        start = int(np.clip(2 * j - 4, 0, rows - SLAB_ROWS))
        halves = []
        for rl in range(2):
            r = 2 * j + rl
            kr0 = int(np.clip(r - WIN_ROWS // 2, 0, rows - WIN_ROWS))
            blocks = []
            for i in range(SLAB_ROWS):
                kr = start + i
                blocks.append(toe[:, kr - r + WIN_ROWS - 1] if kr0 <= kr < kr0 + WIN_ROWS else masked)
            halves.append(jnp.concatenate(blocks, axis=2))
        variants.append(jnp.concatenate(halves, axis=1))
    return jnp.stack(variants)


def _attn_kernel(q_ref, kt_ref, v_ref, bias_ref, g_ref, o_ref, *, n_pairs):
    step = pl.program_id(1)
    pairs_per_step = ATT_TOKENS // PAIR_TOKENS
    lane = lax.broadcasted_iota(jnp.int32, (PAIR_TOKENS, 2 * HEAD_DIM), 1)
    first_head = lane < HEAD_DIM

    n_hp = NA_HEADS // 2
    dims = [slice(hp * 2 * HEAD_DIM, (hp + 1) * 2 * HEAD_DIM) for hp in range(n_hp)]

    def pairs_body(it, carry):
        pair = []
        for pi in range(ATT_PAIRS):
            p = it * ATT_PAIRS + pi
            j = step * pairs_per_step + p
            start2 = jnp.clip(j - 2, 0, n_pairs - SLAB_ROWS // 2)
            variant = jnp.where(j < 2, j, jnp.where(j >= n_pairs - 2, j - (n_pairs - 2) + 3, 2))
            row0 = pl.multiple_of(p * PAIR_TOKENS, PAIR_TOKENS)
            qp = q_ref[0, pl.ds(row0, PAIR_TOKENS), :]
            kt5 = kt_ref[0, pl.ds(start2, SLAB_ROWS // 2)]
            vs = v_ref[0, pl.ds(pl.multiple_of(start2 * PAIR_TOKENS, PAIR_TOKENS), SLAB_TOKENS), :]
            pair.append((row0, variant, qp, kt5, vs))
        units = [(pi, hp) for pi in range(ATT_PAIRS) for hp in range(n_hp)]
        scores, probs, sums, outs = {}, {}, {}, {}

        def score_phase(u):
            pi, hp = u
            _, _, qp, kt5, _ = pair[pi]
            q2 = qp[:, dims[hp]]
            kt2 = jnp.concatenate([kt5[c, dims[hp], :] for c in range(SLAB_ROWS // 2)], axis=1)
            zero = jnp.zeros_like(q2)
            qm = jnp.concatenate([jnp.where(first_head, q2, zero), jnp.where(first_head, zero, q2)], axis=0)
            scores[u] = jnp.dot(qm, kt2, preferred_element_type=F32)

        def softmax_phase(u):
            pi, hp = u
            variant = pair[pi][1]
            b2 = jnp.concatenate([bias_ref[variant, hp * 2], bias_ref[variant, hp * 2 + 1]], axis=0)
            s = scores.pop(u) + b2
            e = jnp.exp2(s - jnp.max(s, axis=-1, keepdims=True))
            sums[u] = jnp.sum(e, axis=-1, keepdims=True)
            probs[u] = e.astype(BF16)

        def value_phase(u):
            pi, hp = u
            vs = pair[pi][4]
            o = jnp.dot(probs.pop(u), vs[:, dims[hp]], preferred_element_type=F32) / sums.pop(u)
            outs[u] = jnp.where(first_head, o[:PAIR_TOKENS], o[PAIR_TOKENS:])

        for t in range(len(units) + 2):
            if t < len(units):
                score_phase(units[t])
            if 0 <= t - 1 < len(units):
                softmax_phase(units[t - 1])
            if 0 <= t - 2 < len(units):
                value_phase(units[t - 2])
        for pi in range(ATT_PAIRS):
            a = jnp.concatenate([outs[(pi, hp)] for hp in range(n_hp)], axis=1)
            o_ref[0, pl.ds(pair[pi][0], PAIR_TOKENS), :] = _rms(a, g_ref[...]).astype(BF16)
        return carry

    lax.fori_loop(0, pairs_per_step // ATT_PAIRS, pairs_body, 0)


def _attention(q3, kt4, v3, bias, g_na):
    bsz, L, _ = q3.shape
    n_pairs = L // PAIR_TOKENS
    return pl.pallas_call(
        functools.partial(_attn_kernel, n_pairs=n_pairs),
        grid=(bsz, L // ATT_TOKENS),
        in_specs=[pl.BlockSpec((1, ATT_TOKENS, NA_WIDTH), lambda b, s: (b, s, 0)),
                  pl.BlockSpec((1, n_pairs, NA_WIDTH, PAIR_TOKENS), lambda b, s: (b, 0, 0, 0)),
                  pl.BlockSpec((1, L, NA_WIDTH), lambda b, s: (b, 0, 0)),
                  pl.BlockSpec((N_BIAS_VARIANTS, NA_HEADS, PAIR_TOKENS, SLAB_TOKENS), lambda b, s: (0, 0, 0, 0),
                               pipeline_mode=pl.Buffered(1)),
                  pl.BlockSpec((1, NA_WIDTH), lambda b, s: (0, 0))],
        out_specs=pl.BlockSpec((1, ATT_TOKENS, NA_WIDTH), lambda b, s: (b, s, 0)),
        out_shape=jax.ShapeDtypeStruct((bsz, L, NA_WIDTH), BF16),
        compiler_params=_cparams(("arbitrary", "arbitrary")),
    )(q3, kt4, v3, bias, g_na)


def _angles(j, k, n):
    return ((j * k) % n).astype(np.float64) * (2.0 * np.pi / n)


def _row_stage_table(R):
    k = np.arange(R)
    ang = _angles(k[:, None], k[None, :], R)
    f = np.concatenate([np.cos(ang), -np.sin(ang)], axis=0)
    return jnp.asarray(np.kron(f, np.eye(DFT_COLS)), dtype=BF16)


def _col_stage_table(L):
    R = L // GRID_W
    k1 = np.arange(R)[:, None, None]
    k2 = np.arange(GRID_W)[None, :, None]
    c = np.arange(GRID_W)[None, None, :]
    ang = _angles(c, k1 + R * k2, L)
    cc, ss = np.cos(ang), np.sin(ang)
    top = np.concatenate([cc, ss], axis=2)
    bot = np.concatenate([-ss, cc], axis=2)
    return jnp.asarray(np.concatenate([top, bot], axis=1), dtype=BF16)


def _channel_table():
    j = np.arange(FN_GROUP_DIM)
    ang = _angles(j[:, None], j[None, :], FN_GROUP_DIM)
    return jnp.asarray(np.concatenate([np.cos(ang), np.sin(ang)], axis=0), dtype=BF16)


def _dft_rows_kernel(f_ref, x_ref, y_ref):
    R = x_ref.shape[1]
    x = x_ref[0].reshape(R * DFT_COLS, FN_WIDTH)
    y = jnp.dot(f_ref[...], x, preferred_element_type=F32).astype(BF16)
    y_ref[0] = y.reshape(2 * R, DFT_COLS, FN_WIDTH)


def _dft_rows(f_rows, u4):
    bsz, R, _, _ = u4.shape
    return pl.pallas_call(
        _dft_rows_kernel,
        grid=(bsz, GRID_W // DFT_COLS),
        in_specs=[pl.BlockSpec((2 * R * DFT_COLS, R * DFT_COLS), lambda b, t: (0, 0)),
                  pl.BlockSpec((1, R, DFT_COLS, FN_WIDTH), lambda b, t: (b, 0, t, 0))],
        out_specs=pl.BlockSpec((1, 2 * R, DFT_COLS, FN_WIDTH), lambda b, t: (b, 0, t, 0)),
        out_shape=jax.ShapeDtypeStruct((bsz, 2 * R, GRID_W, FN_WIDTH), BF16),
        compiler_params=_cparams(("arbitrary", "arbitrary")),
    )(f_rows, u4)


def _dft_cols_kernel(yr_ref, yi_ref, m_ref, cs_ref, wf_ref, g_ref, o_ref):
    re, im = [], []
    for j in range(DFT_K1):
        y = jnp.concatenate([yr_ref[0, j], yi_ref[0, j]], axis=0)
        x = jnp.dot(m_ref[j], y, preferred_element_type=F32)
        re.append(x[:GRID_W])
        im.append(x[GRID_W:])
    a = jnp.concatenate(re, axis=0).astype(BF16)
    b = jnp.concatenate(im, axis=0).astype(BF16)
    ys = []
    for g in range(FN_GROUPS):
        sl = slice(g * FN_GROUP_DIM, (g + 1) * FN_GROUP_DIM)
        ab = jnp.concatenate([a[:, sl], b[:, sl]], axis=1)
        z = jnp.dot(ab, cs_ref[...], preferred_element_type=F32).astype(BF16)
        ys.append(jnp.dot(z, wf_ref[g], preferred_element_type=F32))
    out = _rms(jnp.concatenate(ys, axis=1), g_ref[...]).astype(BF16)
    for j in range(DFT_K1):
        o_ref[0, :, j * FN_WIDTH:(j + 1) * FN_WIDTH] = out[j * GRID_W:(j + 1) * GRID_W, :]


def _dft_cols(y4, m_cols, cs, wf, g_fn):
    bsz, R2, _, _ = y4.shape
    R = R2 // 2
    steps = R // DFT_K1
    return pl.pallas_call(
        _dft_cols_kernel,
        grid=(bsz, steps),
        in_specs=[pl.BlockSpec((1, DFT_K1, GRID_W, FN_WIDTH), lambda b, t: (b, t, 0, 0)),
                  pl.BlockSpec((1, DFT_K1, GRID_W, FN_WIDTH), lambda b, t: (b, steps + t, 0, 0)),
                  pl.BlockSpec((DFT_K1, 2 * GRID_W, 2 * GRID_W), lambda b, t: (t, 0, 0)),
                  pl.BlockSpec((2 * FN_GROUP_DIM, FN_GROUP_DIM), lambda b, t: (0, 0)),
                  pl.BlockSpec((FN_GROUPS, FN_GROUP_DIM, FN_GROUP_DIM), lambda b, t: (0, 0, 0)),
                  pl.BlockSpec((1, FN_WIDTH), lambda b, t: (0, 0))],
        out_specs=pl.BlockSpec((1, GRID_W, DFT_K1 * FN_WIDTH), lambda b, t: (b, 0, t)),
        out_shape=jax.ShapeDtypeStruct((bsz, GRID_W, R * FN_WIDTH), BF16),
        compiler_params=_cparams(("arbitrary", "arbitrary")),
    )(y4, y4, m_cols, cs, wf, g_fn)


def _fourier(u, bsz, L, cs, wf, g_fn):
    R = L // GRID_W
    y = _dft_rows(_row_stage_table(R), u.reshape(bsz, R, GRID_W, FN_WIDTH))
    fn = _dft_cols(y, _col_stage_table(L), cs, wf, g_fn)
    return fn.reshape(bsz * L, FN_WIDTH)


def _outproj_kernel(na_ref, fn_ref, x_ref, gt1_ref, sh2_ref, sc2_ref, gpost_ref, gpre_ref,
                    woa_ref, wof_ref, wr_ref, br_ref,
                    x1_ref, h2_ref, gate_ref, pos_ref, cnt_ref):
    mix = (jnp.dot(na_ref[...], woa_ref[...], preferred_element_type=F32)
           + jnp.dot(fn_ref[...], wof_ref[...], preferred_element_type=F32))
    x1 = x_ref[...] + gt1_ref[0] * _rms(mix, gpost_ref[...])
    x1_ref[...] = x1
    h2 = _rms(x1, gpre_ref[...]) * (1.0 + sc2_ref[0]) + sh2_ref[0]
    hi = h2.astype(BF16)
    h2_ref[...] = hi

    lo = (h2 - hi.astype(F32)).astype(BF16)
    nt = (((1,), (1,)), ((), ()))
    a = lax.dot_general(wr_ref[...], hi, nt, preferred_element_type=F32)
    b = lax.dot_general(wr_ref[0:N_EXPERTS, :], lo, nt, preferred_element_type=F32)
    logits = a[:N_EXPERTS] + a[N_EXPERTS:] + b + br_ref[:, 0:1]

    eidx = lax.broadcasted_iota(jnp.int32, (N_EXPERTS, TM), 0)
    vals, onehots = [], []
    cur = logits
    for k in range(TOP_K):
        m = jnp.max(cur, axis=0, keepdims=True)
        idx = jnp.min(jnp.where(cur == m, eidx, N_EXPERTS), axis=0, keepdims=True)
        hit = eidx == idx
        vals.append(m)
        onehots.append(hit)
        cur = jnp.where(hit, -jnp.inf, cur)
    exps = [jnp.exp(v - vals[0]) for v in vals]
    den = exps[0] + exps[1] + exps[2] + exps[3]
    for k in range(TOP_K):
        gate_ref[k:k + 1, :] = exps[k] / den

    oh = [h.astype(F32) for h in onehots]
    oh_sum = oh[0] + oh[1] + oh[2] + oh[3]
    s_i = lax.broadcasted_iota(jnp.int32, (TM, TM), 0)
    t_i = lax.broadcasted_iota(jnp.int32, (TM, TM), 1)
    earlier = jnp.where(s_i < t_i, 1.0, 0.0).astype(BF16)
    same_expert_before = jnp.dot(oh_sum.astype(BF16), earlier, preferred_element_type=F32)
    counts = jnp.broadcast_to(jnp.sum(oh_sum, axis=1, keepdims=True), (N_EXPERTS, LANES))
    cnt_ref[...] = counts
    run_tiles = jnp.floor((counts + (SUBLANES - 1)) * (1.0 / SUBLANES))
    e_r = lax.broadcasted_iota(jnp.int32, (N_EXPERTS, N_EXPERTS), 0)
    e_c = lax.broadcasted_iota(jnp.int32, (N_EXPERTS, N_EXPERTS), 1)
    lower = jnp.where(e_c < e_r, 1.0, 0.0).astype(BF16)
    run_start = SUBLANES * jnp.dot(lower, run_tiles.astype(BF16), preferred_element_type=F32)[:, 0:1]
    base = run_start + same_expert_before
    for k in range(TOP_K):
        pos_ref[k:k + 1, :] = jnp.sum(oh[k] * base, axis=0, keepdims=True).astype(jnp.int32)


def _outproj(na2d, fn2d, x2d, ada3, ada_row0, bsz, L, g_post, g_pre, woa, wof, wr2, br):
    T = bsz * L
    tpb = L // TM
    row = lambda j: (lambda i: ((ada_row0 + i // tpb) * 6 + j, 0, 0))
    const2 = lambda i: (0, 0)
    tok = lambda i: (i, 0)
    lanes = lambda i: (0, i)
    return pl.pallas_call(
        _outproj_kernel,
        grid=(T // TM,),
        in_specs=[pl.BlockSpec((TM, NA_WIDTH), tok),
                  pl.BlockSpec((TM, FN_WIDTH), tok),
                  pl.BlockSpec((TM, D_MODEL), tok),
                  pl.BlockSpec((1, 1, D_MODEL), row(2)),
                  pl.BlockSpec((1, 1, D_MODEL), row(3)),
                  pl.BlockSpec((1, 1, D_MODEL), row(4)),
                  pl.BlockSpec((1, D_MODEL), const2),
                  pl.BlockSpec((1, D_MODEL), const2),
                  pl.BlockSpec((NA_WIDTH, D_MODEL), const2),
                  pl.BlockSpec((FN_WIDTH, D_MODEL), const2),
                  pl.BlockSpec((2 * N_EXPERTS, D_MODEL), const2),
                  pl.BlockSpec((N_EXPERTS, LANES), const2)],
        out_specs=[pl.BlockSpec((TM, D_MODEL), tok),
                   pl.BlockSpec((TM, D_MODEL), tok),
                   pl.BlockSpec((TOP_K, TM), lanes),
                   pl.BlockSpec((TOP_K, TM), lanes),
                   pl.BlockSpec((N_EXPERTS, LANES), tok)],
        out_shape=[jax.ShapeDtypeStruct((T, D_MODEL), F32),
                   jax.ShapeDtypeStruct((T, D_MODEL), BF16),
                   jax.ShapeDtypeStruct((TOP_K, T), F32),
                   jax.ShapeDtypeStruct((TOP_K, T), jnp.int32),
                   jax.ShapeDtypeStruct((T // TM * N_EXPERTS, LANES), F32)],
        compiler_params=_cparams(("arbitrary",)),
    )(na2d, fn2d, x2d, ada3, ada3, ada3, g_post, g_pre, woa, wof, wr2, br)


def _zero_fill(pad_lo_ref, pad_n_ref, tail_ref, xs_hbm, zbuf, zsem, n_blocks, wait):
    def run(cp):
        cp.wait() if wait else cp.start()

    def expert(e, c):
        pos = pad_lo_ref[e]
        n = pad_n_ref[e]
        for size in _pow2_sizes(ZERO_ROWS):
            @pl.when((n & size) != 0)
            def _(pos=pos, size=size):
                dst = xs_hbm.at[pl.ds(pl.multiple_of(pos, SUBLANES), size)]
                run(pltpu.make_async_copy(zbuf.at[pl.ds(0, size)], dst, zsem))
            pos = pos + (n & size)
        return c

    lax.fori_loop(0, N_EXPERTS, expert, 0)

    def tail(blk, c):
        for half in range(MOE_BLOCK // ZERO_ROWS):
            row0 = pl.multiple_of(blk * MOE_BLOCK + half * ZERO_ROWS, ZERO_ROWS)
            run(pltpu.make_async_copy(zbuf, xs_hbm.at[pl.ds(row0, ZERO_ROWS)], zsem))
        return c

    lax.fori_loop(tail_ref[0], n_blocks, tail, 0)


def _tile_pieces(cnt_ref, off_ref, dst_ref, tile, local_buf, sorted_hbm, sem, to_sorted, fill_copy):
    def expert(e, c):
        j = tile * N_EXPERTS + e
        n = cnt_ref[j]
        src = off_ref[j]
        dst = dst_ref[j]
        for size in _pow2_sizes(TM):
            @pl.when((n & size) != 0)
            def _(src=src, dst=dst, size=size):
                loc = local_buf.at[pl.ds(pl.multiple_of(src, SUBLANES), size)]
                glob = sorted_hbm.at[pl.ds(pl.multiple_of(dst, SUBLANES), size)]
                cp = pltpu.make_async_copy(loc, glob, sem) if to_sorted else pltpu.make_async_copy(glob, loc, sem)
                cp.start()
            src = src + (n & size)
            dst = dst + (n & size)
        return c

    lax.fori_loop(0, N_EXPERTS, expert, 0)

    last = tile * N_EXPERTS + N_EXPERTS - 1
    moved = off_ref[last] + cnt_ref[last]
    fill = TILE_ROWS - moved
    done = 0
    for size in _pow2_sizes(FILL_ROWS):
        @pl.when((fill & size) != 0)
        def _(done=done, size=size):
            fill_copy(pl.multiple_of(moved + done, SUBLANES), pl.multiple_of(done, SUBLANES), size).start()
        done = done + (fill & size)


def _tile_wait(local_buf, sorted_hbm, sem):
    for c in range(TILE_ROWS // SORT_CHUNK):
        rows = pl.ds(c * SORT_CHUNK, SORT_CHUNK)
        pltpu.make_async_copy(local_buf.at[rows], sorted_hbm.at[rows], sem).wait()


def _dispatch_kernel(pad_lo_ref, pad_n_ref, tail_ref, cnt_ref, off_ref, dst_ref,
                     pos_ref, h2a_ref, h2b_ref, xs_hbm, spare_hbm, sbuf, zbuf, sem, zsem,
                     *, n_blocks, tiles_a, n_tiles):
    i = pl.program_id(0)
    buf = i % 2

    @pl.when(i == 0)
    def _():
        zbuf[...] = jnp.zeros_like(zbuf)
        _zero_fill(pad_lo_ref, pad_n_ref, tail_ref, xs_hbm, zbuf, zsem, n_blocks, wait=False)
        _zero_fill(pad_lo_ref, pad_n_ref, tail_ref, xs_hbm, zbuf, zsem, n_blocks, wait=True)
        for b in range(2):
            pltpu.make_async_copy(zbuf.at[pl.ds(0, FILL_ROWS)], spare_hbm.at[b], zsem).start()
        for b in range(2):
            pltpu.make_async_copy(zbuf.at[pl.ds(0, FILL_ROWS)], spare_hbm.at[b], zsem).wait()

    def fill_copy(local_row, fill_row, size):
        del local_row
        return pltpu.make_async_copy(zbuf.at[pl.ds(0, size)], spare_hbm.at[buf, pl.ds(fill_row, size)], sem.at[buf])

    @pl.when(i >= 2)
    def _():
        _tile_wait(sbuf.at[buf], xs_hbm, sem.at[buf])

    def sort_tile(h2_ref):
        h = h2_ref[...]
        pos = pos_ref[...]
        for c in range(TILE_ROWS // SORT_CHUNK):
            row = lax.broadcasted_iota(jnp.int32, (SORT_CHUNK, TM), 0) + c * SORT_CHUNK
            sel = jnp.zeros((SORT_CHUNK, TM), F32)
            for k in range(TOP_K):
                sel = jnp.where(row == pos[k:k + 1, :], 1.0, sel)
            sbuf[buf, c * SORT_CHUNK:(c + 1) * SORT_CHUNK, :] = jnp.dot(sel.astype(BF16), h,
                                                                        preferred_element_type=F32)

    @pl.when(i < tiles_a)
    def _():
        sort_tile(h2a_ref)

    @pl.when(i >= tiles_a)
    def _():
        sort_tile(h2b_ref)

    _tile_pieces(cnt_ref, off_ref, dst_ref, i, sbuf.at[buf], xs_hbm, sem.at[buf], True, fill_copy)

    @pl.when(i == n_tiles - 1)
    def _():
        if n_tiles >= 2:
            _tile_wait(sbuf.at[1 - buf], xs_hbm, sem.at[1 - buf])
        _tile_wait(sbuf.at[buf], xs_hbm, sem.at[buf])


def _dispatch(pad_lo, pad_n, tail, cnt, off, dst, pos, h2a, h2b, n_blocks):
    tiles_a = h2a.shape[0] // TM
    tiles_b = h2b.shape[0] // TM
    n_tiles = tiles_a + tiles_b
    P = n_blocks * MOE_BLOCK
    kern = functools.partial(_dispatch_kernel, n_blocks=n_blocks, tiles_a=tiles_a, n_tiles=n_tiles)
    return pl.pallas_call(
        kern,
        grid_spec=pltpu.PrefetchScalarGridSpec(
            num_scalar_prefetch=6,
            grid=(n_tiles,),
            in_specs=[pl.BlockSpec((TOP_K, TM), lambda i, *_: (0, i)),
                      pl.BlockSpec((TM, D_MODEL), lambda i, *_: (jnp.minimum(i, tiles_a - 1), 0)),
                      pl.BlockSpec((TM, D_MODEL), lambda i, *_: (jnp.maximum(i - tiles_a, 0), 0))],
            out_specs=[pl.BlockSpec(memory_space=pl.ANY), pl.BlockSpec(memory_space=pl.ANY)],
            scratch_shapes=[pltpu.VMEM((2, TILE_ROWS, D_MODEL), F32),
                            pltpu.VMEM((ZERO_ROWS, D_MODEL), F32),
                            pltpu.SemaphoreType.DMA((2,)),
                            pltpu.SemaphoreType.DMA]),
        out_shape=[jax.ShapeDtypeStruct((P, D_MODEL), F32),
                   jax.ShapeDtypeStruct((2, FILL_ROWS, D_MODEL), F32)],
        compiler_params=_cparams(("arbitrary",)),
    )(pad_lo, pad_n, tail, cnt, off, dst, pos, h2a, h2b)[0]


def _expert_kernel(be_ref, used_ref, x_ref, wgu_ref, bgu_ref, wd_ref, bd_ref, y_ref, wgu_bf, wd_bf):
    i = pl.program_id(0)
    live = i < used_ref[0]

    @pl.when(jnp.logical_or(i == 0, be_ref[i] != be_ref[jnp.maximum(i - 1, 0)]))
    def _():
        wgu_bf[...] = wgu_ref[0].astype(BF16)
        wd_bf[...] = wd_ref[0].astype(BF16)

    @pl.when(live)
    def _():
        for part in range(MOE_BLOCK // EXPERT_ROWS):
            rows = slice(part * EXPERT_ROWS, (part + 1) * EXPERT_ROWS)
            x = x_ref[rows, :].astype(BF16)
            gu = jnp.dot(x, wgu_bf[...], preferred_element_type=F32) + bgu_ref[0]
            glu = jnp.minimum(gu[:, :D_FF], SWIGLU_LIMIT)
            lin = jnp.clip(gu[:, D_FF:], -SWIGLU_LIMIT, SWIGLU_LIMIT)
            act = glu * jax.nn.sigmoid(SWIGLU_ALPHA * glu) * (lin + 1.0)
            y_ref[rows, :] = jnp.dot(act.astype(BF16), wd_bf[...], preferred_element_type=F32) + bd_ref[0]

    @pl.when(jnp.logical_not(live))
    def _():
        y_ref[...] = jnp.zeros_like(y_ref)


def _experts(block_e, n_used, xs, wgu, bgu, wd, bd):
    n_blocks = block_e.shape[0]
    x_map = lambda i, be, used: (jnp.minimum(i, used[0] - 1), 0)
    return pl.pallas_call(
        _expert_kernel,
        grid_spec=pltpu.PrefetchScalarGridSpec(
            num_scalar_prefetch=2,
            grid=(n_blocks,),
            in_specs=[pl.BlockSpec((MOE_BLOCK, D_MODEL), x_map),
                      pl.BlockSpec((1, D_MODEL, 2 * D_FF), lambda i, be, used: (be[i], 0, 0)),
                      pl.BlockSpec((1, 1, 2 * D_FF), lambda i, be, used: (be[i], 0, 0)),
                      pl.BlockSpec((1, D_FF, D_MODEL), lambda i, be, used: (be[i], 0, 0)),
                      pl.BlockSpec((1, 1, D_MODEL), lambda i, be, used: (be[i], 0, 0))],
            out_specs=pl.BlockSpec((MOE_BLOCK, D_MODEL), lambda i, be, used: (i, 0)),
            scratch_shapes=[pltpu.VMEM((D_MODEL, 2 * D_FF), BF16),
                            pltpu.VMEM((D_FF, D_MODEL), BF16)]),
        out_shape=jax.ShapeDtypeStruct(xs.shape, F32),
        compiler_params=_cparams(("arbitrary",)),
    )(block_e, n_used, xs, wgu, bgu, wd, bd)


def _combine_kernel(cnt_ref, off_ref, dst_ref, ys_hbm, pos_ref, gate_ref, x1_ref, gt2_ref, g_ref, o_ref,
                    ybuf, sem, *, tile0, n_tiles):
    i = pl.program_id(0)
    buf = i % 2

    def fetch(tile, b):
        def fill_copy(local_row, fill_row, size):
            return pltpu.make_async_copy(ys_hbm.at[pl.ds(fill_row, size)], ybuf.at[b, pl.ds(local_row, size)],
                                         sem.at[b])
        _tile_pieces(cnt_ref, off_ref, dst_ref, tile, ybuf.at[b], ys_hbm, sem.at[b], False, fill_copy)

    @pl.when(i == 0)
    def _():
        fetch(tile0, 0)

    @pl.when(i + 1 < n_tiles)
    def _():
        fetch(tile0 + i + 1, 1 - buf)

    _tile_wait(ybuf.at[buf], ys_hbm, sem.at[buf])

    pos = pos_ref[...]
    gate = gate_ref[...]
    moe = jnp.zeros((TM, D_MODEL), F32)
    for c in range(TILE_ROWS // SORT_CHUNK):
        col = lax.broadcasted_iota(jnp.int32, (TM, SORT_CHUNK), 1) + c * SORT_CHUNK
        w = jnp.zeros((TM, SORT_CHUNK), F32)
        for k in range(TOP_K):
            w = jnp.where(col == pos[:, k:k + 1], gate[:, k:k + 1], w)
        rows = ybuf[buf, c * SORT_CHUNK:(c + 1) * SORT_CHUNK, :]
        moe = moe + jnp.dot(w.astype(BF16), rows.astype(BF16), preferred_element_type=F32)
    o_ref[...] = x1_ref[...] + gt2_ref[0] * _rms(moe, g_ref[...])


def _combine(cnt, off, dst, ys, tile0, pos_tk, gates_tk, x1, ada3, ada_row0, bsz, L, g_post):
    T = bsz * L
    tpb = L // TM
    n_tiles = T // TM
    return pl.pallas_call(
        functools.partial(_combine_kernel, tile0=tile0, n_tiles=n_tiles),
        grid_spec=pltpu.PrefetchScalarGridSpec(
            num_scalar_prefetch=3,
            grid=(n_tiles,),
            in_specs=[pl.BlockSpec(memory_space=pl.ANY),
                      pl.BlockSpec((TM, TOP_K), lambda i, *_: (i, 0)),
                      pl.BlockSpec((TM, TOP_K), lambda i, *_: (i, 0)),
                      pl.BlockSpec((TM, D_MODEL), lambda i, *_: (i, 0)),
                      pl.BlockSpec((1, 1, D_MODEL), lambda i, *_: ((ada_row0 + i // tpb) * 6 + 5, 0, 0)),
                      pl.BlockSpec((1, D_MODEL), lambda i, *_: (0, 0))],
            out_specs=pl.BlockSpec((TM, D_MODEL), lambda i, *_: (i, 0)),
            scratch_shapes=[pltpu.VMEM((2, TILE_ROWS, D_MODEL), F32),
                            pltpu.SemaphoreType.DMA((2,))]),
        out_shape=jax.ShapeDtypeStruct((T, D_MODEL), F32),
        compiler_params=_cparams(("arbitrary",)),
    )(cnt, off, dst, ys, pos_tk, gates_tk, x1, ada3, g_post)


def kernel(x_prompt, x_sample, c_prompt, c_sample, w_ada, b_ada, g_pre_mix, g_post_mix, g_pre_ffn, g_post_ffn,
           w_in, rpb, w_fourier, g_na, g_fn, w_out, w_router, b_router, w_gate_up, b_gate_up, w_down, b_down):
    assert w_ada.shape[0] == 1, "single layer"
    groups = [(x_prompt, c_prompt), (x_sample, c_sample)]
    for x, _ in groups:
        assert x.shape[1] % ATT_TOKENS == 0 and x.shape[1] // GRID_W >= 12 and x.shape[2] == D_MODEL

    row2 = lambda a: a[0].reshape(1, -1)
    w_in0 = w_in[0]
    wq = (w_in0[:, :NA_WIDTH] * (HEAD_DIM ** -0.5 * LOG2_E)).astype(BF16)
    wkt = w_in0[:, NA_WIDTH:2 * NA_WIDTH].T.astype(BF16)
    wv = w_in0[:, 2 * NA_WIDTH:3 * NA_WIDTH].astype(BF16)
    wu = w_in0[:, 3 * NA_WIDTH:].astype(BF16)
    cs = _channel_table()
    bias = _bias_table(rpb[0])
    wf = w_fourier[0].astype(BF16)
    woa = w_out[0, :NA_WIDTH].astype(BF16)
    wof = w_out[0, NA_WIDTH:].astype(BF16)
    wr_t = w_router[0].T
    wr_hi = wr_t.astype(BF16)
    wr_lo = (wr_t - wr_hi.astype(F32)).astype(BF16)
    wr2 = jnp.concatenate([wr_hi, wr_lo], axis=0)
    br = jnp.broadcast_to(b_router[0][:, None], (N_EXPERTS, LANES))
    wgu = w_gate_up[0]
    wd = w_down[0]
    bgu = b_gate_up[0].reshape(N_EXPERTS, 1, 2 * D_FF)
    bd = b_down[0].reshape(N_EXPERTS, 1, D_MODEL)

    c_all = jnp.concatenate([c_prompt, c_sample], axis=0)
    ada3 = _ada(c_all, w_ada[0], b_ada[0]).reshape(c_all.shape[0] * 6, 1, D_MODEL)

    per_group = []
    ada_row0 = 0
    for x, _ in groups:
        bsz, L, _ = x.shape
        x2d = x.reshape(bsz * L, D_MODEL)
        q, kt4, v, u = _inproj(x2d, ada3, ada_row0, bsz, L, row2(g_pre_mix), wq, wkt, wv, wu)
        na = _attention(q.reshape(bsz, L, NA_WIDTH), kt4, v.reshape(bsz, L, NA_WIDTH), bias, row2(g_na))
        fn = _fourier(u, bsz, L, cs, wf, row2(g_fn))
        x1, h2, gate_t, pos_t, cnt = _outproj(
            na.reshape(bsz * L, NA_WIDTH), fn, x2d, ada3, ada_row0, bsz, L,
            row2(g_post_mix), row2(g_pre_ffn), woa, wof, wr2, br)
        per_group.append((x1, h2, gate_t, pos_t, cnt, bsz, L, ada_row0))
        ada_row0 += bsz

    T = sum(g[5] * g[6] for g in per_group)
    max_rows = T * TOP_K + (T // TM) * N_EXPERTS * (SUBLANES - 1) + N_EXPERTS * (MOE_BLOCK - 1)
    n_blocks = -(-max_rows // MOE_BLOCK)
    tile_cnt = jnp.concatenate([g[4][:, 0].reshape(-1, N_EXPERTS) for g in per_group], axis=0).astype(jnp.int32)
    tile_cnt = _round_up(tile_cnt, SUBLANES)
    total = jnp.sum(tile_cnt, axis=0)
    padded = ((total + MOE_BLOCK - 1) // MOE_BLOCK) * MOE_BLOCK
    pend = jnp.cumsum(padded)
    pstart = pend - padded
    run = jnp.cumsum(tile_cnt, axis=0) - tile_cnt
    tile_dst = (pstart[None, :] + run).reshape(-1)
    tile_off = (jnp.cumsum(tile_cnt, axis=1) - tile_cnt).reshape(-1)
    tile_cnt = tile_cnt.reshape(-1)
    block_row0 = jnp.arange(n_blocks, dtype=jnp.int32) * MOE_BLOCK
    block_e = jnp.minimum(jnp.sum((pend[None, :] <= block_row0[:, None]).astype(jnp.int32), axis=1), N_EXPERTS - 1)
    pad_lo = pstart + total
    pad_n = padded - total
    tail = (pend[-1:] // MOE_BLOCK).astype(jnp.int32)

    pos_all = jnp.concatenate([g[3] for g in per_group], axis=1)
    xs = _dispatch(pad_lo, pad_n, tail, tile_cnt, tile_off, tile_dst, pos_all,
                   per_group[0][1], per_group[1][1], n_blocks)
    ys = _experts(block_e, tail, xs, wgu, bgu, wd, bd)

    outs = []
    tile0 = 0
    for x1, _, gate_t, pos_t, _, bsz, L, row0 in per_group:
        out = _combine(tile_cnt, tile_off, tile_dst, ys, tile0, pos_t.T, gate_t.T, x1, ada3, row0, bsz, L,
                       row2(g_post_ffn))
        outs.append(out.reshape(bsz, L, D_MODEL))
        tile0 += bsz * L // TM
    return tuple(outs)
```

```python
import functools

import numpy as np
import jax
import jax.numpy as jnp
from jax import lax
from jax.experimental import pallas as pl
from jax.experimental.pallas import tpu as pltpu

F32 = jnp.float32
BF16 = jnp.bfloat16

D_MODEL = 1024
GRID_W = 64
NA_WIDTH = 512
HEAD_DIM = 64
NA_HEADS = 8
FN_WIDTH = 512
FN_GROUPS = 4
FN_GROUP_DIM = 128
WIN_ROWS = 8
WIN_COLS = 16
N_EXPERTS = 32
TOP_K = 4
D_FF = 1024
SWIGLU_ALPHA = 1.702
SWIGLU_LIMIT = 7.0
MOE_BLOCK = 512
EXPERT_ROWS = 256
EPS = 1e-6

LANES = 128
PAIR_TOKENS = 2 * GRID_W
SLAB_ROWS = 10
SLAB_TOKENS = SLAB_ROWS * GRID_W
N_BIAS_VARIANTS = 5
MASK_VALUE = -1e30
LOG2_E = float(np.log2(np.e))

TM = 512
SUBLANES = 8
SORT_CHUNK = 256
TILE_ROWS = -(-(TM * TOP_K + N_EXPERTS * (SUBLANES - 1)) // SORT_CHUNK) * SORT_CHUNK
FILL_ROWS = TILE_ROWS - TM * TOP_K
ATT_TOKENS = 1024
ATT_PAIRS = 4
DFT_COLS = 16
DFT_K1 = 16
ZERO_ROWS = MOE_BLOCK // 2
VMEM_LIMIT = 56 * 1024 * 1024


def _cparams(sem):
    return pltpu.CompilerParams(dimension_semantics=sem, vmem_limit_bytes=VMEM_LIMIT)


def _rms(x, g):
    return x * lax.rsqrt(jnp.mean(x * x, axis=-1, keepdims=True) + EPS) * g


def _pow2_sizes(limit):
    sizes = []
    while limit >= SUBLANES:
        sizes.append(limit)
        limit //= 2
    return sizes


def _round_up(n, m):
    return (n + m - 1) // m * m


def _ada_kernel(c_ref, w_ref, b_ref, o_ref):
    c = c_ref[...]
    s = c * jax.nn.sigmoid(c)
    o_ref[...] = jnp.dot(s.astype(BF16), w_ref[...].astype(BF16), preferred_element_type=F32) + b_ref[...]


def _ada(c_all, w_ada, b_ada):
    nb = c_all.shape[0]
    n_out = w_ada.shape[1]
    tn = 1536
    return pl.pallas_call(
        _ada_kernel,
        grid=(n_out // tn,),
        in_specs=[pl.BlockSpec((nb, D_MODEL), lambda j: (0, 0)),
                  pl.BlockSpec((D_MODEL, tn), lambda j: (0, j)),
                  pl.BlockSpec((1, tn), lambda j: (0, j))],
        out_specs=pl.BlockSpec((nb, tn), lambda j: (0, j)),
        out_shape=jax.ShapeDtypeStruct((nb, n_out), F32),
        compiler_params=_cparams(("arbitrary",)),
    )(c_all, w_ada, b_ada.reshape(1, n_out))


def _inproj_kernel(x_ref, sh_ref, sc_ref, g_ref, wq_ref, wkt_ref, wv_ref, wu_ref,
                   q_ref, kt_ref, v_ref, u_ref):
    h = _rms(x_ref[...], g_ref[...]) * (1.0 + sc_ref[0]) + sh_ref[0]
    hb = h.astype(BF16)
    q_ref[...] = jnp.dot(hb, wq_ref[...], preferred_element_type=F32).astype(BF16)
    v_ref[...] = jnp.dot(hb, wv_ref[...], preferred_element_type=F32).astype(BF16)
    kt = lax.dot_general(wkt_ref[...], hb, (((1,), (1,)), ((), ())), preferred_element_type=F32).astype(BF16)
    for c in range(TM // PAIR_TOKENS):
        kt_ref[0, c] = kt[:, c * PAIR_TOKENS:(c + 1) * PAIR_TOKENS]
    u_ref[...] = jnp.dot(hb, wu_ref[...], preferred_element_type=F32).astype(BF16)


def _inproj(x2d, ada3, ada_row0, bsz, L, g_pre, wq, wkt, wv, wu):
    T = bsz * L
    tpb = L // TM
    row = lambda j: (lambda i: ((ada_row0 + i // tpb) * 6 + j, 0, 0))
    const2 = lambda i: (0, 0)
    return pl.pallas_call(
        _inproj_kernel,
        grid=(T // TM,),
        in_specs=[pl.BlockSpec((TM, D_MODEL), lambda i: (i, 0)),
                  pl.BlockSpec((1, 1, D_MODEL), row(0)),
                  pl.BlockSpec((1, 1, D_MODEL), row(1)),
                  pl.BlockSpec((1, D_MODEL), const2),
                  pl.BlockSpec((D_MODEL, NA_WIDTH), const2),
                  pl.BlockSpec((NA_WIDTH, D_MODEL), const2),
                  pl.BlockSpec((D_MODEL, NA_WIDTH), const2),
                  pl.BlockSpec((D_MODEL, FN_WIDTH), const2)],
        out_specs=[pl.BlockSpec((TM, NA_WIDTH), lambda i: (i, 0)),
                   pl.BlockSpec((1, TM // PAIR_TOKENS, NA_WIDTH, PAIR_TOKENS),
                                lambda i: (i // tpb, i % tpb, 0, 0)),
                   pl.BlockSpec((TM, NA_WIDTH), lambda i: (i, 0)),
                   pl.BlockSpec((TM, FN_WIDTH), lambda i: (i, 0))],
        out_shape=[jax.ShapeDtypeStruct((T, NA_WIDTH), BF16),
                   jax.ShapeDtypeStruct((bsz, L // PAIR_TOKENS, NA_WIDTH, PAIR_TOKENS), BF16),
                   jax.ShapeDtypeStruct((T, NA_WIDTH), BF16),
                   jax.ShapeDtypeStruct((T, FN_WIDTH), BF16)],
        compiler_params=_cparams(("arbitrary",)),
    )(x2d, ada3, ada3, g_pre, wq, wkt, wv, wu)


def _bias_table(rpb):
    rows = 32
    n_pairs = rows // 2
    c = np.arange(GRID_W)
    kc0 = np.clip(c - WIN_COLS // 2, 0, GRID_W - WIN_COLS)
    col_ok = (c[None, :] >= kc0[:, None]) & (c[None, :] < kc0[:, None] + WIN_COLS)
    ci = c[None, :] - c[:, None] + WIN_COLS - 1
    pick = ((ci[None] == np.arange(2 * WIN_COLS - 1)[:, None, None]) & col_ok[None]).astype(np.float32)
    toe = jnp.einsum('hrk,kcd->hrcd', rpb, jnp.asarray(pick), precision=lax.Precision.HIGHEST)
    toe = jnp.where(col_ok, toe * LOG2_E, MASK_VALUE)
    masked = jnp.full((NA_HEADS, GRID_W, GRID_W), MASK_VALUE, F32)
    variants = []
    for j in (0, 1, 2, n_pairs - 2, n_pairs - 1):
        start = int(np.clip(2 * j - 4, 0, rows - SLAB_ROWS))
        halves = []
        for rl in range(2):
            r = 2 * j + rl
            kr0 = int(np.clip(r - WIN_ROWS // 2, 0, rows - WIN_ROWS))
            blocks = []
            for i in range(SLAB_ROWS):
                kr = start + i
                blocks.append(toe[:, kr - r + WIN_ROWS - 1] if kr0 <= kr < kr0 + WIN_ROWS else masked)
            halves.append(jnp.concatenate(blocks, axis=2))
        variants.append(jnp.concatenate(halves, axis=1))
    return jnp.stack(variants)


def _attn_kernel(q_ref, kt_ref, v_ref, bias_ref, g_ref, o_ref, *, n_pairs):
    step = pl.program_id(1)
    pairs_per_step = ATT_TOKENS // PAIR_TOKENS
    lane = lax.broadcasted_iota(jnp.int32, (PAIR_TOKENS, 2 * HEAD_DIM), 1)
    first_head = lane < HEAD_DIM

    n_hp = NA_HEADS // 2
    dims = [slice(hp * 2 * HEAD_DIM, (hp + 1) * 2 * HEAD_DIM) for hp in range(n_hp)]

    def pairs_body(it, carry):
        pair = []
        for pi in range(ATT_PAIRS):
            p = it * ATT_PAIRS + pi
            j = step * pairs_per_step + p
            start2 = jnp.clip(j - 2, 0, n_pairs - SLAB_ROWS // 2)
            variant = jnp.where(j < 2, j, jnp.where(j >= n_pairs - 2, j - (n_pairs - 2) + 3, 2))
            row0 = pl.multiple_of(p * PAIR_TOKENS, PAIR_TOKENS)
            qp = q_ref[0, pl.ds(row0, PAIR_TOKENS), :]
            kt5 = kt_ref[0, pl.ds(start2, SLAB_ROWS // 2)]
            vs = v_ref[0, pl.ds(pl.multiple_of(start2 * PAIR_TOKENS, PAIR_TOKENS), SLAB_TOKENS), :]
            pair.append((row0, variant, qp, kt5, vs))
        units = [(pi, hp) for pi in range(ATT_PAIRS) for hp in range(n_hp)]
        scores, probs, sums, outs = {}, {}, {}, {}

        def score_phase(u):
            pi, hp = u
            _, _, qp, kt5, _ = pair[pi]
            q2 = qp[:, dims[hp]]
            kt2 = jnp.concatenate([kt5[c, dims[hp], :] for c in range(SLAB_ROWS // 2)], axis=1)
            zero = jnp.zeros_like(q2)
            qm = jnp.concatenate([jnp.where(first_head, q2, zero), jnp.where(first_head, zero, q2)], axis=0)
            scores[u] = jnp.dot(qm, kt2, preferred_element_type=F32)

        def softmax_phase(u):
            pi, hp = u
            variant = pair[pi][1]
            b2 = jnp.concatenate([bias_ref[variant, hp * 2], bias_ref[variant, hp * 2 + 1]], axis=0)
            s = scores.pop(u) + b2
            e = jnp.exp2(s - jnp.max(s, axis=-1, keepdims=True))
            sums[u] = jnp.sum(e, axis=-1, keepdims=True)
            probs[u] = e.astype(BF16)

        def value_phase(u):
            pi, hp = u
            vs = pair[pi][4]
            o = jnp.dot(probs.pop(u), vs[:, dims[hp]], preferred_element_type=F32) / sums.pop(u)
            outs[u] = jnp.where(first_head, o[:PAIR_TOKENS], o[PAIR_TOKENS:])

        for t in range(len(units) + 2):
            if t < len(units):
                score_phase(units[t])
            if 0 <= t - 1 < len(units):
                softmax_phase(units[t - 1])
            if 0 <= t - 2 < len(units):
                value_phase(units[t - 2])
        for pi in range(ATT_PAIRS):
            a = jnp.concatenate([outs[(pi, hp)] for hp in range(n_hp)], axis=1)
            o_ref[0, pl.ds(pair[pi][0], PAIR_TOKENS), :] = _rms(a, g_ref[...]).astype(BF16)
        return carry

    lax.fori_loop(0, pairs_per_step // ATT_PAIRS, pairs_body, 0)


def _attention(q3, kt4, v3, bias, g_na):
    bsz, L, _ = q3.shape
    n_pairs = L // PAIR_TOKENS
    return pl.pallas_call(
        functools.partial(_attn_kernel, n_pairs=n_pairs),
        grid=(bsz, L // ATT_TOKENS),
        in_specs=[pl.BlockSpec((1, ATT_TOKENS, NA_WIDTH), lambda b, s: (b, s, 0)),
                  pl.BlockSpec((1, n_pairs, NA_WIDTH, PAIR_TOKENS), lambda b, s: (b, 0, 0, 0)),
                  pl.BlockSpec((1, L, NA_WIDTH), lambda b, s: (b, 0, 0)),
                  pl.BlockSpec((N_BIAS_VARIANTS, NA_HEADS, PAIR_TOKENS, SLAB_TOKENS), lambda b, s: (0, 0, 0, 0),
                               pipeline_mode=pl.Buffered(1)),
                  pl.BlockSpec((1, NA_WIDTH), lambda b, s: (0, 0))],
        out_specs=pl.BlockSpec((1, ATT_TOKENS, NA_WIDTH), lambda b, s: (b, s, 0)),
        out_shape=jax.ShapeDtypeStruct((bsz, L, NA_WIDTH), BF16),
        compiler_params=_cparams(("arbitrary", "arbitrary")),
    )(q3, kt4, v3, bias, g_na)


def _angles(j, k, n):
    return ((j * k) % n).astype(np.float64) * (2.0 * np.pi / n)


def _row_stage_table(R):
    k = np.arange(R)
    ang = _angles(k[:, None], k[None, :], R)
    f = np.concatenate([np.cos(ang), -np.sin(ang)], axis=0)
    return jnp.asarray(np.kron(f, np.eye(DFT_COLS)), dtype=BF16)


def _col_stage_table(L):
    R = L // GRID_W
    k1 = np.arange(R)[:, None, None]
    k2 = np.arange(GRID_W)[None, :, None]
    c = np.arange(GRID_W)[None, None, :]
    ang = _angles(c, k1 + R * k2, L)
    cc, ss = np.cos(ang), np.sin(ang)
    top = np.concatenate([cc, ss], axis=2)
    bot = np.concatenate([-ss, cc], axis=2)
    return jnp.asarray(np.concatenate([top, bot], axis=1), dtype=BF16)


def _channel_table():
    j = np.arange(FN_GROUP_DIM)
    ang = _angles(j[:, None], j[None, :], FN_GROUP_DIM)
    return jnp.asarray(np.concatenate([np.cos(ang), np.sin(ang)], axis=0), dtype=BF16)


def _dft_rows_kernel(f_ref, x_ref, y_ref):
    R = x_ref.shape[1]
    x = x_ref[0].reshape(R * DFT_COLS, FN_WIDTH)
    y = jnp.dot(f_ref[...], x, preferred_element_type=F32).astype(BF16)
    y_ref[0] = y.reshape(2 * R, DFT_COLS, FN_WIDTH)


def _dft_rows(f_rows, u4):
    bsz, R, _, _ = u4.shape
    return pl.pallas_call(
        _dft_rows_kernel,
        grid=(bsz, GRID_W // DFT_COLS),
        in_specs=[pl.BlockSpec((2 * R * DFT_COLS, R * DFT_COLS), lambda b, t: (0, 0)),
                  pl.BlockSpec((1, R, DFT_COLS, FN_WIDTH), lambda b, t: (b, 0, t, 0))],
        out_specs=pl.BlockSpec((1, 2 * R, DFT_COLS, FN_WIDTH), lambda b, t: (b, 0, t, 0)),
        out_shape=jax.ShapeDtypeStruct((bsz, 2 * R, GRID_W, FN_WIDTH), BF16),
        compiler_params=_cparams(("arbitrary", "arbitrary")),
    )(f_rows, u4)


def _dft_cols_kernel(yr_ref, yi_ref, m_ref, cs_ref, wf_ref, g_ref, o_ref):
    re, im = [], []
    for j in range(DFT_K1):
        y = jnp.concatenate([yr_ref[0, j], yi_ref[0, j]], axis=0)
        x = jnp.dot(m_ref[j], y, preferred_element_type=F32)
        re.append(x[:GRID_W])
        im.append(x[GRID_W:])
    a = jnp.concatenate(re, axis=0).astype(BF16)
    b = jnp.concatenate(im, axis=0).astype(BF16)
    ys = []
    for g in range(FN_GROUPS):
        sl = slice(g * FN_GROUP_DIM, (g + 1) * FN_GROUP_DIM)
        ab = jnp.concatenate([a[:, sl], b[:, sl]], axis=1)
        z = jnp.dot(ab, cs_ref[...], preferred_element_type=F32).astype(BF16)
        ys.append(jnp.dot(z, wf_ref[g], preferred_element_type=F32))
    out = _rms(jnp.concatenate(ys, axis=1), g_ref[...]).astype(BF16)
    for j in range(DFT_K1):
        o_ref[0, :, j * FN_WIDTH:(j + 1) * FN_WIDTH] = out[j * GRID_W:(j + 1) * GRID_W, :]


def _dft_cols(y4, m_cols, cs, wf, g_fn):
    bsz, R2, _, _ = y4.shape
    R = R2 // 2
    steps = R // DFT_K1
    return pl.pallas_call(
        _dft_cols_kernel,
        grid=(bsz, steps),
        in_specs=[pl.BlockSpec((1, DFT_K1, GRID_W, FN_WIDTH), lambda b, t: (b, t, 0, 0)),
                  pl.BlockSpec((1, DFT_K1, GRID_W, FN_WIDTH), lambda b, t: (b, steps + t, 0, 0)),
                  pl.BlockSpec((DFT_K1, 2 * GRID_W, 2 * GRID_W), lambda b, t: (t, 0, 0)),
                  pl.BlockSpec((2 * FN_GROUP_DIM, FN_GROUP_DIM), lambda b, t: (0, 0)),
                  pl.BlockSpec((FN_GROUPS, FN_GROUP_DIM, FN_GROUP_DIM), lambda b, t: (0, 0, 0)),
                  pl.BlockSpec((1, FN_WIDTH), lambda b, t: (0, 0))],
        out_specs=pl.BlockSpec((1, GRID_W, DFT_K1 * FN_WIDTH), lambda b, t: (b, 0, t)),
        out_shape=jax.ShapeDtypeStruct((bsz, GRID_W, R * FN_WIDTH), BF16),
        compiler_params=_cparams(("arbitrary", "arbitrary")),
    )(y4, y4, m_cols, cs, wf, g_fn)


def _fourier(u, bsz, L, cs, wf, g_fn):
    R = L // GRID_W
    y = _dft_rows(_row_stage_table(R), u.reshape(bsz, R, GRID_W, FN_WIDTH))
    fn = _dft_cols(y, _col_stage_table(L), cs, wf, g_fn)
    return fn.reshape(bsz * L, FN_WIDTH)


def _outproj_kernel(na_ref, fn_ref, x_ref, gt1_ref, sh2_ref, sc2_ref, gpost_ref, gpre_ref,
                    woa_ref, wof_ref, wr_ref, br_ref,
                    x1_ref, h2_ref, gate_ref, pos_ref, cnt_ref):
    mix = (jnp.dot(na_ref[...], woa_ref[...], preferred_element_type=F32)
           + jnp.dot(fn_ref[...], wof_ref[...], preferred_element_type=F32))
    x1 = x_ref[...] + gt1_ref[0] * _rms(mix, gpost_ref[...])
    x1_ref[...] = x1
    h2 = _rms(x1, gpre_ref[...]) * (1.0 + sc2_ref[0]) + sh2_ref[0]
    hi = h2.astype(BF16)
    h2_ref[...] = hi

    lo = (h2 - hi.astype(F32)).astype(BF16)
    nt = (((1,), (1,)), ((), ()))
    a = lax.dot_general(wr_ref[...], hi, nt, preferred_element_type=F32)
    b = lax.dot_general(wr_ref[0:N_EXPERTS, :], lo, nt, preferred_element_type=F32)
    logits = a[:N_EXPERTS] + a[N_EXPERTS:] + b + br_ref[:, 0:1]

    eidx = lax.broadcasted_iota(jnp.int32, (N_EXPERTS, TM), 0)
    vals, onehots = [], []
    cur = logits
    for k in range(TOP_K):
        m = jnp.max(cur, axis=0, keepdims=True)
        idx = jnp.min(jnp.where(cur == m, eidx, N_EXPERTS), axis=0, keepdims=True)
        hit = eidx == idx
        vals.append(m)
        onehots.append(hit)
        cur = jnp.where(hit, -jnp.inf, cur)
    exps = [jnp.exp(v - vals[0]) for v in vals]
    den = exps[0] + exps[1] + exps[2] + exps[3]
    for k in range(TOP_K):
        gate_ref[k:k + 1, :] = exps[k] / den

    oh = [h.astype(F32) for h in onehots]
    oh_sum = oh[0] + oh[1] + oh[2] + oh[3]
    s_i = lax.broadcasted_iota(jnp.int32, (TM, TM), 0)
    t_i = lax.broadcasted_iota(jnp.int32, (TM, TM), 1)
    earlier = jnp.where(s_i < t_i, 1.0, 0.0).astype(BF16)
    same_expert_before = jnp.dot(oh_sum.astype(BF16), earlier, preferred_element_type=F32)
    counts = jnp.broadcast_to(jnp.sum(oh_sum, axis=1, keepdims=True), (N_EXPERTS, LANES))
    cnt_ref[...] = counts
    run_tiles = jnp.floor((counts + (SUBLANES - 1)) * (1.0 / SUBLANES))
    e_r = lax.broadcasted_iota(jnp.int32, (N_EXPERTS, N_EXPERTS), 0)
    e_c = lax.broadcasted_iota(jnp.int32, (N_EXPERTS, N_EXPERTS), 1)
    lower = jnp.where(e_c < e_r, 1.0, 0.0).astype(BF16)
    run_start = SUBLANES * jnp.dot(lower, run_tiles.astype(BF16), preferred_element_type=F32)[:, 0:1]
    base = run_start + same_expert_before
    for k in range(TOP_K):
        pos_ref[k:k + 1, :] = jnp.sum(oh[k] * base, axis=0, keepdims=True).astype(jnp.int32)


def _outproj(na2d, fn2d, x2d, ada3, ada_row0, bsz, L, g_post, g_pre, woa, wof, wr2, br):
    T = bsz * L
    tpb = L // TM
    row = lambda j: (lambda i: ((ada_row0 + i // tpb) * 6 + j, 0, 0))
    const2 = lambda i: (0, 0)
    tok = lambda i: (i, 0)
    lanes = lambda i: (0, i)
    return pl.pallas_call(
        _outproj_kernel,
        grid=(T // TM,),
        in_specs=[pl.BlockSpec((TM, NA_WIDTH), tok),
                  pl.BlockSpec((TM, FN_WIDTH), tok),
                  pl.BlockSpec((TM, D_MODEL), tok),
                  pl.BlockSpec((1, 1, D_MODEL), row(2)),
                  pl.BlockSpec((1, 1, D_MODEL), row(3)),
                  pl.BlockSpec((1, 1, D_MODEL), row(4)),
                  pl.BlockSpec((1, D_MODEL), const2),
                  pl.BlockSpec((1, D_MODEL), const2),
                  pl.BlockSpec((NA_WIDTH, D_MODEL), const2),
                  pl.BlockSpec((FN_WIDTH, D_MODEL), const2),
                  pl.BlockSpec((2 * N_EXPERTS, D_MODEL), const2),
                  pl.BlockSpec((N_EXPERTS, LANES), const2)],
        out_specs=[pl.BlockSpec((TM, D_MODEL), tok),
                   pl.BlockSpec((TM, D_MODEL), tok),
                   pl.BlockSpec((TOP_K, TM), lanes),
                   pl.BlockSpec((TOP_K, TM), lanes),
                   pl.BlockSpec((N_EXPERTS, LANES), tok)],
        out_shape=[jax.ShapeDtypeStruct((T, D_MODEL), F32),
                   jax.ShapeDtypeStruct((T, D_MODEL), BF16),
                   jax.ShapeDtypeStruct((TOP_K, T), F32),
                   jax.ShapeDtypeStruct((TOP_K, T), jnp.int32),
                   jax.ShapeDtypeStruct((T // TM * N_EXPERTS, LANES), F32)],
        compiler_params=_cparams(("arbitrary",)),
    )(na2d, fn2d, x2d, ada3, ada3, ada3, g_post, g_pre, woa, wof, wr2, br)


def _zero_fill(pad_lo_ref, pad_n_ref, tail_ref, xs_hbm, zbuf, zsem, n_blocks, wait):
    def run(cp):
        cp.wait() if wait else cp.start()

    def expert(e, c):
        pos = pad_lo_ref[e]
        n = pad_n_ref[e]
        for size in _pow2_sizes(ZERO_ROWS):
            @pl.when((n & size) != 0)
            def _(pos=pos, size=size):
                dst = xs_hbm.at[pl.ds(pl.multiple_of(pos, SUBLANES), size)]
                run(pltpu.make_async_copy(zbuf.at[pl.ds(0, size)], dst, zsem))
            pos = pos + (n & size)
        return c

    lax.fori_loop(0, N_EXPERTS, expert, 0)

    def tail(blk, c):
        for half in range(MOE_BLOCK // ZERO_ROWS):
            row0 = pl.multiple_of(blk * MOE_BLOCK + half * ZERO_ROWS, ZERO_ROWS)
            run(pltpu.make_async_copy(zbuf, xs_hbm.at[pl.ds(row0, ZERO_ROWS)], zsem))
        return c

    lax.fori_loop(tail_ref[0], n_blocks, tail, 0)


def _tile_pieces(cnt_ref, off_ref, dst_ref, tile, local_buf, sorted_hbm, sem, to_sorted, fill_copy):
    def expert(e, c):
        j = tile * N_EXPERTS + e
        n = cnt_ref[j]
        src = off_ref[j]
        dst = dst_ref[j]
        for size in _pow2_sizes(TM):
            @pl.when((n & size) != 0)
            def _(src=src, dst=dst, size=size):
                loc = local_buf.at[pl.ds(pl.multiple_of(src, SUBLANES), size)]
                glob = sorted_hbm.at[pl.ds(pl.multiple_of(dst, SUBLANES), size)]
                cp = pltpu.make_async_copy(loc, glob, sem) if to_sorted else pltpu.make_async_copy(glob, loc, sem)
                cp.start()
            src = src + (n & size)
            dst = dst + (n & size)
        return c

    lax.fori_loop(0, N_EXPERTS, expert, 0)

    last = tile * N_EXPERTS + N_EXPERTS - 1
    moved = off_ref[last] + cnt_ref[last]
    fill = TILE_ROWS - moved
    done = 0
    for size in _pow2_sizes(FILL_ROWS):
        @pl.when((fill & size) != 0)
        def _(done=done, size=size):
            fill_copy(pl.multiple_of(moved + done, SUBLANES), pl.multiple_of(done, SUBLANES), size).start()
        done = done + (fill & size)


def _tile_wait(local_buf, sorted_hbm, sem):
    for c in range(TILE_ROWS // SORT_CHUNK):
        rows = pl.ds(c * SORT_CHUNK, SORT_CHUNK)
        pltpu.make_async_copy(local_buf.at[rows], sorted_hbm.at[rows], sem).wait()


def _dispatch_kernel(pad_lo_ref, pad_n_ref, tail_ref, cnt_ref, off_ref, dst_ref,
                     pos_ref, h2a_ref, h2b_ref, xs_hbm, spare_hbm, sbuf, zbuf, sem, zsem,
                     *, n_blocks, tiles_a, n_tiles):
    i = pl.program_id(0)
    buf = i % 2

    @pl.when(i == 0)
    def _():
        zbuf[...] = jnp.zeros_like(zbuf)
        _zero_fill(pad_lo_ref, pad_n_ref, tail_ref, xs_hbm, zbuf, zsem, n_blocks, wait=False)
        _zero_fill(pad_lo_ref, pad_n_ref, tail_ref, xs_hbm, zbuf, zsem, n_blocks, wait=True)
        for b in range(2):
            pltpu.make_async_copy(zbuf.at[pl.ds(0, FILL_ROWS)], spare_hbm.at[b], zsem).start()
        for b in range(2):
            pltpu.make_async_copy(zbuf.at[pl.ds(0, FILL_ROWS)], spare_hbm.at[b], zsem).wait()

    def fill_copy(local_row, fill_row, size):
        del local_row
        return pltpu.make_async_copy(zbuf.at[pl.ds(0, size)], spare_hbm.at[buf, pl.ds(fill_row, size)], sem.at[buf])

    @pl.when(i >= 2)
    def _():
        _tile_wait(sbuf.at[buf], xs_hbm, sem.at[buf])

    def sort_tile(h2_ref):
        h = h2_ref[...]
        pos = pos_ref[...]
        for c in range(TILE_ROWS // SORT_CHUNK):
            row = lax.broadcasted_iota(jnp.int32, (SORT_CHUNK, TM), 0) + c * SORT_CHUNK
            sel = jnp.zeros((SORT_CHUNK, TM), F32)
            for k in range(TOP_K):
                sel = jnp.where(row == pos[k:k + 1, :], 1.0, sel)
            sbuf[buf, c * SORT_CHUNK:(c + 1) * SORT_CHUNK, :] = jnp.dot(sel.astype(BF16), h,
                                                                        preferred_element_type=F32)

    @pl.when(i < tiles_a)
    def _():
        sort_tile(h2a_ref)

    @pl.when(i >= tiles_a)
    def _():
        sort_tile(h2b_ref)

    _tile_pieces(cnt_ref, off_ref, dst_ref, i, sbuf.at[buf], xs_hbm, sem.at[buf], True, fill_copy)

    @pl.when(i == n_tiles - 1)
    def _():
        if n_tiles >= 2:
            _tile_wait(sbuf.at[1 - buf], xs_hbm, sem.at[1 - buf])
        _tile_wait(sbuf.at[buf], xs_hbm, sem.at[buf])


def _dispatch(pad_lo, pad_n, tail, cnt, off, dst, pos, h2a, h2b, n_blocks):
    tiles_a = h2a.shape[0] // TM
    tiles_b = h2b.shape[0] // TM
    n_tiles = tiles_a + tiles_b
    P = n_blocks * MOE_BLOCK
    kern = functools.partial(_dispatch_kernel, n_blocks=n_blocks, tiles_a=tiles_a, n_tiles=n_tiles)
    return pl.pallas_call(
        kern,
        grid_spec=pltpu.PrefetchScalarGridSpec(
            num_scalar_prefetch=6,
            grid=(n_tiles,),
            in_specs=[pl.BlockSpec((TOP_K, TM), lambda i, *_: (0, i)),
                      pl.BlockSpec((TM, D_MODEL), lambda i, *_: (jnp.minimum(i, tiles_a - 1), 0)),
                      pl.BlockSpec((TM, D_MODEL), lambda i, *_: (jnp.maximum(i - tiles_a, 0), 0))],
            out_specs=[pl.BlockSpec(memory_space=pl.ANY), pl.BlockSpec(memory_space=pl.ANY)],
            scratch_shapes=[pltpu.VMEM((2, TILE_ROWS, D_MODEL), F32),
                            pltpu.VMEM((ZERO_ROWS, D_MODEL), F32),
                            pltpu.SemaphoreType.DMA((2,)),
                            pltpu.SemaphoreType.DMA]),
        out_shape=[jax.ShapeDtypeStruct((P, D_MODEL), F32),
                   jax.ShapeDtypeStruct((2, FILL_ROWS, D_MODEL), F32)],
        compiler_params=_cparams(("arbitrary",)),
    )(pad_lo, pad_n, tail, cnt, off, dst, pos, h2a, h2b)[0]


def _expert_kernel(be_ref, used_ref, x_ref, wgu_ref, bgu_ref, wd_ref, bd_ref, y_ref, wgu_bf, wd_bf):
    i = pl.program_id(0)
    live = i < used_ref[0]

    @pl.when(jnp.logical_or(i == 0, be_ref[i] != be_ref[jnp.maximum(i - 1, 0)]))
    def _():
        wgu_bf[...] = wgu_ref[0].astype(BF16)
        wd_bf[...] = wd_ref[0].astype(BF16)

    @pl.when(live)
    def _():
        for part in range(MOE_BLOCK // EXPERT_ROWS):
            rows = slice(part * EXPERT_ROWS, (part + 1) * EXPERT_ROWS)
            x = x_ref[rows, :].astype(BF16)
            gu = jnp.dot(x, wgu_bf[...], preferred_element_type=F32) + bgu_ref[0]
            glu = jnp.minimum(gu[:, :D_FF], SWIGLU_LIMIT)
            lin = jnp.clip(gu[:, D_FF:], -SWIGLU_LIMIT, SWIGLU_LIMIT)
            act = glu * jax.nn.sigmoid(SWIGLU_ALPHA * glu) * (lin + 1.0)
            y_ref[rows, :] = jnp.dot(act.astype(BF16), wd_bf[...], preferred_element_type=F32) + bd_ref[0]

    @pl.when(jnp.logical_not(live))
    def _():
        y_ref[...] = jnp.zeros_like(y_ref)


def _experts(block_e, n_used, xs, wgu, bgu, wd, bd):
    n_blocks = block_e.shape[0]
    x_map = lambda i, be, used: (jnp.minimum(i, used[0] - 1), 0)
    return pl.pallas_call(
        _expert_kernel,
        grid_spec=pltpu.PrefetchScalarGridSpec(
            num_scalar_prefetch=2,
            grid=(n_blocks,),
            in_specs=[pl.BlockSpec((MOE_BLOCK, D_MODEL), x_map),
                      pl.BlockSpec((1, D_MODEL, 2 * D_FF), lambda i, be, used: (be[i], 0, 0)),
                      pl.BlockSpec((1, 1, 2 * D_FF), lambda i, be, used: (be[i], 0, 0)),
                      pl.BlockSpec((1, D_FF, D_MODEL), lambda i, be, used: (be[i], 0, 0)),
                      pl.BlockSpec((1, 1, D_MODEL), lambda i, be, used: (be[i], 0, 0))],
            out_specs=pl.BlockSpec((MOE_BLOCK, D_MODEL), lambda i, be, used: (i, 0)),
            scratch_shapes=[pltpu.VMEM((D_MODEL, 2 * D_FF), BF16),
                            pltpu.VMEM((D_FF, D_MODEL), BF16)]),
        out_shape=jax.ShapeDtypeStruct(xs.shape, F32),
        compiler_params=_cparams(("arbitrary",)),
    )(block_e, n_used, xs, wgu, bgu, wd, bd)


def _combine_kernel(cnt_ref, off_ref, dst_ref, ys_hbm, pos_ref, gate_ref, x1_ref, gt2_ref, g_ref, o_ref,
                    ybuf, sem, *, tile0, n_tiles):
    i = pl.program_id(0)
    buf = i % 2

    def fetch(tile, b):
        def fill_copy(local_row, fill_row, size):
            return pltpu.make_async_copy(ys_hbm.at[pl.ds(fill_row, size)], ybuf.at[b, pl.ds(local_row, size)],
                                         sem.at[b])
        _tile_pieces(cnt_ref, off_ref, dst_ref, tile, ybuf.at[b], ys_hbm, sem.at[b], False, fill_copy)

    @pl.when(i == 0)
    def _():
        fetch(tile0, 0)

    @pl.when(i + 1 < n_tiles)
    def _():
        fetch(tile0 + i + 1, 1 - buf)

    _tile_wait(ybuf.at[buf], ys_hbm, sem.at[buf])

    pos = pos_ref[...]
    gate = gate_ref[...]
    moe = jnp.zeros((TM, D_MODEL), F32)
    for c in range(TILE_ROWS // SORT_CHUNK):
        col = lax.broadcasted_iota(jnp.int32, (TM, SORT_CHUNK), 1) + c * SORT_CHUNK
        w = jnp.zeros((TM, SORT_CHUNK), F32)
        for k in range(TOP_K):
            w = jnp.where(col == pos[:, k:k + 1], gate[:, k:k + 1], w)
        rows = ybuf[buf, c * SORT_CHUNK:(c + 1) * SORT_CHUNK, :]
        moe = moe + jnp.dot(w.astype(BF16), rows.astype(BF16), preferred_element_type=F32)
    o_ref[...] = x1_ref[...] + gt2_ref[0] * _rms(moe, g_ref[...])


def _combine(cnt, off, dst, ys, tile0, pos_tk, gates_tk, x1, ada3, ada_row0, bsz, L, g_post):
    T = bsz * L
    tpb = L // TM
    n_tiles = T // TM
    return pl.pallas_call(
        functools.partial(_combine_kernel, tile0=tile0, n_tiles=n_tiles),
        grid_spec=pltpu.PrefetchScalarGridSpec(
            num_scalar_prefetch=3,
            grid=(n_tiles,),
            in_specs=[pl.BlockSpec(memory_space=pl.ANY),
                      pl.BlockSpec((TM, TOP_K), lambda i, *_: (i, 0)),
                      pl.BlockSpec((TM, TOP_K), lambda i, *_: (i, 0)),
                      pl.BlockSpec((TM, D_MODEL), lambda i, *_: (i, 0)),
                      pl.BlockSpec((1, 1, D_MODEL), lambda i, *_: ((ada_row0 + i // tpb) * 6 + 5, 0, 0)),
                      pl.BlockSpec((1, D_MODEL), lambda i, *_: (0, 0))],
            out_specs=pl.BlockSpec((TM, D_MODEL), lambda i, *_: (i, 0)),
            scratch_shapes=[pltpu.VMEM((2, TILE_ROWS, D_MODEL), F32),
                            pltpu.SemaphoreType.DMA((2,))]),
        out_shape=jax.ShapeDtypeStruct((T, D_MODEL), F32),
        compiler_params=_cparams(("arbitrary",)),
    )(cnt, off, dst, ys, pos_tk, gates_tk, x1, ada3, g_post)


def kernel(x_prompt, x_sample, c_prompt, c_sample, w_ada, b_ada, g_pre_mix, g_post_mix, g_pre_ffn, g_post_ffn,
           w_in, rpb, w_fourier, g_na, g_fn, w_out, w_router, b_router, w_gate_up, b_gate_up, w_down, b_down):
    assert w_ada.shape[0] == 1, "single layer"
    groups = [(x_prompt, c_prompt), (x_sample, c_sample)]
    for x, _ in groups:
        assert x.shape[1] % ATT_TOKENS == 0 and x.shape[1] // GRID_W >= 12 and x.shape[2] == D_MODEL

    row2 = lambda a: a[0].reshape(1, -1)
    w_in0 = w_in[0]
    wq = (w_in0[:, :NA_WIDTH] * (HEAD_DIM ** -0.5 * LOG2_E)).astype(BF16)
    wkt = w_in0[:, NA_WIDTH:2 * NA_WIDTH].T.astype(BF16)
    wv = w_in0[:, 2 * NA_WIDTH:3 * NA_WIDTH].astype(BF16)
    wu = w_in0[:, 3 * NA_WIDTH:].astype(BF16)
    cs = _channel_table()
    bias = _bias_table(rpb[0])
    wf = w_fourier[0].astype(BF16)
    woa = w_out[0, :NA_WIDTH].astype(BF16)
    wof = w_out[0, NA_WIDTH:].astype(BF16)
    wr_t = w_router[0].T
    wr_hi = wr_t.astype(BF16)
    wr_lo = (wr_t - wr_hi.astype(F32)).astype(BF16)
    wr2 = jnp.concatenate([wr_hi, wr_lo], axis=0)
    br = jnp.broadcast_to(b_router[0][:, None], (N_EXPERTS, LANES))
    wgu = w_gate_up[0]
    wd = w_down[0]
    bgu = b_gate_up[0].reshape(N_EXPERTS, 1, 2 * D_FF)
    bd = b_down[0].reshape(N_EXPERTS, 1, D_MODEL)

    c_all = jnp.concatenate([c_prompt, c_sample], axis=0)
    ada3 = _ada(c_all, w_ada[0], b_ada[0]).reshape(c_all.shape[0] * 6, 1, D_MODEL)

    per_group = []
    ada_row0 = 0
    for x, _ in groups:
        bsz, L, _ = x.shape
        x2d = x.reshape(bsz * L, D_MODEL)
        q, kt4, v, u = _inproj(x2d, ada3, ada_row0, bsz, L, row2(g_pre_mix), wq, wkt, wv, wu)
        na = _attention(q.reshape(bsz, L, NA_WIDTH), kt4, v.reshape(bsz, L, NA_WIDTH), bias, row2(g_na))
        fn = _fourier(u, bsz, L, cs, wf, row2(g_fn))
        x1, h2, gate_t, pos_t, cnt = _outproj(
            na.reshape(bsz * L, NA_WIDTH), fn, x2d, ada3, ada_row0, bsz, L,
            row2(g_post_mix), row2(g_pre_ffn), woa, wof, wr2, br)
        per_group.append((x1, h2, gate_t, pos_t, cnt, bsz, L, ada_row0))
        ada_row0 += bsz

    T = sum(g[5] * g[6] for g in per_group)
    max_rows = T * TOP_K + (T // TM) * N_EXPERTS * (SUBLANES - 1) + N_EXPERTS * (MOE_BLOCK - 1)
    n_blocks = -(-max_rows // MOE_BLOCK)
    tile_cnt = jnp.concatenate([g[4][:, 0].reshape(-1, N_EXPERTS) for g in per_group], axis=0).astype(jnp.int32)
    tile_cnt = _round_up(tile_cnt, SUBLANES)
    total = jnp.sum(tile_cnt, axis=0)
    padded = ((total + MOE_BLOCK - 1) // MOE_BLOCK) * MOE_BLOCK
    pend = jnp.cumsum(padded)
    pstart = pend - padded
    run = jnp.cumsum(tile_cnt, axis=0) - tile_cnt
    tile_dst = (pstart[None, :] + run).reshape(-1)
    tile_off = (jnp.cumsum(tile_cnt, axis=1) - tile_cnt).reshape(-1)
    tile_cnt = tile_cnt.reshape(-1)
    block_row0 = jnp.arange(n_blocks, dtype=jnp.int32) * MOE_BLOCK
    block_e = jnp.minimum(jnp.sum((pend[None, :] <= block_row0[:, None]).astype(jnp.int32), axis=1), N_EXPERTS - 1)
    pad_lo = pstart + total
    pad_n = padded - total
    tail = (pend[-1:] // MOE_BLOCK).astype(jnp.int32)

    pos_all = jnp.concatenate([g[3] for g in per_group], axis=1)
    xs = _dispatch(pad_lo, pad_n, tail, tile_cnt, tile_off, tile_dst, pos_all,
                   per_group[0][1], per_group[1][1], n_blocks)
    ys = _experts(block_e, tail, xs, wgu, bgu, wd, bd)

    outs = []
    tile0 = 0
    for x1, _, gate_t, pos_t, _, bsz, L, row0 in per_group:
        out = _combine(tile_cnt, tile_off, tile_dst, ys, tile0, pos_t.T, gate_t.T, x1, ada3, row0, bsz, L,
                       row2(g_post_ffn))
        outs.append(out.reshape(bsz, L, D_MODEL))
        tile0 += bsz * L // TM
    return tuple(outs)
```

```python
import functools

import numpy as np
import jax
import jax.numpy as jnp
from jax import lax
from jax.experimental import pallas as pl
from jax.experimental.pallas import tpu as pltpu

F32 = jnp.float32
BF16 = jnp.bfloat16

D_MODEL = 1024
GRID_W = 64
NA_WIDTH = 512
HEAD_DIM = 64
NA_HEADS = 8
FN_WIDTH = 512
FN_GROUPS = 4
FN_GROUP_DIM = 128
WIN_ROWS = 8
WIN_COLS = 16
N_EXPERTS = 32
TOP_K = 4
D_FF = 1024
SWIGLU_ALPHA = 1.702
SWIGLU_LIMIT = 7.0
MOE_BLOCK = 512
EXPERT_ROWS = 256
X_RING = 3
EPS = 1e-6

LANES = 128
PAIR_TOKENS = 2 * GRID_W
SLAB_ROWS = 10
SLAB_TOKENS = SLAB_ROWS * GRID_W
N_BIAS_VARIANTS = 5
MASK_VALUE = -1e30
LOG2_E = float(np.log2(np.e))

TM = 512
SUBLANES = 8
SORT_CHUNK = 256
TILE_ROWS = -(-(TM * TOP_K + N_EXPERTS * (SUBLANES - 1)) // SORT_CHUNK) * SORT_CHUNK
FILL_ROWS = TILE_ROWS - TM * TOP_K
ATT_TOKENS = 1024
ATT_PAIRS = 4
DFT_COLS = 16
DFT_K1 = 16
ZERO_ROWS = MOE_BLOCK // 2
VMEM_LIMIT = 56 * 1024 * 1024


def _cparams(sem):
    return pltpu.CompilerParams(dimension_semantics=sem, vmem_limit_bytes=VMEM_LIMIT)


def _rms(x, g):
    return x * lax.rsqrt(jnp.mean(x * x, axis=-1, keepdims=True) + EPS) * g


def _pow2_sizes(limit):
    sizes = []
    while limit >= SUBLANES:
        sizes.append(limit)
        limit //= 2
    return sizes


def _round_up(n, m):
    return (n + m - 1) // m * m


def _ada_kernel(c_ref, w_ref, b_ref, o_ref):
    c = c_ref[...]
    s = c * jax.nn.sigmoid(c)
    o_ref[...] = jnp.dot(s.astype(BF16), w_ref[...].astype(BF16), preferred_element_type=F32) + b_ref[...]


def _ada(c_all, w_ada, b_ada):
    nb = c_all.shape[0]
    n_out = w_ada.shape[1]
    tn = 1536
    return pl.pallas_call(
        _ada_kernel,
        grid=(n_out // tn,),
        in_specs=[pl.BlockSpec((nb, D_MODEL), lambda j: (0, 0)),
                  pl.BlockSpec((D_MODEL, tn), lambda j: (0, j)),
                  pl.BlockSpec((1, tn), lambda j: (0, j))],
        out_specs=pl.BlockSpec((nb, tn), lambda j: (0, j)),
        out_shape=jax.ShapeDtypeStruct((nb, n_out), F32),
        compiler_params=_cparams(("arbitrary",)),
    )(c_all, w_ada, b_ada.reshape(1, n_out))


def _inproj_kernel(x_ref, sh_ref, sc_ref, g_ref, wq_ref, wkt_ref, wv_ref, wu_ref,
                   q_ref, kt_ref, v_ref, u_ref):
    h = _rms(x_ref[...], g_ref[...]) * (1.0 + sc_ref[0]) + sh_ref[0]
    hb = h.astype(BF16)
    q_ref[...] = jnp.dot(hb, wq_ref[...], preferred_element_type=F32).astype(BF16)
    v_ref[...] = jnp.dot(hb, wv_ref[...], preferred_element_type=F32).astype(BF16)
    kt = lax.dot_general(wkt_ref[...], hb, (((1,), (1,)), ((), ())), preferred_element_type=F32).astype(BF16)
    for c in range(TM // PAIR_TOKENS):
        kt_ref[0, c] = kt[:, c * PAIR_TOKENS:(c + 1) * PAIR_TOKENS]
    u_ref[...] = jnp.dot(hb, wu_ref[...], preferred_element_type=F32).astype(BF16)


def _inproj(x2d, ada3, ada_row0, bsz, L, g_pre, wq, wkt, wv, wu):
    T = bsz * L
    tpb = L // TM
    row = lambda j: (lambda i: ((ada_row0 + i // tpb) * 6 + j, 0, 0))
    const2 = lambda i: (0, 0)
    return pl.pallas_call(
        _inproj_kernel,
        grid=(T // TM,),
        in_specs=[pl.BlockSpec((TM, D_MODEL), lambda i: (i, 0)),
                  pl.BlockSpec((1, 1, D_MODEL), row(0)),
                  pl.BlockSpec((1, 1, D_MODEL), row(1)),
                  pl.BlockSpec((1, D_MODEL), const2),
                  pl.BlockSpec((D_MODEL, NA_WIDTH), const2),
                  pl.BlockSpec((NA_WIDTH, D_MODEL), const2),
                  pl.BlockSpec((D_MODEL, NA_WIDTH), const2),
                  pl.BlockSpec((D_MODEL, FN_WIDTH), const2)],
        out_specs=[pl.BlockSpec((TM, NA_WIDTH), lambda i: (i, 0)),
                   pl.BlockSpec((1, TM // PAIR_TOKENS, NA_WIDTH, PAIR_TOKENS),
                                lambda i: (i // tpb, i % tpb, 0, 0)),
                   pl.BlockSpec((TM, NA_WIDTH), lambda i: (i, 0)),
                   pl.BlockSpec((TM, FN_WIDTH), lambda i: (i, 0))],
        out_shape=[jax.ShapeDtypeStruct((T, NA_WIDTH), BF16),
                   jax.ShapeDtypeStruct((bsz, L // PAIR_TOKENS, NA_WIDTH, PAIR_TOKENS), BF16),
                   jax.ShapeDtypeStruct((T, NA_WIDTH), BF16),
                   jax.ShapeDtypeStruct((T, FN_WIDTH), BF16)],
        compiler_params=_cparams(("arbitrary",)),
    )(x2d, ada3, ada3, g_pre, wq, wkt, wv, wu)


def _bias_table(rpb):
    rows = 32
    n_pairs = rows // 2
    c = np.arange(GRID_W)
    kc0 = np.clip(c - WIN_COLS // 2, 0, GRID_W - WIN_COLS)
    col_ok = (c[None, :] >= kc0[:, None]) & (c[None, :] < kc0[:, None] + WIN_COLS)
    ci = c[None, :] - c[:, None] + WIN_COLS - 1
    pick = ((ci[None] == np.arange(2 * WIN_COLS - 1)[:, None, None]) & col_ok[None]).astype(np.float32)
    toe = jnp.einsum('hrk,kcd->hrcd', rpb, jnp.asarray(pick), precision=lax.Precision.HIGHEST)
    toe = jnp.where(col_ok, toe * LOG2_E, MASK_VALUE)
    masked = jnp.full((NA_HEADS, GRID_W, GRID_W), MASK_VALUE, F32)
    variants = []
    for j in (0, 1, 2, n_pairs - 2, n_pairs - 1):
        start = int(np.clip(2 * j - 4, 0, rows - SLAB_ROWS))
        halves = []
        for rl in range(2):
            r = 2 * j + rl
            kr0 = int(np.clip(r - WIN_ROWS // 2, 0, rows - WIN_ROWS))
            blocks = []
            for i in range(SLAB_ROWS):
                kr = start + i
                blocks.append(toe[:, kr - r + WIN_ROWS - 1] if kr0 <= kr < kr0 + WIN_ROWS else masked)
            halves.append(jnp.concatenate(blocks, axis=2))
        variants.append(jnp.concatenate(halves, axis=1))
    return jnp.stack(variants)


def _attn_kernel(q_ref, kt_ref, v_ref, bias_ref, g_ref, o_ref, *, n_pairs):
    step = pl.program_id(1)
    pairs_per_step = ATT_TOKENS // PAIR_TOKENS
    lane = lax.broadcasted_iota(jnp.int32, (PAIR_TOKENS, 2 * HEAD_DIM), 1)
    first_head = lane < HEAD_DIM

    n_hp = NA_HEADS // 2
    dims = [slice(hp * 2 * HEAD_DIM, (hp + 1) * 2 * HEAD_DIM) for hp in range(n_hp)]

    def pairs_body(it, carry):
        pair = []
        for pi in range(ATT_PAIRS):
            p = it * ATT_PAIRS + pi
            j = step * pairs_per_step + p
            start2 = jnp.clip(j - 2, 0, n_pairs - SLAB_ROWS // 2)
            variant = jnp.where(j < 2, j, jnp.where(j >= n_pairs - 2, j - (n_pairs - 2) + 3, 2))
            row0 = pl.multiple_of(p * PAIR_TOKENS, PAIR_TOKENS)
            qp = q_ref[0, pl.ds(row0, PAIR_TOKENS), :]
            kt5 = kt_ref[0, pl.ds(start2, SLAB_ROWS // 2)]
            vs = v_ref[0, pl.ds(pl.multiple_of(start2 * PAIR_TOKENS, PAIR_TOKENS), SLAB_TOKENS), :]
            pair.append((row0, variant, qp, kt5, vs))
        units = [(pi, hp) for pi in range(ATT_PAIRS) for hp in range(n_hp)]
        scores, probs, sums, outs = {}, {}, {}, {}

        def score_phase(u):
            pi, hp = u
            _, _, qp, kt5, _ = pair[pi]
            q2 = qp[:, dims[hp]]
            kt2 = jnp.concatenate([kt5[c, dims[hp], :] for c in range(SLAB_ROWS // 2)], axis=1)
            zero = jnp.zeros_like(q2)
            qm = jnp.concatenate([jnp.where(first_head, q2, zero), jnp.where(first_head, zero, q2)], axis=0)
            scores[u] = jnp.dot(qm, kt2, preferred_element_type=F32)

        def softmax_phase(u):
            pi, hp = u
            variant = pair[pi][1]
            b2 = jnp.concatenate([bias_ref[variant, hp * 2], bias_ref[variant, hp * 2 + 1]], axis=0)
            s = scores.pop(u) + b2
            e = jnp.exp2(s - jnp.max(s, axis=-1, keepdims=True))
            sums[u] = jnp.sum(e, axis=-1, keepdims=True)
            probs[u] = e.astype(BF16)

        def value_phase(u):
            pi, hp = u
            vs = pair[pi][4]
            o = jnp.dot(probs.pop(u), vs[:, dims[hp]], preferred_element_type=F32) / sums.pop(u)
            outs[u] = jnp.where(first_head, o[:PAIR_TOKENS], o[PAIR_TOKENS:])

        for t in range(len(units) + 2):
            if t < len(units):
                score_phase(units[t])
            if 0 <= t - 1 < len(units):
                softmax_phase(units[t - 1])
            if 0 <= t - 2 < len(units):
                value_phase(units[t - 2])
        for pi in range(ATT_PAIRS):
            a = jnp.concatenate([outs[(pi, hp)] for hp in range(n_hp)], axis=1)
            o_ref[0, pl.ds(pair[pi][0], PAIR_TOKENS), :] = _rms(a, g_ref[...]).astype(BF16)
        return carry

    lax.fori_loop(0, pairs_per_step // ATT_PAIRS, pairs_body, 0)


def _attention(q3, kt4, v3, bias, g_na):
    bsz, L, _ = q3.shape
    n_pairs = L // PAIR_TOKENS
    return pl.pallas_call(
        functools.partial(_attn_kernel, n_pairs=n_pairs),
        grid=(bsz, L // ATT_TOKENS),
        in_specs=[pl.BlockSpec((1, ATT_TOKENS, NA_WIDTH), lambda b, s: (b, s, 0)),
                  pl.BlockSpec((1, n_pairs, NA_WIDTH, PAIR_TOKENS), lambda b, s: (b, 0, 0, 0)),
                  pl.BlockSpec((1, L, NA_WIDTH), lambda b, s: (b, 0, 0)),
                  pl.BlockSpec((N_BIAS_VARIANTS, NA_HEADS, PAIR_TOKENS, SLAB_TOKENS), lambda b, s: (0, 0, 0, 0),
                               pipeline_mode=pl.Buffered(1)),
                  pl.BlockSpec((1, NA_WIDTH), lambda b, s: (0, 0))],
        out_specs=pl.BlockSpec((1, ATT_TOKENS, NA_WIDTH), lambda b, s: (b, s, 0)),
        out_shape=jax.ShapeDtypeStruct((bsz, L, NA_WIDTH), BF16),
        compiler_params=_cparams(("arbitrary", "arbitrary")),
    )(q3, kt4, v3, bias, g_na)


def _angles(j, k, n):
    return ((j * k) % n).astype(np.float64) * (2.0 * np.pi / n)


def _row_stage_table(R):
    k = np.arange(R)
    ang = _angles(k[:, None], k[None, :], R)
    f = np.concatenate([np.cos(ang), -np.sin(ang)], axis=0)
    return jnp.asarray(np.kron(f, np.eye(DFT_COLS)), dtype=BF16)


def _col_stage_table(L):
    R = L // GRID_W
    k1 = np.arange(R)[:, None, None]
    k2 = np.arange(GRID_W)[None, :, None]
    c = np.arange(GRID_W)[None, None, :]
    ang = _angles(c, k1 + R * k2, L)
    cc, ss = np.cos(ang), np.sin(ang)
    top = np.concatenate([cc, ss], axis=2)
    bot = np.concatenate([-ss, cc], axis=2)
    return jnp.asarray(np.concatenate([top, bot], axis=1), dtype=BF16)


def _channel_table():
    j = np.arange(FN_GROUP_DIM)
    ang = _angles(j[:, None], j[None, :], FN_GROUP_DIM)
    return jnp.asarray(np.concatenate([np.cos(ang), np.sin(ang)], axis=0), dtype=BF16)


def _dft_rows_kernel(f_ref, x_ref, y_ref):
    R = x_ref.shape[1]
    x = x_ref[0].reshape(R * DFT_COLS, FN_WIDTH)
    y = jnp.dot(f_ref[...], x, preferred_element_type=F32).astype(BF16)
    y_ref[0] = y.reshape(2 * R, DFT_COLS, FN_WIDTH)


def _dft_rows(f_rows, u4):
    bsz, R, _, _ = u4.shape
    return pl.pallas_call(
        _dft_rows_kernel,
        grid=(bsz, GRID_W // DFT_COLS),
        in_specs=[pl.BlockSpec((2 * R * DFT_COLS, R * DFT_COLS), lambda b, t: (0, 0)),
                  pl.BlockSpec((1, R, DFT_COLS, FN_WIDTH), lambda b, t: (b, 0, t, 0))],
        out_specs=pl.BlockSpec((1, 2 * R, DFT_COLS, FN_WIDTH), lambda b, t: (b, 0, t, 0)),
        out_shape=jax.ShapeDtypeStruct((bsz, 2 * R, GRID_W, FN_WIDTH), BF16),
        compiler_params=_cparams(("arbitrary", "arbitrary")),
    )(f_rows, u4)


def _dft_cols_kernel(yr_ref, yi_ref, m_ref, cs_ref, wf_ref, g_ref, o_ref):
    re, im = [], []
    for j in range(DFT_K1):
        y = jnp.concatenate([yr_ref[0, j], yi_ref[0, j]], axis=0)
        x = jnp.dot(m_ref[j], y, preferred_element_type=F32)
        re.append(x[:GRID_W])
        im.append(x[GRID_W:])
    a = jnp.concatenate(re, axis=0).astype(BF16)
    b = jnp.concatenate(im, axis=0).astype(BF16)
    ys = []
    for g in range(FN_GROUPS):
        sl = slice(g * FN_GROUP_DIM, (g + 1) * FN_GROUP_DIM)
        ab = jnp.concatenate([a[:, sl], b[:, sl]], axis=1)
        z = jnp.dot(ab, cs_ref[...], preferred_element_type=F32).astype(BF16)
        ys.append(jnp.dot(z, wf_ref[g], preferred_element_type=F32))
    out = _rms(jnp.concatenate(ys, axis=1), g_ref[...]).astype(BF16)
    for j in range(DFT_K1):
        o_ref[0, :, j * FN_WIDTH:(j + 1) * FN_WIDTH] = out[j * GRID_W:(j + 1) * GRID_W, :]


def _dft_cols(y4, m_cols, cs, wf, g_fn):
    bsz, R2, _, _ = y4.shape
    R = R2 // 2
    steps = R // DFT_K1
    return pl.pallas_call(
        _dft_cols_kernel,
        grid=(bsz, steps),
        in_specs=[pl.BlockSpec((1, DFT_K1, GRID_W, FN_WIDTH), lambda b, t: (b, t, 0, 0)),
                  pl.BlockSpec((1, DFT_K1, GRID_W, FN_WIDTH), lambda b, t: (b, steps + t, 0, 0)),
                  pl.BlockSpec((DFT_K1, 2 * GRID_W, 2 * GRID_W), lambda b, t: (t, 0, 0)),
                  pl.BlockSpec((2 * FN_GROUP_DIM, FN_GROUP_DIM), lambda b, t: (0, 0)),
                  pl.BlockSpec((FN_GROUPS, FN_GROUP_DIM, FN_GROUP_DIM), lambda b, t: (0, 0, 0)),
                  pl.BlockSpec((1, FN_WIDTH), lambda b, t: (0, 0))],
        out_specs=pl.BlockSpec((1, GRID_W, DFT_K1 * FN_WIDTH), lambda b, t: (b, 0, t)),
        out_shape=jax.ShapeDtypeStruct((bsz, GRID_W, R * FN_WIDTH), BF16),
        compiler_params=_cparams(("arbitrary", "arbitrary")),
    )(y4, y4, m_cols, cs, wf, g_fn)


def _fourier(u, bsz, L, cs, wf, g_fn):
    R = L // GRID_W
    y = _dft_rows(_row_stage_table(R), u.reshape(bsz, R, GRID_W, FN_WIDTH))
    fn = _dft_cols(y, _col_stage_table(L), cs, wf, g_fn)
    return fn.reshape(bsz * L, FN_WIDTH)


def _outproj_kernel(na_ref, fn_ref, x_ref, gt1_ref, sh2_ref, sc2_ref, gpost_ref, gpre_ref,
                    woa_ref, wof_ref, wr_ref, br_ref,
                    x1_ref, h2_ref, gate_ref, pos_ref, cnt_ref):
    mix = (jnp.dot(na_ref[...], woa_ref[...], preferred_element_type=F32)
           + jnp.dot(fn_ref[...], wof_ref[...], preferred_element_type=F32))
    x1 = x_ref[...] + gt1_ref[0] * _rms(mix, gpost_ref[...])
    x1_ref[...] = x1
    h2 = _rms(x1, gpre_ref[...]) * (1.0 + sc2_ref[0]) + sh2_ref[0]
    hi = h2.astype(BF16)
    h2_ref[...] = hi

    lo = (h2 - hi.astype(F32)).astype(BF16)
    nt = (((1,), (1,)), ((), ()))
    a = lax.dot_general(wr_ref[...], hi, nt, preferred_element_type=F32)
    b = lax.dot_general(wr_ref[0:N_EXPERTS, :], lo, nt, preferred_element_type=F32)
    logits = a[:N_EXPERTS] + a[N_EXPERTS:] + b + br_ref[:, 0:1]

    eidx = lax.broadcasted_iota(jnp.int32, (N_EXPERTS, TM), 0)
    vals, onehots = [], []
    cur = logits
    for k in range(TOP_K):
        m = jnp.max(cur, axis=0, keepdims=True)
        idx = jnp.min(jnp.where(cur == m, eidx, N_EXPERTS), axis=0, keepdims=True)
        hit = eidx == idx
        vals.append(m)
        onehots.append(hit)
        cur = jnp.where(hit, -jnp.inf, cur)
    exps = [jnp.exp(v - vals[0]) for v in vals]
    den = exps[0] + exps[1] + exps[2] + exps[3]
    for k in range(TOP_K):
        gate_ref[k:k + 1, :] = exps[k] / den

    oh = [h.astype(F32) for h in onehots]
    oh_sum = oh[0] + oh[1] + oh[2] + oh[3]
    s_i = lax.broadcasted_iota(jnp.int32, (TM, TM), 0)
    t_i = lax.broadcasted_iota(jnp.int32, (TM, TM), 1)
    earlier = jnp.where(s_i < t_i, 1.0, 0.0).astype(BF16)
    same_expert_before = jnp.dot(oh_sum.astype(BF16), earlier, preferred_element_type=F32)
    counts = jnp.broadcast_to(jnp.sum(oh_sum, axis=1, keepdims=True), (N_EXPERTS, LANES))
    cnt_ref[...] = counts
    run_tiles = jnp.floor((counts + (SUBLANES - 1)) * (1.0 / SUBLANES))
    e_r = lax.broadcasted_iota(jnp.int32, (N_EXPERTS, N_EXPERTS), 0)
    e_c = lax.broadcasted_iota(jnp.int32, (N_EXPERTS, N_EXPERTS), 1)
    lower = jnp.where(e_c < e_r, 1.0, 0.0).astype(BF16)
    run_start = SUBLANES * jnp.dot(lower, run_tiles.astype(BF16), preferred_element_type=F32)[:, 0:1]
    base = run_start + same_expert_before
    for k in range(TOP_K):
        pos_ref[k:k + 1, :] = jnp.sum(oh[k] * base, axis=0, keepdims=True).astype(jnp.int32)


def _outproj(na2d, fn2d, x2d, ada3, ada_row0, bsz, L, g_post, g_pre, woa, wof, wr2, br):
    T = bsz * L
    tpb = L // TM
    row = lambda j: (lambda i: ((ada_row0 + i // tpb) * 6 + j, 0, 0))
    const2 = lambda i: (0, 0)
    tok = lambda i: (i, 0)
    lanes = lambda i: (0, i)
    return pl.pallas_call(
        _outproj_kernel,
        grid=(T // TM,),
        in_specs=[pl.BlockSpec((TM, NA_WIDTH), tok),
                  pl.BlockSpec((TM, FN_WIDTH), tok),
                  pl.BlockSpec((TM, D_MODEL), tok),
                  pl.BlockSpec((1, 1, D_MODEL), row(2)),
                  pl.BlockSpec((1, 1, D_MODEL), row(3)),
                  pl.BlockSpec((1, 1, D_MODEL), row(4)),
                  pl.BlockSpec((1, D_MODEL), const2),
                  pl.BlockSpec((1, D_MODEL), const2),
                  pl.BlockSpec((NA_WIDTH, D_MODEL), const2),
                  pl.BlockSpec((FN_WIDTH, D_MODEL), const2),
                  pl.BlockSpec((2 * N_EXPERTS, D_MODEL), const2),
                  pl.BlockSpec((N_EXPERTS, LANES), const2)],
        out_specs=[pl.BlockSpec((TM, D_MODEL), tok),
                   pl.BlockSpec((TM, D_MODEL), tok),
                   pl.BlockSpec((TOP_K, TM), lanes),
                   pl.BlockSpec((TOP_K, TM), lanes),
                   pl.BlockSpec((N_EXPERTS, LANES), tok)],
        out_shape=[jax.ShapeDtypeStruct((T, D_MODEL), F32),
                   jax.ShapeDtypeStruct((T, D_MODEL), BF16),
                   jax.ShapeDtypeStruct((TOP_K, T), F32),
                   jax.ShapeDtypeStruct((TOP_K, T), jnp.int32),
                   jax.ShapeDtypeStruct((T // TM * N_EXPERTS, LANES), F32)],
        compiler_params=_cparams(("arbitrary",)),
    )(na2d, fn2d, x2d, ada3, ada3, ada3, g_post, g_pre, woa, wof, wr2, br)


def _zero_fill(pad_lo_ref, pad_n_ref, tail_ref, xs_hbm, zbuf, zsem, n_blocks, wait):
    def run(cp):
        cp.wait() if wait else cp.start()

    def expert(e, c):
        pos = pad_lo_ref[e]
        n = pad_n_ref[e]
        for size in _pow2_sizes(ZERO_ROWS):
            @pl.when((n & size) != 0)
            def _(pos=pos, size=size):
                dst = xs_hbm.at[pl.ds(pl.multiple_of(pos, SUBLANES), size)]
                run(pltpu.make_async_copy(zbuf.at[pl.ds(0, size)], dst, zsem))
            pos = pos + (n & size)
        return c

    lax.fori_loop(0, N_EXPERTS, expert, 0)

    def tail(blk, c):
        for half in range(MOE_BLOCK // ZERO_ROWS):
            row0 = pl.multiple_of(blk * MOE_BLOCK + half * ZERO_ROWS, ZERO_ROWS)
            run(pltpu.make_async_copy(zbuf, xs_hbm.at[pl.ds(row0, ZERO_ROWS)], zsem))
        return c

    lax.fori_loop(tail_ref[0], n_blocks, tail, 0)


def _tile_pieces(cnt_ref, off_ref, dst_ref, tile, local_buf, sorted_hbm, sem, to_sorted, fill_copy):
    def expert(e, c):
        j = tile * N_EXPERTS + e
        n = cnt_ref[j]
        src = off_ref[j]
        dst = dst_ref[j]
        for size in _pow2_sizes(TM):
            @pl.when((n & size) != 0)
            def _(src=src, dst=dst, size=size):
                loc = local_buf.at[pl.ds(pl.multiple_of(src, SUBLANES), size)]
                glob = sorted_hbm.at[pl.ds(pl.multiple_of(dst, SUBLANES), size)]
                cp = pltpu.make_async_copy(loc, glob, sem) if to_sorted else pltpu.make_async_copy(glob, loc, sem)
                cp.start()
            src = src + (n & size)
            dst = dst + (n & size)
        return c

    lax.fori_loop(0, N_EXPERTS, expert, 0)

    last = tile * N_EXPERTS + N_EXPERTS - 1
    moved = off_ref[last] + cnt_ref[last]
    fill = TILE_ROWS - moved
    done = 0
    for size in _pow2_sizes(FILL_ROWS):
        @pl.when((fill & size) != 0)
        def _(done=done, size=size):
            fill_copy(pl.multiple_of(moved + done, SUBLANES), pl.multiple_of(done, SUBLANES), size).start()
        done = done + (fill & size)


def _tile_wait(local_buf, sorted_hbm, sem):
    for c in range(TILE_ROWS // SORT_CHUNK):
        rows = pl.ds(c * SORT_CHUNK, SORT_CHUNK)
        pltpu.make_async_copy(local_buf.at[rows], sorted_hbm.at[rows], sem).wait()


def _dispatch_kernel(pad_lo_ref, pad_n_ref, tail_ref, cnt_ref, off_ref, dst_ref,
                     pos_ref, h2a_ref, h2b_ref, xs_hbm, spare_hbm, sbuf, zbuf, sem, zsem,
                     *, n_blocks, tiles_a, n_tiles):
    i = pl.program_id(0)
    buf = i % 2

    @pl.when(i == 0)
    def _():
        zbuf[...] = jnp.zeros_like(zbuf)
        _zero_fill(pad_lo_ref, pad_n_ref, tail_ref, xs_hbm, zbuf, zsem, n_blocks, wait=False)
        _zero_fill(pad_lo_ref, pad_n_ref, tail_ref, xs_hbm, zbuf, zsem, n_blocks, wait=True)
        for b in range(2):
            pltpu.make_async_copy(zbuf.at[pl.ds(0, FILL_ROWS)], spare_hbm.at[b], zsem).start()
        for b in range(2):
            pltpu.make_async_copy(zbuf.at[pl.ds(0, FILL_ROWS)], spare_hbm.at[b], zsem).wait()

    def fill_copy(local_row, fill_row, size):
        del local_row
        return pltpu.make_async_copy(zbuf.at[pl.ds(0, size)], spare_hbm.at[buf, pl.ds(fill_row, size)], sem.at[buf])

    @pl.when(i >= 2)
    def _():
        _tile_wait(sbuf.at[buf], xs_hbm, sem.at[buf])

    def sort_tile(h2_ref):
        h = h2_ref[...]
        pos = pos_ref[...]
        for c in range(TILE_ROWS // SORT_CHUNK):
            row = lax.broadcasted_iota(jnp.int32, (SORT_CHUNK, TM), 0) + c * SORT_CHUNK
            sel = jnp.zeros((SORT_CHUNK, TM), F32)
            for k in range(TOP_K):
                sel = jnp.where(row == pos[k:k + 1, :], 1.0, sel)
            sbuf[buf, c * SORT_CHUNK:(c + 1) * SORT_CHUNK, :] = jnp.dot(sel.astype(BF16), h,
                                                                        preferred_element_type=F32)

    @pl.when(i < tiles_a)
    def _():
        sort_tile(h2a_ref)

    @pl.when(i >= tiles_a)
    def _():
        sort_tile(h2b_ref)

    _tile_pieces(cnt_ref, off_ref, dst_ref, i, sbuf.at[buf], xs_hbm, sem.at[buf], True, fill_copy)

    @pl.when(i == n_tiles - 1)
    def _():
        if n_tiles >= 2:
            _tile_wait(sbuf.at[1 - buf], xs_hbm, sem.at[1 - buf])
        _tile_wait(sbuf.at[buf], xs_hbm, sem.at[buf])


def _dispatch(pad_lo, pad_n, tail, cnt, off, dst, pos, h2a, h2b, n_blocks):
    tiles_a = h2a.shape[0] // TM
    tiles_b = h2b.shape[0] // TM
    n_tiles = tiles_a + tiles_b
    P = n_blocks * MOE_BLOCK
    kern = functools.partial(_dispatch_kernel, n_blocks=n_blocks, tiles_a=tiles_a, n_tiles=n_tiles)
    return pl.pallas_call(
        kern,
        grid_spec=pltpu.PrefetchScalarGridSpec(
            num_scalar_prefetch=6,
            grid=(n_tiles,),
            in_specs=[pl.BlockSpec((TOP_K, TM), lambda i, *_: (0, i)),
                      pl.BlockSpec((TM, D_MODEL), lambda i, *_: (jnp.minimum(i, tiles_a - 1), 0)),
                      pl.BlockSpec((TM, D_MODEL), lambda i, *_: (jnp.maximum(i - tiles_a, 0), 0))],
            out_specs=[pl.BlockSpec(memory_space=pl.ANY), pl.BlockSpec(memory_space=pl.ANY)],
            scratch_shapes=[pltpu.VMEM((2, TILE_ROWS, D_MODEL), F32),
                            pltpu.VMEM((ZERO_ROWS, D_MODEL), F32),
                            pltpu.SemaphoreType.DMA((2,)),
                            pltpu.SemaphoreType.DMA]),
        out_shape=[jax.ShapeDtypeStruct((P, D_MODEL), F32),
                   jax.ShapeDtypeStruct((2, FILL_ROWS, D_MODEL), F32)],
        compiler_params=_cparams(("arbitrary",)),
    )(pad_lo, pad_n, tail, cnt, off, dst, pos, h2a, h2b)[0]


def _expert_kernel(be_ref, used_ref, x_hbm, wgu_ref, bgu_ref, wd_ref, bd_ref, y_ref, wgu_bf, wd_bf, xbuf, xsem):
    i = pl.program_id(0)
    n_used = used_ref[0]
    live = i < n_used

    def x_copy(blk):
        rows = pl.ds(pl.multiple_of(blk * MOE_BLOCK, MOE_BLOCK), MOE_BLOCK)
        return pltpu.make_async_copy(x_hbm.at[rows], xbuf.at[blk % X_RING], xsem.at[blk % X_RING])

    @pl.when(i == 0)
    def _():
        for b in range(X_RING - 1):
            @pl.when(b < n_used)
            def _(b=b):
                x_copy(b).start()

    @pl.when(i + X_RING - 1 < n_used)
    def _():
        x_copy(i + X_RING - 1).start()

    @pl.when(jnp.logical_or(i == 0, be_ref[i] != be_ref[jnp.maximum(i - 1, 0)]))
    def _():
        wgu_bf[...] = wgu_ref[0].astype(BF16)
        wd_bf[...] = wd_ref[0].astype(BF16)

    @pl.when(live)
    def _():
        x_copy(i).wait()
        x_ref = xbuf.at[i % X_RING]
        for part in range(MOE_BLOCK // EXPERT_ROWS):
            rows = slice(part * EXPERT_ROWS, (part + 1) * EXPERT_ROWS)
            x = x_ref[rows, :].astype(BF16)
            gu = jnp.dot(x, wgu_bf[...], preferred_element_type=F32) + bgu_ref[0]
            glu = jnp.minimum(gu[:, :D_FF], SWIGLU_LIMIT)
            lin = jnp.clip(gu[:, D_FF:], -SWIGLU_LIMIT, SWIGLU_LIMIT)
            act = glu * jax.nn.sigmoid(SWIGLU_ALPHA * glu) * (lin + 1.0)
            y_ref[rows, :] = jnp.dot(act.astype(BF16), wd_bf[...], preferred_element_type=F32) + bd_ref[0]

    @pl.when(jnp.logical_not(live))
    def _():
        y_ref[...] = jnp.zeros_like(y_ref)


def _experts(block_e, n_used, xs, wgu, bgu, wd, bd):
    n_blocks = block_e.shape[0]
    x_map = lambda i, be, used: (jnp.minimum(i, used[0] - 1), 0)
    return pl.pallas_call(
        _expert_kernel,
        grid_spec=pltpu.PrefetchScalarGridSpec(
            num_scalar_prefetch=2,
            grid=(n_blocks,),
            in_specs=[pl.BlockSpec(memory_space=pl.ANY),
                      pl.BlockSpec((1, D_MODEL, 2 * D_FF), lambda i, be, used: (be[i], 0, 0)),
                      pl.BlockSpec((1, 1, 2 * D_FF), lambda i, be, used: (be[i], 0, 0)),
                      pl.BlockSpec((1, D_FF, D_MODEL), lambda i, be, used: (be[i], 0, 0)),
                      pl.BlockSpec((1, 1, D_MODEL), lambda i, be, used: (be[i], 0, 0))],
            out_specs=pl.BlockSpec((MOE_BLOCK, D_MODEL), lambda i, be, used: (i, 0)),
            scratch_shapes=[pltpu.VMEM((D_MODEL, 2 * D_FF), BF16),
                            pltpu.VMEM((D_FF, D_MODEL), BF16),
                            pltpu.VMEM((X_RING, MOE_BLOCK, D_MODEL), F32),
                            pltpu.SemaphoreType.DMA((X_RING,))]),
        out_shape=jax.ShapeDtypeStruct(xs.shape, F32),
        compiler_params=_cparams(("arbitrary",)),
    )(block_e, n_used, xs, wgu, bgu, wd, bd)


def _combine_kernel(cnt_ref, off_ref, dst_ref, ys_hbm, pos_ref, gate_ref, x1_ref, gt2_ref, g_ref, o_ref,
                    ybuf, sem, *, tile0, n_tiles):
    i = pl.program_id(0)
    buf = i % 2

    def fetch(tile, b):
        def fill_copy(local_row, fill_row, size):
            return pltpu.make_async_copy(ys_hbm.at[pl.ds(fill_row, size)], ybuf.at[b, pl.ds(local_row, size)],
                                         sem.at[b])
        _tile_pieces(cnt_ref, off_ref, dst_ref, tile, ybuf.at[b], ys_hbm, sem.at[b], False, fill_copy)

    @pl.when(i == 0)
    def _():
        fetch(tile0, 0)

    @pl.when(i + 1 < n_tiles)
    def _():
        fetch(tile0 + i + 1, 1 - buf)

    _tile_wait(ybuf.at[buf], ys_hbm, sem.at[buf])

    pos = pos_ref[...]
    gate = gate_ref[...]
    moe = jnp.zeros((TM, D_MODEL), F32)
    for c in range(TILE_ROWS // SORT_CHUNK):
        col = lax.broadcasted_iota(jnp.int32, (TM, SORT_CHUNK), 1) + c * SORT_CHUNK
        w = jnp.zeros((TM, SORT_CHUNK), F32)
        for k in range(TOP_K):
            w = jnp.where(col == pos[:, k:k + 1], gate[:, k:k + 1], w)
        rows = ybuf[buf, c * SORT_CHUNK:(c + 1) * SORT_CHUNK, :]
        moe = moe + jnp.dot(w.astype(BF16), rows.astype(BF16), preferred_element_type=F32)
    o_ref[...] = x1_ref[...] + gt2_ref[0] * _rms(moe, g_ref[...])


def _combine(cnt, off, dst, ys, tile0, pos_tk, gates_tk, x1, ada3, ada_row0, bsz, L, g_post):
    T = bsz * L
    tpb = L // TM
    n_tiles = T // TM
    return pl.pallas_call(
        functools.partial(_combine_kernel, tile0=tile0, n_tiles=n_tiles),
        grid_spec=pltpu.PrefetchScalarGridSpec(
            num_scalar_prefetch=3,
            grid=(n_tiles,),
            in_specs=[pl.BlockSpec(memory_space=pl.ANY),
                      pl.BlockSpec((TM, TOP_K), lambda i, *_: (i, 0)),
                      pl.BlockSpec((TM, TOP_K), lambda i, *_: (i, 0)),
                      pl.BlockSpec((TM, D_MODEL), lambda i, *_: (i, 0)),
                      pl.BlockSpec((1, 1, D_MODEL), lambda i, *_: ((ada_row0 + i // tpb) * 6 + 5, 0, 0)),
                      pl.BlockSpec((1, D_MODEL), lambda i, *_: (0, 0))],
            out_specs=pl.BlockSpec((TM, D_MODEL), lambda i, *_: (i, 0)),
            scratch_shapes=[pltpu.VMEM((2, TILE_ROWS, D_MODEL), F32),
                            pltpu.SemaphoreType.DMA((2,))]),
        out_shape=jax.ShapeDtypeStruct((T, D_MODEL), F32),
        compiler_params=_cparams(("arbitrary",)),
    )(cnt, off, dst, ys, pos_tk, gates_tk, x1, ada3, g_post)


def kernel(x_prompt, x_sample, c_prompt, c_sample, w_ada, b_ada, g_pre_mix, g_post_mix, g_pre_ffn, g_post_ffn,
           w_in, rpb, w_fourier, g_na, g_fn, w_out, w_router, b_router, w_gate_up, b_gate_up, w_down, b_down):
    assert w_ada.shape[0] == 1, "single layer"
    groups = [(x_prompt, c_prompt), (x_sample, c_sample)]
    for x, _ in groups:
        assert x.shape[1] % ATT_TOKENS == 0 and x.shape[1] // GRID_W >= 12 and x.shape[2] == D_MODEL

    row2 = lambda a: a[0].reshape(1, -1)
    w_in0 = w_in[0]
    wq = (w_in0[:, :NA_WIDTH] * (HEAD_DIM ** -0.5 * LOG2_E)).astype(BF16)
    wkt = w_in0[:, NA_WIDTH:2 * NA_WIDTH].T.astype(BF16)
    wv = w_in0[:, 2 * NA_WIDTH:3 * NA_WIDTH].astype(BF16)
    wu = w_in0[:, 3 * NA_WIDTH:].astype(BF16)
    cs = _channel_table()
    bias = _bias_table(rpb[0])
    wf = w_fourier[0].astype(BF16)
    woa = w_out[0, :NA_WIDTH].astype(BF16)
    wof = w_out[0, NA_WIDTH:].astype(BF16)
    wr_t = w_router[0].T
    wr_hi = wr_t.astype(BF16)
    wr_lo = (wr_t - wr_hi.astype(F32)).astype(BF16)
    wr2 = jnp.concatenate([wr_hi, wr_lo], axis=0)
    br = jnp.broadcast_to(b_router[0][:, None], (N_EXPERTS, LANES))
    wgu = w_gate_up[0]
    wd = w_down[0]
    bgu = b_gate_up[0].reshape(N_EXPERTS, 1, 2 * D_FF)
    bd = b_down[0].reshape(N_EXPERTS, 1, D_MODEL)

    c_all = jnp.concatenate([c_prompt, c_sample], axis=0)
    ada3 = _ada(c_all, w_ada[0], b_ada[0]).reshape(c_all.shape[0] * 6, 1, D_MODEL)

    per_group = []
    ada_row0 = 0
    for x, _ in groups:
        bsz, L, _ = x.shape
        x2d = x.reshape(bsz * L, D_MODEL)
        q, kt4, v, u = _inproj(x2d, ada3, ada_row0, bsz, L, row2(g_pre_mix), wq, wkt, wv, wu)
        na = _attention(q.reshape(bsz, L, NA_WIDTH), kt4, v.reshape(bsz, L, NA_WIDTH), bias, row2(g_na))
        fn = _fourier(u, bsz, L, cs, wf, row2(g_fn))
        x1, h2, gate_t, pos_t, cnt = _outproj(
            na.reshape(bsz * L, NA_WIDTH), fn, x2d, ada3, ada_row0, bsz, L,
            row2(g_post_mix), row2(g_pre_ffn), woa, wof, wr2, br)
        per_group.append((x1, h2, gate_t, pos_t, cnt, bsz, L, ada_row0))
        ada_row0 += bsz

    T = sum(g[5] * g[6] for g in per_group)
    max_rows = T * TOP_K + (T // TM) * N_EXPERTS * (SUBLANES - 1) + N_EXPERTS * (MOE_BLOCK - 1)
    n_blocks = -(-max_rows // MOE_BLOCK)
    tile_cnt = jnp.concatenate([g[4][:, 0].reshape(-1, N_EXPERTS) for g in per_group], axis=0).astype(jnp.int32)
    tile_cnt = _round_up(tile_cnt, SUBLANES)
    total = jnp.sum(tile_cnt, axis=0)
    padded = ((total + MOE_BLOCK - 1) // MOE_BLOCK) * MOE_BLOCK
    pend = jnp.cumsum(padded)
    pstart = pend - padded
    run = jnp.cumsum(tile_cnt, axis=0) - tile_cnt
    tile_dst = (pstart[None, :] + run).reshape(-1)
    tile_off = (jnp.cumsum(tile_cnt, axis=1) - tile_cnt).reshape(-1)
    tile_cnt = tile_cnt.reshape(-1)
    block_row0 = jnp.arange(n_blocks, dtype=jnp.int32) * MOE_BLOCK
    block_e = jnp.minimum(jnp.sum((pend[None, :] <= block_row0[:, None]).astype(jnp.int32), axis=1), N_EXPERTS - 1)
    pad_lo = pstart + total
    pad_n = padded - total
    tail = (pend[-1:] // MOE_BLOCK).astype(jnp.int32)

    pos_all = jnp.concatenate([g[3] for g in per_group], axis=1)
    xs = _dispatch(pad_lo, pad_n, tail, tile_cnt, tile_off, tile_dst, pos_all,
                   per_group[0][1], per_group[1][1], n_blocks)
    ys = _experts(block_e, tail, xs, wgu, bgu, wd, bd)

    outs = []
    tile0 = 0
    for x1, _, gate_t, pos_t, _, bsz, L, row0 in per_group:
        out = _combine(tile_cnt, tile_off, tile_dst, ys, tile0, pos_t.T, gate_t.T, x1, ada3, row0, bsz, L,
                       row2(g_post_ffn))
        outs.append(out.reshape(bsz, L, D_MODEL))
        tile0 += bsz * L // TM
    return tuple(outs)
```
